```python
import jax, jax.numpy as jnp
from jax import lax
import numpy as np

D_MODEL = 4096
BATCH = 4
SEQ = 2048
DEPTH = 1
DEC_BATCH = 128
DEC_SEQ = 8
PAST_LEN = 16384
PAGE_SIZE = 128

MIX_WIDTH = D_MODEL
CONV_CH = MIX_WIDTH // 2
CONV_WIDTH = 31
CONV_STATE = CONV_WIDTH - 1
N_HEADS = 4
DV = (MIX_WIDTH - CONV_CH) // N_HEADS
DQK = DV // 2
CHUNK = 64
GATE_CAP = 15.0
NEG = -1e30
N_EXPERTS = 32
TOP_K = 4
D_FF = D_MODEL // 2
MOE_BLOCK = 16
SWIGLU_LIMIT = 7.0
SWIGLU_ALPHA = 1.702
EPS = 1e-6
IN_COLS = 2 * CONV_CH + 2 * N_HEADS * DQK + 2 * N_HEADS * DV + 2 * N_HEADS

kernel_name = 'hymba_conformer_mlstm_moe_adaln_step'


def _split_points():
    sizes = (CONV_CH, CONV_CH, N_HEADS * DQK, N_HEADS * DQK, N_HEADS * DV, N_HEADS * DV, N_HEADS, N_HEADS)
    pts, acc = [], 0
    for s in sizes[:-1]:
        acc += s
        pts.append(acc)
    return pts


def _rmsnorm(x, w):
    xf = x.astype(jnp.float32)
    y = xf * lax.rsqrt(jnp.mean(xf * xf, -1, keepdims=True) + EPS)
    return (y * w.astype(jnp.float32)).astype(x.dtype)


def _layernorm(x, w, b):
    xf = x.astype(jnp.float32)
    mu = jnp.mean(xf, -1, keepdims=True)
    xc = xf - mu
    y = xc * lax.rsqrt(jnp.mean(xc * xc, -1, keepdims=True) + EPS)
    return (y * w.astype(jnp.float32) + b.astype(jnp.float32)).astype(x.dtype)


def _softcap(z):
    return GATE_CAP * jnp.tanh(z / GATE_CAP)


def _causal_dwconv(u, buf, w, b):
    full = jnp.concatenate([buf.astype(u.dtype), u], axis=1)
    y = lax.conv_general_dilated(full, w[:, None, :].astype(u.dtype), window_strides=(1,), padding='VALID',
                                 dimension_numbers=('NWC', 'WIO', 'NWC'), feature_group_count=u.shape[-1])
    return y + b.astype(u.dtype), full[:, -CONV_STATE:]


def _mlstm(q, k, v, log_i, log_f, C0, n0, m0):
    B, S = q.shape[:2]
    L = min(CHUNK, S)
    nC = -(-S // L)
    pad = nC * L - S
    if pad:
        padw = lambda t, val: jnp.pad(t, ((0, 0), (0, pad)) + ((0, 0),) * (t.ndim - 2), constant_values=val)
        q, k, v, log_f = padw(q, 0.0), padw(k, 0.0), padw(v, 0.0), padw(log_f, 0.0)
        log_i = padw(log_i, NEG)

    def chunks(t):
        t = t.reshape((B, nC, L) + t.shape[2:])
        return jnp.swapaxes(jnp.moveaxis(t, 1, 0), 2, 3)

    causal = jnp.tril(jnp.ones((L, L), bool))

    def step(carry, xs):
        C, n, m = carry
        qc, kc, vc, li, lf = xs
        b = jnp.cumsum(lf, axis=-1)
        dmat = jnp.where(causal, b[..., :, None] - b[..., None, :] + li[..., None, :], -jnp.inf)
        inter = b + m[..., None]
        m_t = jnp.maximum(inter, jnp.max(dmat, -1))
        a = jnp.exp(inter - m_t)
        s = jnp.einsum('bhtd,bhsd->bhts', qc, kc) * jnp.exp(dmat - m_t[..., None])
        num = a[..., None] * jnp.einsum('bhtd,bhde->bhte', qc, C) + jnp.einsum('bhts,bhse->bhte', s, vc)
        den = a * jnp.einsum('bhtd,bhd->bht', qc, n) + jnp.sum(s, -1)
        h = num / jnp.maximum(jnp.abs(den), jnp.exp(-m_t))[..., None]
        bL = b[..., -1]
        g = bL[..., None] - b + li
        m_new = jnp.maximum(bL + m, jnp.max(g, -1))
        decay = jnp.exp(bL + m - m_new)
        wk = jnp.exp(g - m_new[..., None])
        kw = kc * wk[..., None]
        C_new = decay[..., None, None] * C + jnp.einsum('bhsd,bhse->bhde', kw, vc)
        n_new = decay[..., None] * n + jnp.sum(kw, axis=-2)
        return (C_new, n_new, m_new), h

    (C, n, m), h = lax.scan(step, (C0, n0, m0), tuple(chunks(t) for t in (q, k, v, log_i, log_f)))
    h = jnp.moveaxis(jnp.swapaxes(h, 2, 3), 0, 1).reshape(B, nC * L, N_HEADS, DV)[:, :S]
    return h, C, n, m


def _moe(h, w_router, b_router, w1, b1, w2, b2):
    T, D = h.shape
    A = T * TOP_K
    logits = h.astype(jnp.float32) @ w_router.astype(jnp.float32) + b_router.astype(jnp.float32)
    top_v, top_i = lax.top_k(logits, TOP_K)
    gates = jax.nn.softmax(top_v, axis=-1)
    flat_e = top_i.reshape(-1)
    order = jnp.argsort(flat_e)
    sorted_e = flat_e[order]
    tok = order // TOP_K
    counts = jnp.bincount(flat_e, length=N_EXPERTS).astype(jnp.int32)
    padded = (counts + MOE_BLOCK - 1) // MOE_BLOCK * MOE_BLOCK
    pad_end = jnp.cumsum(padded)
    pad_start = pad_end - padded
    start = jnp.cumsum(counts) - counts
    dest = pad_start[sorted_e] + jnp.arange(A, dtype=jnp.int32) - start[sorted_e]
    n_blocks = -(-A // MOE_BLOCK) + N_EXPERTS
    R = n_blocks * MOE_BLOCK
    xbuf = jnp.zeros((R, D), h.dtype).at[dest].set(h[tok])
    blk_start = jnp.arange(n_blocks, dtype=jnp.int32) * MOE_BLOCK
    blk_e = jnp.minimum(jnp.searchsorted(pad_end, blk_start, side='right'), N_EXPERTS - 1)
    blk_used = blk_start < pad_end[-1]

    def expert_block(args):
        xb, e, used = args
        def run():
            gu = xb @ w1[e].astype(h.dtype) + b1[e].astype(h.dtype)
            gt, up = jnp.split(gu, 2, axis=-1)
            gt = jnp.minimum(gt, SWIGLU_LIMIT)
            up = jnp.clip(up, -SWIGLU_LIMIT, SWIGLU_LIMIT)
            act = (up + 1) * gt * jax.nn.sigmoid(SWIGLU_ALPHA * gt)
            return (act @ w2[e].astype(h.dtype) + b2[e].astype(h.dtype)).astype(h.dtype)
        return lax.cond(used, run, lambda: jnp.zeros_like(xb))

    ybuf = lax.map(expert_block, (xbuf.reshape(n_blocks, MOE_BLOCK, D), blk_e, blk_used)).reshape(R, D)
    out = ybuf[dest] * gates.reshape(-1)[order][:, None].astype(h.dtype)
    return jnp.zeros_like(h).at[tok].add(out)


def _layer(x, c, conv_buf, C0, n0, m0, p):
    B, S, _ = x.shape
    f32 = jnp.float32
    mod = (jax.nn.silu(c) @ p['w_ada'] + p['b_ada']).astype(x.dtype)
    sh1, sc1, g1, sh2, sc2, g2 = jnp.split(mod[:, None, :], 6, axis=-1)
    h = _rmsnorm(x, p['norm1_pre']) * (1 + sc1) + sh1
    proj = h @ p['w_in']
    a_val, a_gate, q, k, v, o, ig, fg = jnp.split(proj, _split_points(), axis=-1)
    u = a_val * jax.nn.sigmoid(a_gate)
    u, new_buf = _causal_dwconv(u, conv_buf, p['conv_w'], p['conv_b'])
    conv_out = jax.nn.silu(_layernorm(u, p['conv_ln_w'], p['conv_ln_b']))
    q = q.astype(f32).reshape(B, S, N_HEADS, DQK)
    k = k.astype(f32).reshape(B, S, N_HEADS, DQK) * (DQK ** -0.5)
    v = v.astype(f32).reshape(B, S, N_HEADS, DV)
    gb = p['b_gates'].astype(f32)
    log_i = _softcap(ig.astype(f32) + gb[:N_HEADS])
    log_f = jax.nn.log_sigmoid(_softcap(fg.astype(f32) + gb[N_HEADS:]))
    hm, C, n, m = _mlstm(q, k, v, log_i, log_f, C0.astype(f32), n0.astype(f32), m0.astype(f32))
    hm = hm * lax.rsqrt(jnp.mean(hm * hm, -1, keepdims=True) + EPS) * p['mlstm_norm_w'].astype(f32)
    hm = hm.reshape(B, S, N_HEADS * DV) * jax.nn.sigmoid(o.astype(f32))
    mix = jnp.concatenate([conv_out, hm.astype(x.dtype)], axis=-1) @ p['w_out']
    x = x + g1 * _rmsnorm(mix, p['norm1_post'])
    h2 = _rmsnorm(x, p['norm2_pre']) * (1 + sc2) + sh2
    f = _moe(h2.reshape(B * S, D_MODEL), p['w_router'], p['b_router'], p['w1'], p['b1'], p['w2'], p['b2']).reshape(B, S, D_MODEL)
    x = x + g2 * _rmsnorm(f, p['norm2_post'])
    return x, new_buf.astype(conv_buf.dtype), C.astype(C0.dtype), n.astype(n0.dtype), m.astype(m0.dtype)


def setup_inputs(seed: int = 0) -> dict:
    key = jax.random.key(seed)
    ks = jax.random.split(key, 32)
    nrm = lambda k, shape, scale: jax.random.normal(k, shape, jnp.float32) * scale
    f_bias = jnp.linspace(3.0, 6.0, N_HEADS, dtype=jnp.float32)
    b_gates = jnp.concatenate([nrm(ks[8], (DEPTH, N_HEADS), 0.1),
                               f_bias[None, :] + nrm(ks[9], (DEPTH, N_HEADS), 0.1)], axis=-1)
    return {
        'x_prompt': nrm(ks[0], (BATCH, SEQ, D_MODEL), 1.0),
        'x_sample': nrm(ks[1], (DEC_BATCH, DEC_SEQ, D_MODEL), 1.0),
        'c_prompt': nrm(ks[2], (BATCH, D_MODEL), 1.0),
        'c_sample': nrm(ks[3], (DEC_BATCH, D_MODEL), 1.0),
        'state_conv': nrm(ks[4], (DEPTH, DEC_BATCH, CONV_STATE, CONV_CH), 0.5),
        'state_mlstm_C': nrm(ks[5], (DEPTH, DEC_BATCH, N_HEADS, DQK, DV), 0.25),
        'state_mlstm_n': nrm(ks[6], (DEPTH, DEC_BATCH, N_HEADS, DQK), 0.25),
        'state_mlstm_m': nrm(ks[7], (DEPTH, DEC_BATCH, N_HEADS), 1.0),
        'w_ada': nrm(ks[10], (DEPTH, D_MODEL, 6 * D_MODEL), 0.5 * D_MODEL ** -0.5),
        'b_ada': nrm(ks[11], (DEPTH, 6 * D_MODEL), 0.02),
        'norm1_pre': 1.0 + nrm(ks[12], (DEPTH, D_MODEL), 0.05),
        'w_in': nrm(ks[13], (DEPTH, D_MODEL, IN_COLS), D_MODEL ** -0.5),
        'b_gates': b_gates,
        'conv_w': nrm(ks[14], (DEPTH, CONV_WIDTH, CONV_CH), CONV_WIDTH ** -0.5),
        'conv_b': nrm(ks[15], (DEPTH, CONV_CH), 0.02),
        'conv_ln_w': 1.0 + nrm(ks[16], (DEPTH, CONV_CH), 0.05),
        'conv_ln_b': nrm(ks[17], (DEPTH, CONV_CH), 0.02),
        'mlstm_norm_w': 1.0 + nrm(ks[18], (DEPTH, N_HEADS, DV), 0.05),
        'w_out': nrm(ks[19], (DEPTH, MIX_WIDTH, D_MODEL), MIX_WIDTH ** -0.5),
        'norm1_post': 1.0 + nrm(ks[20], (DEPTH, D_MODEL), 0.05),
        'norm2_pre': 1.0 + nrm(ks[21], (DEPTH, D_MODEL), 0.05),
        'w_router': nrm(ks[22], (DEPTH, D_MODEL, N_EXPERTS), D_MODEL ** -0.5),
        'b_router': nrm(ks[23], (DEPTH, N_EXPERTS), 0.01),
        'w1': nrm(ks[24], (DEPTH, N_EXPERTS, D_MODEL, 2 * D_FF), D_MODEL ** -0.5),
        'b1': nrm(ks[25], (DEPTH, N_EXPERTS, 2 * D_FF), 0.01),
        'w2': nrm(ks[26], (DEPTH, N_EXPERTS, D_FF, D_MODEL), D_FF ** -0.5),
        'b2': nrm(ks[27], (DEPTH, N_EXPERTS, D_MODEL), 0.01),
        'norm2_post': 1.0 + nrm(ks[28], (DEPTH, D_MODEL), 0.05),
    }


def reference(x_prompt, x_sample, c_prompt, c_sample, state_conv, state_mlstm_C, state_mlstm_n, state_mlstm_m,
              w_ada, b_ada, norm1_pre, w_in, b_gates, conv_w, conv_b, conv_ln_w, conv_ln_b, mlstm_norm_w,
              w_out, norm1_post, norm2_pre, w_router, b_router, w1, b1, w2, b2, norm2_post):
    Bp = x_prompt.shape[0]
    xp, xs = x_prompt, x_sample
    conv_p, C_p, n_p, m_p = [], [], [], []
    conv_s, C_s, n_s, m_s = [], [], [], []
    for l in range(DEPTH):
        p = dict(w_ada=w_ada[l], b_ada=b_ada[l], norm1_pre=norm1_pre[l], w_in=w_in[l], b_gates=b_gates[l],
                 conv_w=conv_w[l], conv_b=conv_b[l], conv_ln_w=conv_ln_w[l], conv_ln_b=conv_ln_b[l],
                 mlstm_norm_w=mlstm_norm_w[l], w_out=w_out[l], norm1_post=norm1_post[l], norm2_pre=norm2_pre[l],
                 w_router=w_router[l], b_router=b_router[l], w1=w1[l], b1=b1[l], w2=w2[l], b2=b2[l],
                 norm2_post=norm2_post[l])
        z_conv = jnp.zeros((Bp, CONV_STATE, CONV_CH), x_prompt.dtype)
        z_C = jnp.zeros((Bp, N_HEADS, DQK, DV), x_prompt.dtype)
        z_n = jnp.zeros((Bp, N_HEADS, DQK), x_prompt.dtype)
        z_m = jnp.zeros((Bp, N_HEADS), x_prompt.dtype)
        xp, cb, Cn, nn, mn = _layer(xp, c_prompt, z_conv, z_C, z_n, z_m, p)
        conv_p.append(cb); C_p.append(Cn); n_p.append(nn); m_p.append(mn)
        xs, cb, Cn, nn, mn = _layer(xs, c_sample, state_conv[l], state_mlstm_C[l], state_mlstm_n[l], state_mlstm_m[l], p)
        conv_s.append(cb); C_s.append(Cn); n_s.append(nn); m_s.append(mn)
    return (xp, xs, jnp.stack(conv_p), jnp.stack(C_p), jnp.stack(n_p), jnp.stack(m_p),
            jnp.stack(conv_s), jnp.stack(C_s), jnp.stack(n_s), jnp.stack(m_s))
```

```python
import functools

import jax
import jax.numpy as jnp
from jax import lax
from jax.experimental import pallas as pl
from jax.experimental.pallas import tpu as pltpu

F32 = jnp.float32
BF16 = jnp.bfloat16
I32 = jnp.int32

EPS = 1e-6
GATE_CAP = 15.0
TOP_K = 4
SWIGLU_LIMIT = 7.0
SWIGLU_ALPHA = 1.702
NEG_BIG = -1e30

LANES = 128
SUBLANES = 8
VMEM_LIMIT = 56 * 1024 * 1024
MLSTM_CHUNK = 256
MOE_ROWS = 256


def _params(sem):
    return pltpu.CompilerParams(dimension_semantics=sem, vmem_limit_bytes=VMEM_LIMIT)


def _sigmoid(x):
    return 1.0 / (1.0 + jnp.exp(-x))


def _pick(n, prefs):
    for p in prefs:
        if n % p == 0:
            return p
    return n


def _ada_kernel(c_ref, w_ref, b_ref, o_ref):
    c = c_ref[...]
    s = (c * _sigmoid(c)).astype(BF16)
    o_ref[...] = jnp.dot(s, w_ref[...].astype(BF16), preferred_element_type=F32) + b_ref[...]


def _ada(c_all, w_ada, b_ada, l):
    mp, d = c_all.shape
    n = w_ada.shape[-1]
    tn = _pick(n, (512, 256, 128))
    return pl.pallas_call(
        _ada_kernel,
        grid=(n // tn,),
        in_specs=[pl.BlockSpec((mp, d), lambda j: (0, 0)),
                  pl.BlockSpec((None, d, tn), lambda j: (l, 0, j)),
                  pl.BlockSpec((None, 1, tn), lambda j: (l, 0, j))],
        out_specs=pl.BlockSpec((mp, tn), lambda j: (0, j)),
        out_shape=jax.ShapeDtypeStruct((mp, n), F32),
        compiler_params=_params(("parallel",)),
        name="ada",
    )(c_all, w_ada, b_ada.reshape(b_ada.shape[0], 1, n))


def _prenorm_kernel(x_ref, sc_ref, sh_ref, nw_ref, wg_ref, bg_ref, h_ref, g_ref, *, n_heads):
    x = x_ref[...]
    y = x * lax.rsqrt(jnp.mean(x * x, -1, keepdims=True) + EPS) * nw_ref[...]
    h = y * (1.0 + sc_ref[...]) + sh_ref[...]
    h_ref[...] = h.astype(BF16)
    z = jnp.dot(h, wg_ref[...], preferred_element_type=F32, precision=lax.Precision.HIGHEST) + bg_ref[...]
    cap = GATE_CAP * jnp.tanh(z / GATE_CAP)
    logsig = jnp.minimum(cap, 0.0) - jnp.log(1.0 + jnp.exp(-jnp.abs(cap)))
    lane = lax.broadcasted_iota(I32, z.shape, 1)
    g_ref[...] = jnp.where(lane < n_heads, cap, logsig)


def _mod_spec(arr, ts):
    d = arr.shape[-1]
    if arr.shape[1] == 1:
        return pl.BlockSpec((None, 1, d), lambda b, i, *_: (b, 0, 0))
    return pl.BlockSpec((None, ts, d), lambda b, i, *_: (b, i, 0))


def _prenorm(x3, sc3, sh3, nw, wg, bg, n_heads):
    b, s, d = x3.shape
    ts = _pick(s, (256, 128, 64, 32, 16, 8))
    ns = s // ts
    kern = functools.partial(_prenorm_kernel, n_heads=n_heads)
    return pl.pallas_call(
        kern,
        grid=(b, ns),
        in_specs=[pl.BlockSpec((None, ts, d), lambda bb, i: (bb, i, 0)),
                  _mod_spec(sc3, ts), _mod_spec(sh3, ts),
                  pl.BlockSpec((1, d), lambda bb, i: (0, 0)),
                  pl.BlockSpec((d, LANES), lambda bb, i: (0, 0)),
                  pl.BlockSpec((1, LANES), lambda bb, i: (0, 0))],
        out_specs=[pl.BlockSpec((ts, d), lambda bb, i: (bb * ns + i, 0)),
                   pl.BlockSpec((ts, LANES), lambda bb, i: (bb * ns + i, 0))],
        out_shape=[jax.ShapeDtypeStruct((b * s, d), BF16),
                   jax.ShapeDtypeStruct((b * s, LANES), F32)],
        compiler_params=_params(("parallel", "parallel")),
        name="prenorm",
    )(x3, sc3, sh3, nw, wg, bg)


def _inproj_kernel(a_ref, w_ref, o_ref, wbf_ref):
    @pl.when(pl.program_id(1) == 0)
    def _():
        wbf_ref[...] = w_ref[...].astype(BF16)

    o_ref[...] = jnp.dot(a_ref[...], wbf_ref[...], preferred_element_type=F32)


def _inproj(h, w_in, l, n_main):
    m, d = h.shape
    tn = _pick(n_main, (512, 256, 128))
    tm = _pick(m, (512, 256, 128, 64, 32, 16, 8))
    return pl.pallas_call(
        _inproj_kernel,
        grid=(n_main // tn, m // tm),
        in_specs=[pl.BlockSpec((tm, d), lambda j, i: (i, 0)),
                  pl.BlockSpec((None, d, tn), lambda j, i: (l, 0, j))],
        out_specs=pl.BlockSpec((tm, tn), lambda j, i: (i, j)),
        out_shape=jax.ShapeDtypeStruct((m, n_main), F32),
        scratch_shapes=[pltpu.VMEM((d, tn), BF16)],
        compiler_params=_params(("arbitrary", "arbitrary")),
        name="inproj",
    )(h, w_in)


CONV_HALO = 32


def _conv_taps(f_ref, w_ref, base, rows, c0, cc, width):
    acc = jnp.zeros((rows, cc), F32)
    for ph in range(SUBLANES):
        if ph >= width:
            break
        n_al = (width - 1 - ph) // SUBLANES + 1
        gb = f_ref[base + ph: base + ph + rows + SUBLANES * (n_al - 1), c0:c0 + cc]
        for a in range(n_al):
            j = SUBLANES * a + ph
            acc = acc + w_ref[j:j + 1, c0:c0 + cc] * gb[SUBLANES * a:SUBLANES * a + rows]
    return acc


def _ln_swish(y, lw, lb):
    mu = jnp.mean(y, -1, keepdims=True)
    yc = y - mu
    yn = yc * lax.rsqrt(jnp.mean(yc * yc, -1, keepdims=True) + EPS) * lw + lb
    return yn * _sigmoid(yn)


def _conv_seq_kernel(av_ref, ag_ref, st_ref, w_ref, cb_ref, lw_ref, lb_ref, o_ref, ns_ref, f_ref, y_ref,
                     *, ts, width, rb, cc):
    i = pl.program_id(1)
    off = CONV_HALO - (width - 1)
    ch = f_ref.shape[1]

    @pl.when(i == 0)
    def _():
        f_ref[0:CONV_HALO, :] = jnp.zeros((CONV_HALO, ch), F32)
        f_ref[off:CONV_HALO, :] = st_ref[...]

    f_ref[CONV_HALO:CONV_HALO + ts, :] = av_ref[...] * _sigmoid(ag_ref[...])
    for r0 in range(0, ts, rb):
        for c0 in range(0, ch, cc):
            acc = _conv_taps(f_ref, w_ref, r0 + off, rb, c0, cc, width)
            y_ref[r0:r0 + rb, c0:c0 + cc] = acc + cb_ref[:, c0:c0 + cc]
    o_ref[...] = _ln_swish(y_ref[...], lw_ref[...], lb_ref[...]).astype(o_ref.dtype)

    @pl.when(i == pl.num_programs(1) - 1)
    def _():
        ns_ref[...] = f_ref[ts + off:ts + CONV_HALO, :]

    f_ref[0:CONV_HALO, :] = f_ref[ts:ts + CONV_HALO, :]


def _conv_seq(proj, state, w, cb, lw, lb, bsz, s):
    width, ch = w.shape
    ts = _pick(s, (128, 64, 32))
    ns = s // ts
    rb = _pick(ts, (64, 32))
    cc = _pick(ch, (256, 128))
    kern = functools.partial(_conv_seq_kernel, ts=ts, width=width, rb=rb, cc=cc)
    vec = pl.BlockSpec((1, ch), lambda b, i: (0, 0))
    return pl.pallas_call(
        kern,
        grid=(bsz, ns),
        in_specs=[pl.BlockSpec((ts, ch), lambda b, i: (b * ns + i, 0)),
                  pl.BlockSpec((ts, ch), lambda b, i: (b * ns + i, 1)),
                  pl.BlockSpec((None, width - 1, ch), lambda b, i: (b, 0, 0)),
                  pl.BlockSpec((width, ch), lambda b, i: (0, 0)),
                  vec, vec, vec],
        out_specs=[pl.BlockSpec((ts, ch), lambda b, i: (b * ns + i, 0)),
                   pl.BlockSpec((None, width - 1, ch), lambda b, i: (b, 0, 0))],
        out_shape=[jax.ShapeDtypeStruct((bsz * s, ch), BF16),
                   jax.ShapeDtypeStruct((bsz, width - 1, ch), F32)],
        scratch_shapes=[pltpu.VMEM((CONV_HALO + ts, ch), F32), pltpu.VMEM((ts, ch), F32)],
        compiler_params=_params(("arbitrary", "arbitrary")),
        name="conv_seq",
    )(proj, proj, state, w, cb, lw, lb)


def _conv_step_kernel(av_ref, ag_ref, st_ref, w_ref, cb_ref, lw_ref, lb_ref, o_ref, ns_ref, f_ref, y_ref,
                      *, nb, s, width, cc):
    nst = width - 1
    ch = f_ref.shape[1]

    def body(q, carry):
        r = pl.multiple_of(q * s, s)
        f_ref[0:nst, :] = st_ref[q]
        f_ref[nst:nst + s, :] = av_ref[pl.ds(r, s), :] * _sigmoid(ag_ref[pl.ds(r, s), :])
        for c0 in range(0, ch, cc):
            acc = _conv_taps(f_ref, w_ref, 0, s, c0, cc, width)
            y_ref[pl.ds(r, s), c0:c0 + cc] = acc + cb_ref[:, c0:c0 + cc]
        ns_ref[q] = f_ref[s:s + nst, :]
        return carry

    lax.fori_loop(0, nb, body, 0)
    o_ref[...] = _ln_swish(y_ref[...], lw_ref[...], lb_ref[...]).astype(o_ref.dtype)


def _conv_step(proj, state, w, cb, lw, lb, bsz, s):
    width, ch = w.shape
    assert s % SUBLANES == 0
    nb = _pick(bsz, (16, 8, 4, 2, 1))
    cc = _pick(ch, (512, 256, 128))
    kern = functools.partial(_conv_step_kernel, nb=nb, s=s, width=width, cc=cc)
    vec = pl.BlockSpec((1, ch), lambda b: (0, 0))
    frows = -(-(width - 1 + s) // SUBLANES) * SUBLANES
    return pl.pallas_call(
        kern,
        grid=(bsz // nb,),
        in_specs=[pl.BlockSpec((nb * s, ch), lambda b: (b, 0)),
                  pl.BlockSpec((nb * s, ch), lambda b: (b, 1)),
                  pl.BlockSpec((nb, width - 1, ch), lambda b: (b, 0, 0)),
                  pl.BlockSpec((width, ch), lambda b: (0, 0)),
                  vec, vec, vec],
        out_specs=[pl.BlockSpec((nb * s, ch), lambda b: (b, 0)),
                   pl.BlockSpec((nb, width - 1, ch), lambda b: (b, 0, 0))],
        out_shape=[jax.ShapeDtypeStruct((bsz * s, ch), BF16),
                   jax.ShapeDtypeStruct((bsz, width - 1, ch), F32)],
        scratch_shapes=[pltpu.VMEM((frows, ch), F32), pltpu.VMEM((nb * s, ch), F32)],
        compiler_params=_params(("parallel",)),
        name="conv_step",
    )(proj, proj, state, w, cb, lw, lb)


def _mlstm_kernel(q_ref, k_ref, v_ref, o_ref, g_ref, gt_ref, c0_ref, n0_ref, m0_ref, nw_ref,
                  hm_ref, c_out, n_out, m_out, c_s, n_s, m_s, mall_s, *, n_heads, scale):
    hd = pl.program_id(1)
    c = pl.program_id(2)
    last = pl.num_programs(2) - 1
    ln = q_ref.shape[0]

    @pl.when(c == 0)
    def _():
        c_s[...] = c0_ref[...]
        n_s[...] = n0_ref[...]
        m0 = m0_ref[...]
        lane0 = lax.broadcasted_iota(I32, m0.shape, 1)
        m_s[...] = jnp.sum(jnp.where(lane0 == hd, m0, 0.0), axis=1, keepdims=True)

    @pl.when(jnp.logical_and(c == 0, hd == 0))
    def _():
        mall_s[...] = jnp.zeros(mall_s.shape, F32)

    g = g_ref[...]
    lane = lax.broadcasted_iota(I32, g.shape, 1)
    li_c = jnp.sum(jnp.where(lane == hd, g, 0.0), axis=1, keepdims=True)
    lf_c = jnp.sum(jnp.where(lane == hd + n_heads, g, 0.0), axis=1, keepdims=True)
    gt = gt_ref[...]
    sub = lax.broadcasted_iota(I32, gt.shape, 0)
    li_r = jnp.sum(jnp.where(sub == hd, gt, 0.0), axis=0, keepdims=True)
    lf_r = jnp.sum(jnp.where(sub == hd + n_heads, gt, 0.0), axis=0, keepdims=True)

    row = lax.broadcasted_iota(I32, (ln, ln), 0)
    col = lax.broadcasted_iota(I32, (ln, ln), 1)
    tri = row >= col
    b_c = jnp.sum(jnp.where(tri, lf_r, 0.0), axis=1, keepdims=True)
    b_r = jnp.sum(jnp.where(row <= col, lf_c, 0.0), axis=0, keepdims=True)
    b_l = jnp.sum(lf_r, axis=1, keepdims=True)
    m_prev = m_s[...]

    dmat = jnp.where(tri, b_c - b_r + li_r, NEG_BIG)
    inter = b_c + m_prev
    m_t = jnp.maximum(inter, jnp.max(dmat, axis=1, keepdims=True))
    a = jnp.exp(inter - m_t)

    q = q_ref[...]
    k = k_ref[...] * scale
    qb = q.astype(BF16)
    vb = v_ref[...].astype(BF16)
    cst = c_s[...]
    nst = n_s[...]
    s = lax.dot_general(qb, k.astype(BF16), (((1,), (1,)), ((), ())), preferred_element_type=F32)
    s = s * jnp.exp(dmat - m_t)
    num = a * jnp.dot(qb, cst.astype(BF16), preferred_element_type=F32) \
        + jnp.dot(s.astype(BF16), vb, preferred_element_type=F32)
    den = a * jnp.sum(q * nst, axis=1, keepdims=True) + jnp.sum(s, axis=1, keepdims=True)
    h = num / jnp.maximum(jnp.abs(den), jnp.exp(-m_t))
    hn = h * lax.rsqrt(jnp.mean(h * h, -1, keepdims=True) + EPS) * nw_ref[...]
    hm_ref[...] = (hn * _sigmoid(o_ref[...])).astype(hm_ref.dtype)

    g_r = b_l - b_r + li_r
    g_c = b_l - b_c + li_c
    m_new = jnp.maximum(b_l + m_prev, jnp.max(g_r, axis=1, keepdims=True))
    decay = jnp.exp(b_l + m_prev - m_new)
    kw = k * jnp.exp(g_c - m_new)
    c_new = decay * cst + lax.dot_general(kw.astype(BF16), vb, (((0,), (0,)), ((), ())),
                                          preferred_element_type=F32)
    n_new = decay * nst + jnp.sum(kw, axis=0, keepdims=True)
    c_s[...] = c_new
    n_s[...] = n_new
    m_s[...] = m_new

    @pl.when(c == last)
    def _():
        c_out[...] = c_new
        n_out[...] = n_new
        lane_m = lax.broadcasted_iota(I32, mall_s.shape, 1)
        mall = jnp.where(lane_m == hd, m_new, mall_s[...])
        mall_s[...] = mall
        m_out[...] = mall


def _mlstm(proj, g, c0, n0, m0, norm_w, bsz, s, col_q):
    _, n_heads, dqk, dv = c0.shape
    ln = s if s <= MLSTM_CHUNK else MLSTM_CHUNK
    assert s % ln == 0 and ln % SUBLANES == 0
    nc = s // ln
    assert col_q % dqk == 0 and (col_q + 2 * n_heads * dqk) % dv == 0
    qb0 = col_q // dqk
    kb0 = qb0 + n_heads
    vb0 = (col_q + 2 * n_heads * dqk) // dv
    ob0 = vb0 + n_heads
    gt3 = g[:, :2 * n_heads].reshape(bsz * nc, ln, 2 * n_heads).transpose(0, 2, 1)
    kern = functools.partial(_mlstm_kernel, n_heads=n_heads, scale=dqk ** -0.5)
    rowblk = lambda b, h, c: b * nc + c
    out = pl.pallas_call(
        kern,
        grid=(bsz, n_heads, nc),
        in_specs=[pl.BlockSpec((ln, dqk), lambda b, h, c: (rowblk(b, h, c), qb0 + h)),
                  pl.BlockSpec((ln, dqk), lambda b, h, c: (rowblk(b, h, c), kb0 + h)),
                  pl.BlockSpec((ln, dv), lambda b, h, c: (rowblk(b, h, c), vb0 + h)),
                  pl.BlockSpec((ln, dv), lambda b, h, c: (rowblk(b, h, c), ob0 + h)),
                  pl.BlockSpec((ln, LANES), lambda b, h, c: (rowblk(b, h, c), 0)),
                  pl.BlockSpec((None, 2 * n_heads, ln), lambda b, h, c: (rowblk(b, h, c), 0, 0)),
                  pl.BlockSpec((None, None, dqk, dv), lambda b, h, c: (b, h, 0, 0)),
                  pl.BlockSpec((None, 1, dqk), lambda b, h, c: (b * n_heads + h, 0, 0)),
                  pl.BlockSpec((None, 1, n_heads), lambda b, h, c: (b, 0, 0)),
                  pl.BlockSpec((None, 1, dv), lambda b, h, c: (h, 0, 0))],
        out_specs=[pl.BlockSpec((ln, dv), lambda b, h, c: (rowblk(b, h, c), h)),
                   pl.BlockSpec((None, None, dqk, dv), lambda b, h, c: (b, h, 0, 0)),
                   pl.BlockSpec((None, 1, dqk), lambda b, h, c: (b * n_heads + h, 0, 0)),
                   pl.BlockSpec((None, 1, n_heads), lambda b, h, c: (b, 0, 0))],
        out_shape=[jax.ShapeDtypeStruct((bsz * s, n_heads * dv), BF16),
                   jax.ShapeDtypeStruct((bsz, n_heads, dqk, dv), F32),
                   jax.ShapeDtypeStruct((bsz * n_heads, 1, dqk), F32),
                   jax.ShapeDtypeStruct((bsz, 1, n_heads), F32)],
        scratch_shapes=[pltpu.VMEM((dqk, dv), F32), pltpu.VMEM((1, dqk), F32),
                        pltpu.VMEM((1, 1), F32), pltpu.VMEM((1, n_heads), F32)],
        compiler_params=_params(("arbitrary", "arbitrary", "arbitrary")),
        name="mlstm",
    )(proj, proj, proj, proj, g, gt3, c0, n0.reshape(bsz * n_heads, 1, dqk), m0.reshape(bsz, 1, n_heads),
      norm_w.reshape(n_heads, 1, dv))
    hm, c_new, n_new, m_new = out
    return hm, c_new, n_new.reshape(bsz, n_heads, dqk), m_new.reshape(bsz, n_heads)


def _post_kernel(co_ref, hm_ref, w_ref, x_ref, g1_ref, sc2_ref, sh2_ref, n1_ref, n2_ref, wr_ref, br_ref,
                 x1_ref, h2_ref, ti_ref, tg_ref, acc_ref, *, nk_half):
    k = pl.program_id(2)

    @pl.when(k == 0)
    def _():
        acc_ref[...] = jnp.zeros(acc_ref.shape, F32)

    wb = w_ref[...].astype(BF16)

    @pl.when(k < nk_half)
    def _():
        acc_ref[...] += jnp.dot(co_ref[...], wb, preferred_element_type=F32)

    @pl.when(k >= nk_half)
    def _():
        acc_ref[...] += jnp.dot(hm_ref[...], wb, preferred_element_type=F32)

    @pl.when(k == pl.num_programs(2) - 1)
    def _():
        mix = acc_ref[...]
        mn = mix * lax.rsqrt(jnp.mean(mix * mix, -1, keepdims=True) + EPS) * n1_ref[...]
        x1 = x_ref[...] + g1_ref[...] * mn
        x1_ref[...] = x1
        y2 = x1 * lax.rsqrt(jnp.mean(x1 * x1, -1, keepdims=True) + EPS) * n2_ref[...]
        h2 = y2 * (1.0 + sc2_ref[...]) + sh2_ref[...]
        h2_ref[...] = h2
        logits = jnp.dot(h2, wr_ref[...], preferred_element_type=F32,
                         precision=lax.Precision.HIGHEST) + br_ref[...]
        lane = lax.broadcasted_iota(I32, logits.shape, 1)
        idx_out = jnp.zeros(logits.shape, I32)
        val_out = jnp.zeros(logits.shape, F32)
        top = None
        den = jnp.zeros((logits.shape[0], 1), F32)
        for r in range(TOP_K):
            mx = jnp.max(logits, axis=1, keepdims=True)
            ix = jnp.min(jnp.where(logits == mx, lane, LANES), axis=1, keepdims=True)
            if top is None:
                top = mx
            e = jnp.exp(mx - top)
            den = den + e
            idx_out = jnp.where(lane == r, ix, idx_out)
            val_out = jnp.where(lane == r, e, val_out)
            logits = jnp.where(lane == ix, NEG_BIG * 2.0, logits)
        ti_ref[...] = idx_out
        tg_ref[...] = val_out / den


def _post(co, hm, w_out, x3, g1, sc2, sh2, n1, n2, wr, br, l):
    b, s, d = x3.shape
    kc = co.shape[1]
    assert hm.shape[1] == kc and w_out.shape[1] == 2 * kc
    ts = _pick(s, (256, 128, 64, 32, 16, 8) if g1.shape[1] == 1 else (128, 64, 32, 16, 8))
    ns = s // ts
    tk = _pick(kc, (512, 256, 128))
    nkh = kc // tk
    kern = functools.partial(_post_kernel, nk_half=nkh)
    vec = pl.BlockSpec((1, d), lambda bb, i, k: (0, 0))
    rows = lambda bb, i, k: (bb * ns + i, 0)
    return pl.pallas_call(
        kern,
        grid=(b, ns, 2 * nkh),
        in_specs=[pl.BlockSpec((ts, tk), lambda bb, i, k: (bb * ns + i, jnp.minimum(k, nkh - 1))),
                  pl.BlockSpec((ts, tk), lambda bb, i, k: (bb * ns + i, jnp.maximum(k - nkh, 0))),
                  pl.BlockSpec((None, tk, d), lambda bb, i, k: (l, k, 0)),
                  pl.BlockSpec((None, ts, d), lambda bb, i, k: (bb, i, 0)),
                  _mod_spec(g1, ts), _mod_spec(sc2, ts), _mod_spec(sh2, ts),
                  vec, vec,
                  pl.BlockSpec((d, LANES), lambda bb, i, k: (0, 0)),
                  pl.BlockSpec((1, LANES), lambda bb, i, k: (0, 0))],
        out_specs=[pl.BlockSpec((ts, d), rows), pl.BlockSpec((ts, d), rows),
                   pl.BlockSpec((ts, LANES), rows), pl.BlockSpec((ts, LANES), rows)],
        out_shape=[jax.ShapeDtypeStruct((b * s, d), F32), jax.ShapeDtypeStruct((b * s, d), F32),
                   jax.ShapeDtypeStruct((b * s, LANES), I32), jax.ShapeDtypeStruct((b * s, LANES), F32)],
        scratch_shapes=[pltpu.VMEM((ts, d), F32)],
        compiler_params=_params(("arbitrary", "arbitrary", "arbitrary")),
        name="post",
    )(co, hm, w_out, x3, g1, sc2, sh2, n1, n2, wr, br)


def _row_copy(src_hbm, dst_ref, sem, src_row, dst_row):
    return pltpu.make_async_copy(src_hbm.at[pl.ds(src_row, 1), :], dst_ref.at[pl.ds(dst_row, 1), :], sem)


def _gather_kernel(idx_ref, src_hbm, o_ref, buf_ref, sem):
    n = buf_ref.shape[0]

    def start(r, carry):
        _row_copy(src_hbm, buf_ref, sem, idx_ref[0, r], r).start()
        return carry

    def wait(r, carry):
        _row_copy(src_hbm, buf_ref, sem, 0, r).wait()
        return carry

    lax.fori_loop(0, n, start, 0)
    lax.fori_loop(0, n, wait, 0)
    o_ref[...] = buf_ref[...].astype(o_ref.dtype)


def _gather_rows(src, row_tok, tm):
    _, d = src.shape
    nt = row_tok.shape[0] // tm
    return pl.pallas_call(
        _gather_kernel,
        grid=(nt,),
        in_specs=[pl.BlockSpec((None, 1, tm), lambda i: (i, 0, 0), memory_space=pltpu.SMEM),
                  pl.BlockSpec(memory_space=pl.ANY)],
        out_specs=pl.BlockSpec((tm, d), lambda i: (i, 0)),
        out_shape=jax.ShapeDtypeStruct((nt * tm, d), BF16),
        scratch_shapes=[pltpu.VMEM((tm, d), F32), pltpu.SemaphoreType.DMA(())],
        compiler_params=_params(("arbitrary",)),
        name="moe_gather",
    )(row_tok.reshape(nt, 1, tm), src)


def _new_expert(te_ref, i):
    return jnp.logical_or(i == 0, te_ref[i] != te_ref[jnp.maximum(i - 1, 0)])


def _gmm1_kernel(te_ref, tu_ref, x_ref, wg_ref, wu_ref, bg_ref, bu_ref, o_ref, wgb_ref, wub_ref):
    i = pl.program_id(1)
    used = tu_ref[i] == 1

    @pl.when(jnp.logical_and(used, _new_expert(te_ref, i)))
    def _():
        wgb_ref[...] = wg_ref[...].astype(BF16)
        wub_ref[...] = wu_ref[...].astype(BF16)

    @pl.when(used)
    def _():
        x = x_ref[...]
        gt = jnp.dot(x, wgb_ref[...], preferred_element_type=F32) + bg_ref[...]
        up = jnp.dot(x, wub_ref[...], preferred_element_type=F32) + bu_ref[...]
        gt = jnp.minimum(gt, SWIGLU_LIMIT)
        up = jnp.clip(up, -SWIGLU_LIMIT, SWIGLU_LIMIT)
        act = (up + 1.0) * gt * _sigmoid(SWIGLU_ALPHA * gt)
        o_ref[...] = act.astype(o_ref.dtype)

    @pl.when(jnp.logical_not(used))
    def _():
        o_ref[...] = jnp.zeros(o_ref.shape, o_ref.dtype)


def _gmm1(xs, w1, b1, tile_e, tile_u, l, tm):
    rows, d = xs.shape
    dff = w1.shape[-1] // 2
    nt = rows // tm
    tn = _pick(dff, (512, 256, 128))
    nj = dff // tn
    b1r = b1.reshape(b1.shape[0], b1.shape[1], 1, 2 * dff)
    grid_spec = pltpu.PrefetchScalarGridSpec(
        num_scalar_prefetch=2,
        grid=(nj, nt),
        in_specs=[pl.BlockSpec((tm, d), lambda j, i, te, tu: (i, 0)),
                  pl.BlockSpec((None, None, d, tn), lambda j, i, te, tu: (l, te[i], 0, j)),
                  pl.BlockSpec((None, None, d, tn), lambda j, i, te, tu: (l, te[i], 0, nj + j)),
                  pl.BlockSpec((None, None, 1, tn), lambda j, i, te, tu: (l, te[i], 0, j)),
                  pl.BlockSpec((None, None, 1, tn), lambda j, i, te, tu: (l, te[i], 0, nj + j))],
        out_specs=pl.BlockSpec((tm, tn), lambda j, i, te, tu: (i, j)),
        scratch_shapes=[pltpu.VMEM((d, tn), BF16), pltpu.VMEM((d, tn), BF16)])
    return pl.pallas_call(
        _gmm1_kernel,
        grid_spec=grid_spec,
        out_shape=jax.ShapeDtypeStruct((rows, dff), BF16),
        compiler_params=_params(("arbitrary", "arbitrary")),
        name="moe_gmm1",
    )(tile_e, tile_u, xs, w1, w1, b1r, b1r)


def _gmm2_kernel(te_ref, tu_ref, a_ref, w_ref, b_ref, o_ref, wb_ref):
    i = pl.program_id(1)
    used = tu_ref[i] == 1

    @pl.when(jnp.logical_and(used, _new_expert(te_ref, i)))
    def _():
        wb_ref[...] = w_ref[...].astype(BF16)

    @pl.when(used)
    def _():
        o_ref[...] = jnp.dot(a_ref[...], wb_ref[...], preferred_element_type=F32) + b_ref[...]

    @pl.when(jnp.logical_not(used))
    def _():
        o_ref[...] = jnp.zeros(o_ref.shape, o_ref.dtype)


def _gmm2(act, w2, b2, tile_e, tile_u, l, tm):
    rows, dff = act.shape
    d = w2.shape[-1]
    nt = rows // tm
    tn = _pick(d, (1024, 512, 256, 128))
    b2r = b2.reshape(b2.shape[0], b2.shape[1], 1, d)
    grid_spec = pltpu.PrefetchScalarGridSpec(
        num_scalar_prefetch=2,
        grid=(d // tn, nt),
        in_specs=[pl.BlockSpec((tm, dff), lambda j, i, te, tu: (i, 0)),
                  pl.BlockSpec((None, None, dff, tn), lambda j, i, te, tu: (l, te[i], 0, j)),
                  pl.BlockSpec((None, None, 1, tn), lambda j, i, te, tu: (l, te[i], 0, j))],
        out_specs=pl.BlockSpec((tm, tn), lambda j, i, te, tu: (i, j)),
        scratch_shapes=[pltpu.VMEM((dff, tn), BF16)])
    return pl.pallas_call(
        _gmm2_kernel,
        grid_spec=grid_spec,
        out_shape=jax.ShapeDtypeStruct((rows, d), F32),
        compiler_params=_params(("arbitrary", "arbitrary")),
        name="moe_gmm2",
    )(tile_e, tile_u, act, w2, b2r)


def _combine_kernel(pos_ref, y_hbm, tg_ref, x1_ref, g2_ref, nw_ref, o_ref, buf_ref, sem, *, tt):
    def start(r, carry):
        for kk in range(TOP_K):
            _row_copy(y_hbm, buf_ref.at[kk], sem, pos_ref[0, r * TOP_K + kk], r).start()
        return carry

    def wait(r, carry):
        for kk in range(TOP_K):
            _row_copy(y_hbm, buf_ref.at[kk], sem, 0, r).wait()
        return carry

    lax.fori_loop(0, tt, start, 0)
    lax.fori_loop(0, tt, wait, 0)
    gates = tg_ref[...]
    lane = lax.broadcasted_iota(I32, gates.shape, 1)
    f = jnp.zeros(o_ref.shape, F32)
    for kk in range(TOP_K):
        gk = jnp.sum(jnp.where(lane == kk, gates, 0.0), axis=1, keepdims=True)
        f = f + gk * buf_ref[kk]
    fn = f * lax.rsqrt(jnp.mean(f * f, -1, keepdims=True) + EPS) * nw_ref[...]
    o_ref[...] = x1_ref[...] + g2_ref[...] * fn


def _combine(y, pos, tg, x1, g2, nw, bsz, s):
    d = x1.shape[1]
    tt = _pick(s, (128, 64, 32, 16, 8))
    ns = s // tt
    nt = bsz * ns
    kern = functools.partial(_combine_kernel, tt=tt)
    rows = lambda b, i: (b * ns + i, 0)
    return pl.pallas_call(
        kern,
        grid=(bsz, ns),
        in_specs=[pl.BlockSpec((None, 1, tt * TOP_K), lambda b, i: (b * ns + i, 0, 0), memory_space=pltpu.SMEM),
                  pl.BlockSpec(memory_space=pl.ANY),
                  pl.BlockSpec((tt, LANES), rows),
                  pl.BlockSpec((tt, d), rows),
                  _mod_spec(g2, tt),
                  pl.BlockSpec((1, d), lambda b, i: (0, 0))],
        out_specs=pl.BlockSpec((tt, d), rows),
        out_shape=jax.ShapeDtypeStruct((bsz * s, d), F32),
        scratch_shapes=[pltpu.VMEM((TOP_K, tt, d), F32), pltpu.SemaphoreType.DMA(())],
        compiler_params=_params(("arbitrary", "arbitrary")),
        name="moe_combine",
    )(pos.reshape(nt, 1, tt * TOP_K), y, tg, x1, g2, nw)


def _route(top_i, n_experts, tm):
    t = top_i.shape[0]
    a = t * TOP_K
    flat_e = top_i.reshape(a)
    onehot = (flat_e[:, None] == jnp.arange(n_experts, dtype=I32)[None, :]).astype(I32)
    rank = jnp.sum((jnp.cumsum(onehot, axis=0) - onehot) * onehot, axis=1)
    counts = jnp.sum(onehot, axis=0)
    padded = (counts + tm - 1) // tm * tm
    pad_end = jnp.cumsum(padded)
    pad_start = pad_end - padded
    dest = pad_start[flat_e] + rank
    nt = a // tm + n_experts
    tile_start = jnp.arange(nt, dtype=I32) * tm
    tile_u = (tile_start < pad_end[-1]).astype(I32)
    n_used = jnp.sum(tile_u)
    tile_e = jnp.minimum(jnp.searchsorted(pad_end, tile_start, side='right'), n_experts - 1).astype(I32)
    tile_e = jnp.where(tile_u == 1, tile_e, tile_e[jnp.maximum(n_used - 1, 0)])
    row_tok = jnp.zeros((nt * tm,), I32).at[dest].set(jnp.arange(a, dtype=I32) // TOP_K)
    return dest.reshape(t, TOP_K), row_tok, tile_e, tile_u


def _mixer(x3, mods, conv_buf, c0, n0, m0, p, l, per_row):
    b, s, d = x3.shape
    n_heads = c0.shape[1]
    n_main = p['w_in'].shape[-1] - 2 * n_heads
    ch = p['conv_w'].shape[-1]
    if per_row:
        mods = [jnp.repeat(mm, s, axis=0)[None] for mm in mods]
        xv = x3.reshape(1, b * s, d)
    else:
        mods = [mm[:, None, :] for mm in mods]
        xv = x3
    sh1, sc1, g1, sh2, sc2, g2 = mods
    h, g = _prenorm(xv, sc1, sh1, p['norm1_pre'], p['w_gate'], p['b_gate'], n_heads)
    proj = _inproj(h, p['w_in'], l, n_main)
    if per_row:
        co, new_buf = _conv_step(proj, conv_buf, p['conv_w'], p['conv_b'], p['conv_ln_w'], p['conv_ln_b'], b, s)
    else:
        co, new_buf = _conv_seq(proj, conv_buf, p['conv_w'], p['conv_b'], p['conv_ln_w'], p['conv_ln_b'], b, s)
    hm, c_new, n_new, m_new = _mlstm(proj, g, c0, n0, m0, p['mlstm_norm_w'], b, s, 2 * ch)
    x1, h2, ti, tg = _post(co, hm, p['w_out'], xv, g1, sc2, sh2, p['norm1_post'], p['norm2_pre'],
                           p['w_router'], p['b_router'], l)
    return dict(x1=x1, h2=h2, ti=ti, tg=tg, g2=g2, state=(new_buf, c_new, n_new, m_new), bs=xv.shape[:2])


def kernel(x_prompt, x_sample, c_prompt, c_sample, state_conv, state_mlstm_C, state_mlstm_n, state_mlstm_m,
           w_ada, b_ada, norm1_pre, w_in, b_gates, conv_w, conv_b, conv_ln_w, conv_ln_b, mlstm_norm_w,
           w_out, norm1_post, norm2_pre, w_router, b_router, w1, b1, w2, b2, norm2_post):
    depth = w_ada.shape[0]
    bp, sp, d = x_prompt.shape
    bs, ss, _ = x_sample.shape
    n_heads, dqk, dv = state_mlstm_C.shape[2:]
    n_experts = w_router.shape[-1]
    nst, ch = state_conv.shape[2:]
    n_gate = 2 * n_heads
    assert n_gate <= LANES and n_experts <= LANES

    xp, xs = x_prompt, x_sample
    mp = -(-(bp + bs) // SUBLANES) * SUBLANES
    c_all = jnp.zeros((mp, d), F32).at[:bp].set(c_prompt).at[bp:bp + bs].set(c_sample)
    outs = [[] for _ in range(8)]
    for l in range(depth):
        row = lambda v: v[l].reshape(1, -1)
        p = dict(
            w_in=w_in, w_out=w_out,
            w_gate=jnp.pad(w_in[l, :, w_in.shape[-1] - n_gate:], ((0, 0), (0, LANES - n_gate))),
            b_gate=jnp.pad(b_gates[l], (0, LANES - n_gate)).reshape(1, LANES),
            norm1_pre=row(norm1_pre), conv_w=conv_w[l], conv_b=row(conv_b), conv_ln_w=row(conv_ln_w),
            conv_ln_b=row(conv_ln_b), mlstm_norm_w=mlstm_norm_w[l], norm1_post=row(norm1_post),
            norm2_pre=row(norm2_pre),
            w_router=jnp.pad(w_router[l], ((0, 0), (0, LANES - n_experts))),
            b_router=jnp.pad(b_router[l], (0, LANES - n_experts), constant_values=NEG_BIG).reshape(1, LANES))
        mod = _ada(c_all, w_ada, b_ada, l)
        mods = jnp.split(mod, 6, axis=-1)
        zero = lambda shape: jnp.zeros(shape, F32)
        gp = _mixer(xp, [mm[:bp] for mm in mods], zero((bp, nst, ch)), zero((bp, n_heads, dqk, dv)),
                    zero((bp, n_heads, dqk)), zero((bp, n_heads)), p, l, per_row=False)
        gs = _mixer(xs, [mm[bp:bp + bs] for mm in mods], state_conv[l], state_mlstm_C[l], state_mlstm_n[l],
                    state_mlstm_m[l], p, l, per_row=True)

        tp, ts_ = bp * sp, bs * ss
        top_i = jnp.concatenate([gp['ti'][:, :TOP_K], gs['ti'][:, :TOP_K]], axis=0)
        dest, row_tok, tile_e, tile_u = _route(top_i, n_experts, MOE_ROWS)
        h2_all = jnp.concatenate([gp['h2'], gs['h2']], axis=0)
        xsorted = _gather_rows(h2_all, row_tok, MOE_ROWS)
        act = _gmm1(xsorted, w1, b1, tile_e, tile_u, l, MOE_ROWS)
        y = _gmm2(act, w2, b2, tile_e, tile_u, l, MOE_ROWS)
        nw2 = row(norm2_post)
        xp = _combine(y, dest[:tp], gp['tg'], gp['x1'], gp['g2'], nw2, *gp['bs']).reshape(bp, sp, d)
        xs = _combine(y, dest[tp:], gs['tg'], gs['x1'], gs['g2'], nw2, *gs['bs']).reshape(bs, ss, d)
        for o, v in zip(outs, gp['state'] + gs['state']):
            o.append(v)
    return (xp, xs) + tuple(jnp.stack(o) for o in outs)
```

```python
import functools

import jax
import jax.numpy as jnp
from jax import lax
from jax.experimental import pallas as pl
from jax.experimental.pallas import tpu as pltpu

F32 = jnp.float32
BF16 = jnp.bfloat16
I32 = jnp.int32
U32 = jnp.uint32

EPS = 1e-6
GATE_CAP = 15.0
TOP_K = 4
SWIGLU_LIMIT = 7.0
SWIGLU_ALPHA = 1.702
NEG_BIG = -1e30

LANES = 128
SUBLANES = 8
VMEM_LIMIT = 56 * 1024 * 1024
MLSTM_CHUNK = 256
MOE_ROWS = 256
ROW_TILE = 256
HI_HALF = 0xFFFF0000


def _params(sem):
    return pltpu.CompilerParams(dimension_semantics=sem, vmem_limit_bytes=VMEM_LIMIT)


def _sigmoid(x):
    return 1.0 / (1.0 + jnp.exp(-x))


def _pick(n, prefs):
    for p in prefs:
        if n % p == 0:
            return p
    return n


def _pack_pair(lo, hi):
    lo_b = lax.bitcast_convert_type(lo.astype(BF16).astype(F32), U32) >> 16
    hi_b = lax.bitcast_convert_type(hi.astype(BF16).astype(F32), U32) & jnp.uint32(HI_HALF)
    return hi_b | lo_b


def _unpack_pair(p):
    lo = lax.bitcast_convert_type(p << 16, F32).astype(BF16)
    hi = lax.bitcast_convert_type(p & jnp.uint32(HI_HALF), F32).astype(BF16)
    return lo, hi


class _Group:
    def __init__(self, bsz, s, mod_row0, tile_rows):
        self.bsz, self.s = bsz, s
        if s >= tile_rows:
            self.nbq, self.st = 1, _pick(s, (tile_rows, 128, 64, 32, 16, 8))
            self.n_outer, self.n_inner = bsz, s // self.st
        else:
            self.nbq, self.st = _pick(bsz, (tile_rows // s, 8, 4, 2, 1)), s
            self.n_outer, self.n_inner = bsz // self.nbq, 1
        assert mod_row0 % self.nbq == 0
        self.mod_blk0 = mod_row0 // self.nbq
        self.ts = self.nbq * self.st
        self.grid = (self.n_outer, self.n_inner)

    def rows(self, width, col=0):
        ni = self.n_inner
        return pl.BlockSpec((self.ts, width), lambda o, i, *_: (o * ni + i, col))

    def mod(self, d, col):
        b0 = self.mod_blk0
        return pl.BlockSpec((self.nbq, 1, d), lambda o, i, *_: (b0 + o, 0, col))

    def const(self, shape):
        nd = len(shape)
        return pl.BlockSpec(shape, lambda o, i, *_: (0,) * nd)


def _mod_rows(m_ref, st):
    m = m_ref[...]
    nbq, _, d = m.shape
    if nbq == 1:
        return m[0]
    return jnp.broadcast_to(m, (nbq, st, d)).reshape(nbq * st, d)


def _ada_kernel(c_ref, w_ref, b_ref, o_ref):
    c = c_ref[...]
    s = (c * _sigmoid(c)).astype(BF16)
    o_ref[...] = jnp.dot(s, w_ref[...].astype(BF16), preferred_element_type=F32) + b_ref[...]


def _ada(c_all, w_ada, b_ada, l):
    mp, d = c_all.shape
    n = w_ada.shape[-1]
    tn = _pick(n, (512, 256, 128))
    return pl.pallas_call(
        _ada_kernel,
        grid=(n // tn,),
        in_specs=[pl.BlockSpec((mp, d), lambda j: (0, 0)),
                  pl.BlockSpec((None, d, tn), lambda j: (l, 0, j)),
                  pl.BlockSpec((None, 1, tn), lambda j: (l, 0, j))],
        out_specs=pl.BlockSpec((mp, tn), lambda j: (0, j)),
        out_shape=jax.ShapeDtypeStruct((mp, n), F32),
        compiler_params=_params(("parallel",)),
        name="ada",
    )(c_all, w_ada, b_ada.reshape(b_ada.shape[0], 1, n))


def _prenorm_kernel(x_ref, sc_ref, sh_ref, nw_ref, wg_ref, bg_ref, h_ref, g_ref, *, n_heads, st):
    x = x_ref[...]
    y = x * lax.rsqrt(jnp.mean(x * x, -1, keepdims=True) + EPS) * nw_ref[...]
    h = y * (1.0 + _mod_rows(sc_ref, st)) + _mod_rows(sh_ref, st)
    h_ref[...] = h.astype(BF16)
    z = jnp.dot(h, wg_ref[...], preferred_element_type=F32, precision=lax.Precision.HIGHEST) + bg_ref[...]
    cap = GATE_CAP * jnp.tanh(z / GATE_CAP)
    logsig = jnp.minimum(cap, 0.0) - jnp.log(1.0 + jnp.exp(-jnp.abs(cap)))
    lane = lax.broadcasted_iota(I32, z.shape, 1)
    g_ref[...] = jnp.where(lane < n_heads, cap, logsig)


def _prenorm(x2, mod3, grp, nw, wg, bg, n_heads):
    rows, d = x2.shape
    kern = functools.partial(_prenorm_kernel, n_heads=n_heads, st=grp.st)
    return pl.pallas_call(
        kern,
        grid=grp.grid,
        in_specs=[grp.rows(d), grp.mod(d, 1), grp.mod(d, 0), grp.const((1, d)),
                  grp.const((d, LANES)), grp.const((1, LANES))],
        out_specs=[grp.rows(d), grp.rows(LANES)],
        out_shape=[jax.ShapeDtypeStruct((rows, d), BF16), jax.ShapeDtypeStruct((rows, LANES), F32)],
        compiler_params=_params(("parallel", "parallel")),
        name="prenorm",
    )(x2, mod3, mod3, nw, wg, bg)


def _inproj_kernel(a_ref, w_ref, o_ref, wbf_ref):
    @pl.when(pl.program_id(1) == 0)
    def _():
        wbf_ref[...] = w_ref[...].astype(BF16)

    o_ref[...] = jnp.dot(a_ref[...], wbf_ref[...], preferred_element_type=F32)


def _inproj(h, w_in, l, n_main):
    m, d = h.shape
    tn = _pick(n_main, (512, 256, 128))
    tm = _pick(m, (512, 256, 128, 64, 32, 16, 8))
    return pl.pallas_call(
        _inproj_kernel,
        grid=(n_main // tn, m // tm),
        in_specs=[pl.BlockSpec((tm, d), lambda j, i: (i, 0)),
                  pl.BlockSpec((None, d, tn), lambda j, i: (l, 0, j))],
        out_specs=pl.BlockSpec((tm, tn), lambda j, i: (i, j)),
        out_shape=jax.ShapeDtypeStruct((m, n_main), F32),
        scratch_shapes=[pltpu.VMEM((d, tn), BF16)],
        compiler_params=_params(("arbitrary", "arbitrary")),
        name="inproj",
    )(h, w_in)


CONV_HALO = 32


def _conv_taps(f_ref, w_ref, base, rows, c0, cc, width):
    acc = jnp.zeros((rows, cc), F32)
    for ph in range(SUBLANES):
        if ph >= width:
            break
        n_al = (width - 1 - ph) // SUBLANES + 1
        gb = f_ref[base + ph: base + ph + rows + SUBLANES * (n_al - 1), c0:c0 + cc]
        for a in range(n_al):
            j = SUBLANES * a + ph
            acc = acc + w_ref[j:j + 1, c0:c0 + cc] * gb[SUBLANES * a:SUBLANES * a + rows]
    return acc


def _ln_swish(y, lw, lb):
    mu = jnp.mean(y, -1, keepdims=True)
    yc = y - mu
    yn = yc * lax.rsqrt(jnp.mean(yc * yc, -1, keepdims=True) + EPS) * lw + lb
    return yn * _sigmoid(yn)


def _conv_seq_kernel(av_ref, ag_ref, st_ref, w_ref, cb_ref, lw_ref, lb_ref, o_ref, ns_ref, f_ref, y_ref,
                     *, ts, width, rb, cc):
    i = pl.program_id(1)
    off = CONV_HALO - (width - 1)
    ch = f_ref.shape[1]

    @pl.when(i == 0)
    def _():
        f_ref[0:CONV_HALO, :] = jnp.zeros((CONV_HALO, ch), F32)
        f_ref[off:CONV_HALO, :] = st_ref[...]

    f_ref[CONV_HALO:CONV_HALO + ts, :] = av_ref[...] * _sigmoid(ag_ref[...])
    for r0 in range(0, ts, rb):
        for c0 in range(0, ch, cc):
            acc = _conv_taps(f_ref, w_ref, r0 + off, rb, c0, cc, width)
            y_ref[r0:r0 + rb, c0:c0 + cc] = acc + cb_ref[:, c0:c0 + cc]
    o_ref[...] = _ln_swish(y_ref[...], lw_ref[...], lb_ref[...]).astype(o_ref.dtype)

    @pl.when(i == pl.num_programs(1) - 1)
    def _():
        ns_ref[...] = f_ref[ts + off:ts + CONV_HALO, :]

    f_ref[0:CONV_HALO, :] = f_ref[ts:ts + CONV_HALO, :]


def _conv_seq(proj, state, w, cb, lw, lb, bsz, s):
    width, ch = w.shape
    ts = _pick(s, (128, 64, 32))
    ns = s // ts
    rb = _pick(ts, (64, 32))
    cc = _pick(ch, (256, 128))
    kern = functools.partial(_conv_seq_kernel, ts=ts, width=width, rb=rb, cc=cc)
    vec = pl.BlockSpec((1, ch), lambda b, i: (0, 0))
    return pl.pallas_call(
        kern,
        grid=(bsz, ns),
        in_specs=[pl.BlockSpec((ts, ch), lambda b, i: (b * ns + i, 0)),
                  pl.BlockSpec((ts, ch), lambda b, i: (b * ns + i, 1)),
                  pl.BlockSpec((None, width - 1, ch), lambda b, i: (b, 0, 0)),
                  pl.BlockSpec((width, ch), lambda b, i: (0, 0)),
                  vec, vec, vec],
        out_specs=[pl.BlockSpec((ts, ch), lambda b, i: (b * ns + i, 0)),
                   pl.BlockSpec((None, width - 1, ch), lambda b, i: (b, 0, 0))],
        out_shape=[jax.ShapeDtypeStruct((bsz * s, ch), BF16),
                   jax.ShapeDtypeStruct((bsz, width - 1, ch), F32)],
        scratch_shapes=[pltpu.VMEM((CONV_HALO + ts, ch), F32), pltpu.VMEM((ts, ch), F32)],
        compiler_params=_params(("arbitrary", "arbitrary")),
        name="conv_seq",
    )(proj, proj, state, w, cb, lw, lb)


def _conv_step_kernel(av_ref, ag_ref, st_ref, w_ref, cb_ref, lw_ref, lb_ref, o_ref, ns_ref, f_ref, y_ref,
                      *, nb, s, width, cc):
    nst = width - 1
    ch = f_ref.shape[1]

    def body(q, carry):
        r = pl.multiple_of(q * s, s)
        f_ref[0:nst, :] = st_ref[q]
        f_ref[nst:nst + s, :] = av_ref[pl.ds(r, s), :] * _sigmoid(ag_ref[pl.ds(r, s), :])
        for c0 in range(0, ch, cc):
            acc = _conv_taps(f_ref, w_ref, 0, s, c0, cc, width)
            y_ref[pl.ds(r, s), c0:c0 + cc] = acc + cb_ref[:, c0:c0 + cc]
        ns_ref[q] = f_ref[s:s + nst, :]
        return carry

    lax.fori_loop(0, nb, body, 0)
    o_ref[...] = _ln_swish(y_ref[...], lw_ref[...], lb_ref[...]).astype(o_ref.dtype)


def _conv_step(proj, state, w, cb, lw, lb, bsz, s):
    width, ch = w.shape
    assert s % SUBLANES == 0
    nb = _pick(bsz, (16, 8, 4, 2, 1))
    cc = _pick(ch, (512, 256, 128))
    kern = functools.partial(_conv_step_kernel, nb=nb, s=s, width=width, cc=cc)
    vec = pl.BlockSpec((1, ch), lambda b: (0, 0))
    frows = -(-(width - 1 + s) // SUBLANES) * SUBLANES
    return pl.pallas_call(
        kern,
        grid=(bsz // nb,),
        in_specs=[pl.BlockSpec((nb * s, ch), lambda b: (b, 0)),
                  pl.BlockSpec((nb * s, ch), lambda b: (b, 1)),
                  pl.BlockSpec((nb, width - 1, ch), lambda b: (b, 0, 0)),
                  pl.BlockSpec((width, ch), lambda b: (0, 0)),
                  vec, vec, vec],
        out_specs=[pl.BlockSpec((nb * s, ch), lambda b: (b, 0)),
                   pl.BlockSpec((nb, width - 1, ch), lambda b: (b, 0, 0))],
        out_shape=[jax.ShapeDtypeStruct((bsz * s, ch), BF16),
                   jax.ShapeDtypeStruct((bsz, width - 1, ch), F32)],
        scratch_shapes=[pltpu.VMEM((frows, ch), F32), pltpu.VMEM((nb * s, ch), F32)],
        compiler_params=_params(("parallel",)),
        name="conv_step",
    )(proj, proj, state, w, cb, lw, lb)


def _mlstm_kernel(q_ref, k_ref, v_ref, o_ref, g_ref, gt_ref, c0_ref, n0_ref, m0_ref, nw_ref,
                  hm_ref, c_out, n_out, m_out, c_s, n_s, m_s, mall_s, *, n_heads, scale):
    hd = pl.program_id(1)
    c = pl.program_id(2)
    last = pl.num_programs(2) - 1
    ln = q_ref.shape[0]

    @pl.when(c == 0)
    def _():
        c_s[...] = c0_ref[...]
        n_s[...] = n0_ref[...]
        m0 = m0_ref[...]
        lane0 = lax.broadcasted_iota(I32, m0.shape, 1)
        m_s[...] = jnp.sum(jnp.where(lane0 == hd, m0, 0.0), axis=1, keepdims=True)

    @pl.when(jnp.logical_and(c == 0, hd == 0))
    def _():
        mall_s[...] = jnp.zeros(mall_s.shape, F32)

    g = g_ref[...]
    lane = lax.broadcasted_iota(I32, g.shape, 1)
    li_c = jnp.sum(jnp.where(lane == hd, g, 0.0), axis=1, keepdims=True)
    lf_c = jnp.sum(jnp.where(lane == hd + n_heads, g, 0.0), axis=1, keepdims=True)
    gt = gt_ref[...]
    sub = lax.broadcasted_iota(I32, gt.shape, 0)
    li_r = jnp.sum(jnp.where(sub == hd, gt, 0.0), axis=0, keepdims=True)
    lf_r = jnp.sum(jnp.where(sub == hd + n_heads, gt, 0.0), axis=0, keepdims=True)

    row = lax.broadcasted_iota(I32, (ln, ln), 0)
    col = lax.broadcasted_iota(I32, (ln, ln), 1)
    tri = row >= col
    b_c = jnp.sum(jnp.where(tri, lf_r, 0.0), axis=1, keepdims=True)
    b_r = jnp.sum(jnp.where(row <= col, lf_c, 0.0), axis=0, keepdims=True)
    b_l = jnp.sum(lf_r, axis=1, keepdims=True)
    m_prev = m_s[...]

    dmat = jnp.where(tri, b_c - b_r + li_r, NEG_BIG)
    inter = b_c + m_prev
    m_t = jnp.maximum(inter, jnp.max(dmat, axis=1, keepdims=True))
    a = jnp.exp(inter - m_t)

    q = q_ref[...]
    k = k_ref[...] * scale
    qb = q.astype(BF16)
    vb = v_ref[...].astype(BF16)
    cst = c_s[...]
    nst = n_s[...]
    s = lax.dot_general(qb, k.astype(BF16), (((1,), (1,)), ((), ())), preferred_element_type=F32)
    s = s * jnp.exp(dmat - m_t)
    num = a * jnp.dot(qb, cst.astype(BF16), preferred_element_type=F32) \
        + jnp.dot(s.astype(BF16), vb, preferred_element_type=F32)
    den = a * jnp.sum(q * nst, axis=1, keepdims=True) + jnp.sum(s, axis=1, keepdims=True)
    h = num / jnp.maximum(jnp.abs(den), jnp.exp(-m_t))
    hn = h * lax.rsqrt(jnp.mean(h * h, -1, keepdims=True) + EPS) * nw_ref[...]
    hm_ref[...] = (hn * _sigmoid(o_ref[...])).astype(hm_ref.dtype)

    g_r = b_l - b_r + li_r
    g_c = b_l - b_c + li_c
    m_new = jnp.maximum(b_l + m_prev, jnp.max(g_r, axis=1, keepdims=True))
    decay = jnp.exp(b_l + m_prev - m_new)
    kw = k * jnp.exp(g_c - m_new)
    c_new = decay * cst + lax.dot_general(kw.astype(BF16), vb, (((0,), (0,)), ((), ())),
                                          preferred_element_type=F32)
    n_new = decay * nst + jnp.sum(kw, axis=0, keepdims=True)
    c_s[...] = c_new
    n_s[...] = n_new
    m_s[...] = m_new

    @pl.when(c == last)
    def _():
        c_out[...] = c_new
        n_out[...] = n_new
        lane_m = lax.broadcasted_iota(I32, mall_s.shape, 1)
        mall = jnp.where(lane_m == hd, m_new, mall_s[...])
        mall_s[...] = mall
        m_out[...] = mall


def _mlstm(proj, g, c0, n0, m0, norm_w, bsz, s, col_q):
    _, n_heads, dqk, dv = c0.shape
    ln = s if s <= MLSTM_CHUNK else MLSTM_CHUNK
    assert s % ln == 0 and ln % SUBLANES == 0
    nc = s // ln
    assert col_q % dqk == 0 and (col_q + 2 * n_heads * dqk) % dv == 0
    qb0 = col_q // dqk
    kb0 = qb0 + n_heads
    vb0 = (col_q + 2 * n_heads * dqk) // dv
    ob0 = vb0 + n_heads
    gt3 = g[:, :2 * n_heads].reshape(bsz * nc, ln, 2 * n_heads).transpose(0, 2, 1)
    kern = functools.partial(_mlstm_kernel, n_heads=n_heads, scale=dqk ** -0.5)
    rowblk = lambda b, h, c: b * nc + c
    out = pl.pallas_call(
        kern,
        grid=(bsz, n_heads, nc),
        in_specs=[pl.BlockSpec((ln, dqk), lambda b, h, c: (rowblk(b, h, c), qb0 + h)),
                  pl.BlockSpec((ln, dqk), lambda b, h, c: (rowblk(b, h, c), kb0 + h)),
                  pl.BlockSpec((ln, dv), lambda b, h, c: (rowblk(b, h, c), vb0 + h)),
                  pl.BlockSpec((ln, dv), lambda b, h, c: (rowblk(b, h, c), ob0 + h)),
                  pl.BlockSpec((ln, LANES), lambda b, h, c: (rowblk(b, h, c), 0)),
                  pl.BlockSpec((None, 2 * n_heads, ln), lambda b, h, c: (rowblk(b, h, c), 0, 0)),
                  pl.BlockSpec((None, None, dqk, dv), lambda b, h, c: (b, h, 0, 0)),
                  pl.BlockSpec((None, 1, dqk), lambda b, h, c: (b * n_heads + h, 0, 0)),
                  pl.BlockSpec((None, 1, n_heads), lambda b, h, c: (b, 0, 0)),
                  pl.BlockSpec((None, 1, dv), lambda b, h, c: (h, 0, 0))],
        out_specs=[pl.BlockSpec((ln, dv), lambda b, h, c: (rowblk(b, h, c), h)),
                   pl.BlockSpec((None, None, dqk, dv), lambda b, h, c: (b, h, 0, 0)),
                   pl.BlockSpec((None, 1, dqk), lambda b, h, c: (b * n_heads + h, 0, 0)),
                   pl.BlockSpec((None, 1, n_heads), lambda b, h, c: (b, 0, 0))],
        out_shape=[jax.ShapeDtypeStruct((bsz * s, n_heads * dv), BF16),
                   jax.ShapeDtypeStruct((bsz, n_heads, dqk, dv), F32),
                   jax.ShapeDtypeStruct((bsz * n_heads, 1, dqk), F32),
                   jax.ShapeDtypeStruct((bsz, 1, n_heads), F32)],
        scratch_shapes=[pltpu.VMEM((dqk, dv), F32), pltpu.VMEM((1, dqk), F32),
                        pltpu.VMEM((1, 1), F32), pltpu.VMEM((1, n_heads), F32)],
        compiler_params=_params(("arbitrary", "arbitrary", "arbitrary")),
        name="mlstm",
    )(proj, proj, proj, proj, g, gt3, c0, n0.reshape(bsz * n_heads, 1, dqk), m0.reshape(bsz, 1, n_heads),
      norm_w.reshape(n_heads, 1, dv))
    hm, c_new, n_new, m_new = out
    return hm, c_new, n_new.reshape(bsz, n_heads, dqk), m_new.reshape(bsz, n_heads)


def _outproj_kernel(co_ref, hm_ref, w_ref, o_ref, wbf_ref):
    @pl.when(pl.program_id(1) == 0)
    def _():
        wbf_ref[...] = w_ref[...].astype(BF16)

    kc = co_ref.shape[1]
    o_ref[...] = jnp.dot(co_ref[...], wbf_ref[0:kc, :], preferred_element_type=F32) \
        + jnp.dot(hm_ref[...], wbf_ref[kc:2 * kc, :], preferred_element_type=F32)


def _outproj(co, hm, w_out, l):
    m, kc = co.shape
    d = w_out.shape[-1]
    assert hm.shape[1] == kc and w_out.shape[1] == 2 * kc
    tn = _pick(d, (512, 256, 128))
    tm = _pick(m, (512, 256, 128, 64, 32, 16, 8))
    return pl.pallas_call(
        _outproj_kernel,
        grid=(d // tn, m // tm),
        in_specs=[pl.BlockSpec((tm, kc), lambda j, i: (i, 0)),
                  pl.BlockSpec((tm, kc), lambda j, i: (i, 0)),
                  pl.BlockSpec((None, 2 * kc, tn), lambda j, i: (l, 0, j))],
        out_specs=pl.BlockSpec((tm, tn), lambda j, i: (i, j)),
        out_shape=jax.ShapeDtypeStruct((m, d), F32),
        scratch_shapes=[pltpu.VMEM((2 * kc, tn), BF16)],
        compiler_params=_params(("arbitrary", "arbitrary")),
        name="outproj",
    )(co, hm, w_out)


def _router_kernel(mix_ref, x_ref, g1_ref, sc2_ref, sh2_ref, n1_ref, n2_ref, wr_ref, br_ref,
                   x1_ref, h2_ref, ti_ref, tg_ref, *, st):
    mix = mix_ref[...]
    mn = mix * lax.rsqrt(jnp.mean(mix * mix, -1, keepdims=True) + EPS) * n1_ref[...]
    x1 = x_ref[...] + _mod_rows(g1_ref, st) * mn
    x1_ref[...] = x1
    y2 = x1 * lax.rsqrt(jnp.mean(x1 * x1, -1, keepdims=True) + EPS) * n2_ref[...]
    h2 = y2 * (1.0 + _mod_rows(sc2_ref, st)) + _mod_rows(sh2_ref, st)
    half = h2.shape[1] // 2
    h2_ref[...] = _pack_pair(h2[:, :half], h2[:, half:])
    logits = jnp.dot(h2, wr_ref[...], preferred_element_type=F32,
                     precision=lax.Precision.HIGHEST) + br_ref[...]
    n_exp = logits.shape[1]
    lane = lax.broadcasted_iota(I32, logits.shape, 1)
    lane_o = lax.broadcasted_iota(I32, ti_ref.shape, 1)
    idx_out = jnp.zeros(ti_ref.shape, I32)
    val_out = jnp.zeros(tg_ref.shape, F32)
    top = None
    den = jnp.zeros((logits.shape[0], 1), F32)
    for r in range(TOP_K):
        mx = jnp.max(logits, axis=1, keepdims=True)
        ix = jnp.min(jnp.where(logits == mx, lane, n_exp), axis=1, keepdims=True)
        if top is None:
            top = mx
        e = jnp.exp(mx - top)
        den = den + e
        idx_out = jnp.where(lane_o == r, ix, idx_out)
        val_out = jnp.where(lane_o == r, e, val_out)
        logits = jnp.where(lane == ix, NEG_BIG, logits)
    ti_ref[...] = idx_out
    tg_ref[...] = val_out / den


def _router(mix, x2, mod3, grp, n1, n2, wr, br):
    rows, d = x2.shape
    n_exp = wr.shape[1]
    kern = functools.partial(_router_kernel, st=grp.st)
    return pl.pallas_call(
        kern,
        grid=grp.grid,
        in_specs=[grp.rows(d), grp.rows(d), grp.mod(d, 2), grp.mod(d, 4), grp.mod(d, 3),
                  grp.const((1, d)), grp.const((1, d)), grp.const((d, n_exp)), grp.const((1, n_exp))],
        out_specs=[grp.rows(d), grp.rows(d // 2), grp.rows(LANES), grp.rows(LANES)],
        out_shape=[jax.ShapeDtypeStruct((rows, d), F32), jax.ShapeDtypeStruct((rows, d // 2), U32),
                   jax.ShapeDtypeStruct((rows, LANES), I32), jax.ShapeDtypeStruct((rows, LANES), F32)],
        compiler_params=_params(("parallel", "parallel")),
        name="router",
    )(mix, x2, mod3, mod3, mod3, n1, n2, wr, br)


def _rank_kernel(ti_ref, ps_ref, o_ref, carry_ref):
    @pl.when(pl.program_id(0) == 0)
    def _():
        carry_ref[...] = jnp.zeros(carry_ref.shape, F32)

    ti = ti_ref[...].astype(F32)
    tt = ti.shape[0]
    lane = lax.broadcasted_iota(I32, ti.shape, 1)
    lane_f = lane.astype(F32)
    cols = []
    member = jnp.zeros(ti.shape, F32)
    for kk in range(TOP_K):
        ek = jnp.sum(jnp.where(lane == kk, ti, 0.0), axis=1, keepdims=True)
        cols.append(ek)
        member = member + jnp.where(lane_f == ek, 1.0, 0.0)
    r = lax.broadcasted_iota(I32, (tt, tt), 0)
    c = lax.broadcasted_iota(I32, (tt, tt), 1)
    before = jnp.where(r > c, 1.0, 0.0).astype(BF16)
    base = jnp.dot(before, member.astype(BF16), preferred_element_type=F32) + carry_ref[...] + ps_ref[...]
    out = jnp.zeros(o_ref.shape, I32)
    for kk in range(TOP_K):
        dk = jnp.sum(jnp.where(lane_f == cols[kk], base, 0.0), axis=1, keepdims=True)
        out = jnp.where(lane == kk, dk.astype(I32), out)
    o_ref[...] = out
    carry_ref[...] += jnp.sum(member, axis=0, keepdims=True)


def _rank(top_i, pad_start):
    t = top_i.shape[0]
    tt = _pick(t, (512, 256, 128, 64, 32, 16, 8))
    return pl.pallas_call(
        _rank_kernel,
        grid=(t // tt,),
        in_specs=[pl.BlockSpec((tt, LANES), lambda i: (i, 0)), pl.BlockSpec((1, LANES), lambda i: (0, 0))],
        out_specs=pl.BlockSpec((tt, LANES), lambda i: (i, 0)),
        out_shape=jax.ShapeDtypeStruct((t, LANES), I32),
        scratch_shapes=[pltpu.VMEM((1, LANES), F32)],
        compiler_params=_params(("arbitrary",)),
        name="moe_rank",
    )(top_i, pad_start)


def _route(top_i, n_experts, tm):
    t = top_i.shape[0]
    a = t * TOP_K
    flat_e = top_i[:, :TOP_K].reshape(a)
    counts = jnp.sum((flat_e[:, None] == jnp.arange(n_experts, dtype=I32)[None, :]).astype(I32), axis=0)
    padded = (counts + tm - 1) // tm * tm
    pad_end = jnp.cumsum(padded)
    pad_start = pad_end - padded
    nt = a // tm + n_experts
    tile_start = jnp.arange(nt, dtype=I32) * tm
    tile_u = (tile_start < pad_end[-1]).astype(I32)
    n_used = jnp.sum(tile_u)
    tile_e = jnp.minimum(jnp.searchsorted(pad_end, tile_start, side='right'), n_experts - 1).astype(I32)
    tile_e = jnp.where(tile_u == 1, tile_e, tile_e[jnp.maximum(n_used - 1, 0)])
    idx = jnp.arange(nt, dtype=I32)
    starts = jnp.logical_and(tile_u == 1, jnp.logical_or(idx == 0, tile_e != jnp.roll(tile_e, 1)))
    seg = (jnp.cumsum(starts.astype(I32)) - 1).astype(I32)
    later = lax.cummin(jnp.where(starts, idx, nt)[::-1])[::-1]
    nxt = jnp.concatenate([later[1:], jnp.full((1,), nt, I32)])
    next_e = tile_e[jnp.where(nxt >= nt, 0, nxt)]
    n_seg = jnp.sum(starts.astype(I32)).reshape(1)
    tiles = (tile_e, tile_u, seg, next_e, n_seg)
    ps = jnp.zeros((1, LANES), F32).at[0, :n_experts].set(pad_start.astype(F32))
    dest = _rank(top_i, ps)
    return dest, tiles, nt


def _row_copy(src_ref, dst_ref, sem, src_row, dst_row):
    return pltpu.make_async_copy(src_ref.at[pl.ds(src_row, 1), :], dst_ref.at[pl.ds(dst_row, 1), :], sem)


def _dispatch_kernel(dest_ref, src_ref, xs_in, xs_out, sem):
    del xs_in
    tt = src_ref.shape[0]

    def start(r, carry):
        for kk in range(TOP_K):
            _row_copy(src_ref, xs_out, sem, r, dest_ref[0, r * TOP_K + kk]).start()
        return carry

    def wait(r, carry):
        for kk in range(TOP_K):
            _row_copy(src_ref, xs_out, sem, r, 0).wait()
        return carry

    lax.fori_loop(0, tt, start, 0)
    lax.fori_loop(0, tt, wait, 0)


def _dispatch(h2p, dest, xs):
    t, w = h2p.shape
    tt = _pick(t, (256, 128, 64, 32, 16, 8))
    nt = t // tt
    return pl.pallas_call(
        _dispatch_kernel,
        grid=(nt,),
        in_specs=[pl.BlockSpec((None, 1, tt * TOP_K), lambda i: (i, 0, 0), memory_space=pltpu.SMEM),
                  pl.BlockSpec((tt, w), lambda i: (i, 0)),
                  pl.BlockSpec(memory_space=pl.ANY)],
        out_specs=pl.BlockSpec(memory_space=pl.ANY),
        out_shape=jax.ShapeDtypeStruct(xs.shape, xs.dtype),
        scratch_shapes=[pltpu.SemaphoreType.DMA(())],
        input_output_aliases={2: 0},
        compiler_params=_params(("arbitrary",)),
        name="moe_dispatch",
    )(dest[:, :TOP_K].reshape(nt, 1, tt * TOP_K), h2p, xs)


N_TILE_TABLES = 5


def _segment_weights(tables, w_hbm, wst_ref, sem, *, l, tn, col_offs):
    te_ref, tu_ref, sg_ref, ne_ref, ns_ref = tables
    j = pl.program_id(0)
    i = pl.program_id(1)
    n_seg = ns_ref[0]
    first = jnp.logical_and(tu_ref[i] == 1,
                            jnp.logical_or(i == 0, te_ref[i] != te_ref[jnp.maximum(i - 1, 0)]))
    g = j * n_seg + sg_ref[i]
    slot = lax.rem(g, 2)

    def copies(e, jj, sl):
        return [pltpu.make_async_copy(
            w_hbm.at[l, e, :, pl.ds(pl.multiple_of((off + jj) * tn, tn), tn)], wst_ref.at[sl, m], sem.at[sl])
            for m, off in enumerate(col_offs)]

    @pl.when(jnp.logical_and(j == 0, i == 0))
    def _():
        for cp in copies(te_ref[0], 0, 0):
            cp.start()

    @pl.when(first)
    def _():
        for cp in copies(te_ref[i], j, slot):
            cp.wait()

        @pl.when(g + 1 < pl.num_programs(0) * n_seg)
        def _():
            jn = jnp.where(sg_ref[i] + 1 == n_seg, j + 1, j)
            for cp in copies(ne_ref[i], jn, 1 - slot):
                cp.start()

    return first, slot


def _gmm1_kernel(*refs, l, tn, nj):
    tables = refs[:N_TILE_TABLES]
    x_ref, w_hbm, bg_ref, bu_ref, o_ref, wst_ref, wgb_ref, wub_ref, sem = refs[N_TILE_TABLES:]
    used = tables[1][pl.program_id(1)] == 1
    first, slot = _segment_weights(tables, w_hbm, wst_ref, sem, l=l, tn=tn, col_offs=(0, nj))

    @pl.when(first)
    def _():
        wgb_ref[...] = wst_ref[slot, 0].astype(BF16)
        wub_ref[...] = wst_ref[slot, 1].astype(BF16)

    @pl.when(used)
    def _():
        half = x_ref.shape[1]
        xlo, xhi = _unpack_pair(x_ref[...])
        gt = jnp.dot(xlo, wgb_ref[0:half, :], preferred_element_type=F32) \
            + jnp.dot(xhi, wgb_ref[half:2 * half, :], preferred_element_type=F32) + bg_ref[...]
        up = jnp.dot(xlo, wub_ref[0:half, :], preferred_element_type=F32) \
            + jnp.dot(xhi, wub_ref[half:2 * half, :], preferred_element_type=F32) + bu_ref[...]
        gt = jnp.minimum(gt, SWIGLU_LIMIT)
        up = jnp.clip(up, -SWIGLU_LIMIT, SWIGLU_LIMIT)
        act = (up + 1.0) * gt * _sigmoid(SWIGLU_ALPHA * gt)
        o_ref[...] = act.astype(o_ref.dtype)

    @pl.when(jnp.logical_not(used))
    def _():
        o_ref[...] = jnp.zeros(o_ref.shape, o_ref.dtype)


def _gmm1(xs, w1, b1, tiles, l, tm):
    rows, half = xs.shape
    d = 2 * half
    dff = w1.shape[-1] // 2
    nt = rows // tm
    tn = _pick(dff, (512, 256, 128))
    nj = dff // tn
    b1r = b1.reshape(b1.shape[0], b1.shape[1], 1, 2 * dff)
    kern = functools.partial(_gmm1_kernel, l=l, tn=tn, nj=nj)
    grid_spec = pltpu.PrefetchScalarGridSpec(
        num_scalar_prefetch=N_TILE_TABLES,
        grid=(nj, nt),
        in_specs=[pl.BlockSpec((tm, half), lambda j, i, *t: (i, 0)),
                  pl.BlockSpec(memory_space=pl.ANY),
                  pl.BlockSpec((None, None, 1, tn), lambda j, i, te, *t: (l, te[i], 0, j)),
                  pl.BlockSpec((None, None, 1, tn), lambda j, i, te, *t: (l, te[i], 0, nj + j))],
        out_specs=pl.BlockSpec((tm, tn), lambda j, i, *t: (i, j)),
        scratch_shapes=[pltpu.VMEM((2, 2, d, tn), F32), pltpu.VMEM((d, tn), BF16), pltpu.VMEM((d, tn), BF16),
                        pltpu.SemaphoreType.DMA((2,))])
    return pl.pallas_call(
        kern,
        grid_spec=grid_spec,
        out_shape=jax.ShapeDtypeStruct((rows, dff), BF16),
        compiler_params=_params(("arbitrary", "arbitrary")),
        name="moe_gmm1",
    )(*tiles, xs, w1, b1r, b1r)


def _gmm2_kernel(*refs, l, tn):
    tables = refs[:N_TILE_TABLES]
    a_ref, w_hbm, b_ref, o_ref, wst_ref, wb_ref, sem = refs[N_TILE_TABLES:]
    used = tables[1][pl.program_id(1)] == 1
    first, slot = _segment_weights(tables, w_hbm, wst_ref, sem, l=l, tn=tn, col_offs=(0,))

    @pl.when(first)
    def _():
        wb_ref[...] = wst_ref[slot, 0].astype(BF16)

    @pl.when(used)
    def _():
        o_ref[...] = jnp.dot(a_ref[...], wb_ref[...], preferred_element_type=F32) + b_ref[...]

    @pl.when(jnp.logical_not(used))
    def _():
        o_ref[...] = jnp.zeros(o_ref.shape, o_ref.dtype)


def _gmm2(act, w2, b2, tiles, l, tm):
    rows, dff = act.shape
    d = w2.shape[-1]
    nt = rows // tm
    tn = _pick(d, (2048, 1024, 512, 256, 128))
    b2r = b2.reshape(b2.shape[0], b2.shape[1], 1, d)
    kern = functools.partial(_gmm2_kernel, l=l, tn=tn)
    grid_spec = pltpu.PrefetchScalarGridSpec(
        num_scalar_prefetch=N_TILE_TABLES,
        grid=(d // tn, nt),
        in_specs=[pl.BlockSpec((tm, dff), lambda j, i, *t: (i, 0)),
                  pl.BlockSpec(memory_space=pl.ANY),
                  pl.BlockSpec((None, None, 1, tn), lambda j, i, te, *t: (l, te[i], 0, j))],
        out_specs=pl.BlockSpec((tm, tn), lambda j, i, *t: (i, j)),
        scratch_shapes=[pltpu.VMEM((2, 1, dff, tn), F32), pltpu.VMEM((dff, tn), BF16),
                        pltpu.SemaphoreType.DMA((2,))])
    return pl.pallas_call(
        kern,
        grid_spec=grid_spec,
        out_shape=jax.ShapeDtypeStruct((rows, d), F32),
        compiler_params=_params(("arbitrary", "arbitrary")),
        name="moe_gmm2",
    )(*tiles, act, w2, b2r)


def _combine_kernel(pos_ref, y_hbm, tg_ref, x1_ref, g2_ref, nw_ref, o_ref, buf_ref, sem, *, st):
    tt = o_ref.shape[0]

    def start(r, carry):
        for kk in range(TOP_K):
            _row_copy(y_hbm, buf_ref.at[kk], sem, pos_ref[0, r * TOP_K + kk], r).start()
        return carry

    def wait(r, carry):
        for kk in range(TOP_K):
            _row_copy(y_hbm, buf_ref.at[kk], sem, 0, r).wait()
        return carry

    lax.fori_loop(0, tt, start, 0)
    lax.fori_loop(0, tt, wait, 0)
    gates = tg_ref[...]
    lane = lax.broadcasted_iota(I32, gates.shape, 1)
    f = jnp.zeros(o_ref.shape, F32)
    for kk in range(TOP_K):
        gk = jnp.sum(jnp.where(lane == kk, gates, 0.0), axis=1, keepdims=True)
        f = f + gk * buf_ref[kk]
    fn = f * lax.rsqrt(jnp.mean(f * f, -1, keepdims=True) + EPS) * nw_ref[...]
    o_ref[...] = x1_ref[...] + _mod_rows(g2_ref, st) * fn


def _combine(y, dest, tg, x1, mod3, grp, nw):
    rows, d = x1.shape
    tt = grp.ts
    nt = rows // tt
    ni = grp.n_inner
    kern = functools.partial(_combine_kernel, st=grp.st)
    return pl.pallas_call(
        kern,
        grid=grp.grid,
        in_specs=[pl.BlockSpec((None, 1, tt * TOP_K), lambda o, i: (o * ni + i, 0, 0), memory_space=pltpu.SMEM),
                  pl.BlockSpec(memory_space=pl.ANY),
                  grp.rows(LANES), grp.rows(d), grp.mod(d, 5), grp.const((1, d))],
        out_specs=grp.rows(d),
        out_shape=jax.ShapeDtypeStruct((rows, d), F32),
        scratch_shapes=[pltpu.VMEM((TOP_K, tt, d), F32), pltpu.SemaphoreType.DMA(())],
        compiler_params=_params(("arbitrary", "arbitrary")),
        name="moe_combine",
    )(dest[:, :TOP_K].reshape(nt, 1, tt * TOP_K), y, tg, x1, mod3, nw)


def _mixer(x2, mod3, grp, conv_buf, c0, n0, m0, p, l):
    n_heads = c0.shape[1]
    n_main = p['w_in'].shape[-1] - 2 * n_heads
    ch = p['conv_w'].shape[-1]
    h, g = _prenorm(x2, mod3, grp, p['norm1_pre'], p['w_gate'], p['b_gate'], n_heads)
    proj = _inproj(h, p['w_in'], l, n_main)
    conv = _conv_seq if grp.nbq == 1 else _conv_step
    co, new_buf = conv(proj, conv_buf, p['conv_w'], p['conv_b'], p['conv_ln_w'], p['conv_ln_b'], grp.bsz, grp.s)
    hm, c_new, n_new, m_new = _mlstm(proj, g, c0, n0, m0, p['mlstm_norm_w'], grp.bsz, grp.s, 2 * ch)
    mix = _outproj(co, hm, p['w_out'], l)
    x1, h2p, ti, tg = _router(mix, x2, mod3, grp, p['norm1_post'], p['norm2_pre'], p['w_router'], p['b_router'])
    return dict(x1=x1, h2p=h2p, ti=ti, tg=tg, state=(new_buf, c_new, n_new, m_new))


def kernel(x_prompt, x_sample, c_prompt, c_sample, state_conv, state_mlstm_C, state_mlstm_n, state_mlstm_m,
           w_ada, b_ada, norm1_pre, w_in, b_gates, conv_w, conv_b, conv_ln_w, conv_ln_b, mlstm_norm_w,
           w_out, norm1_post, norm2_pre, w_router, b_router, w1, b1, w2, b2, norm2_post):
    depth = w_ada.shape[0]
    bp, sp, d = x_prompt.shape
    bs, ss, _ = x_sample.shape
    n_heads, dqk, dv = state_mlstm_C.shape[2:]
    n_experts = w_router.shape[-1]
    nst, ch = state_conv.shape[2:]
    n_gate = 2 * n_heads
    assert n_gate <= LANES and n_experts <= LANES
    tp, tsmp = bp * sp, bs * ss

    mp = -(-(bp + bs) // SUBLANES) * SUBLANES
    c_all = jnp.zeros((mp, d), F32).at[:bs].set(c_sample).at[bs:bs + bp].set(c_prompt)
    grp_p = _Group(bp, sp, bs, ROW_TILE)
    grp_s = _Group(bs, ss, 0, ROW_TILE)

    xp, xs = x_prompt.reshape(tp, d), x_sample.reshape(tsmp, d)
    outs = [[] for _ in range(8)]
    for l in range(depth):
        row = lambda v: v[l].reshape(1, -1)
        p = dict(
            w_in=w_in, w_out=w_out,
            w_gate=jnp.pad(w_in[l, :, w_in.shape[-1] - n_gate:], ((0, 0), (0, LANES - n_gate))),
            b_gate=jnp.pad(b_gates[l], (0, LANES - n_gate)).reshape(1, LANES),
            norm1_pre=row(norm1_pre), conv_w=conv_w[l], conv_b=row(conv_b), conv_ln_w=row(conv_ln_w),
            conv_ln_b=row(conv_ln_b), mlstm_norm_w=mlstm_norm_w[l], norm1_post=row(norm1_post),
            norm2_pre=row(norm2_pre), w_router=w_router[l], b_router=row(b_router))
        mod3 = _ada(c_all, w_ada, b_ada, l).reshape(mp, 1, 6 * d)
        zero = lambda shape: jnp.zeros(shape, F32)
        gp = _mixer(xp, mod3, grp_p, zero((bp, nst, ch)), zero((bp, n_heads, dqk, dv)),
                    zero((bp, n_heads, dqk)), zero((bp, n_heads)), p, l)
        gs = _mixer(xs, mod3, grp_s, state_conv[l], state_mlstm_C[l], state_mlstm_n[l],
                    state_mlstm_m[l], p, l)

        top_i = jnp.concatenate([gp['ti'], gs['ti']], axis=0)
        dest, tiles, nt = _route(top_i, n_experts, MOE_ROWS)
        xsorted = jnp.zeros((nt * MOE_ROWS, d // 2), U32)
        xsorted = _dispatch(gp['h2p'], dest[:tp], xsorted)
        xsorted = _dispatch(gs['h2p'], dest[tp:], xsorted)
        act = _gmm1(xsorted, w1, b1, tiles, l, MOE_ROWS)
        y = _gmm2(act, w2, b2, tiles, l, MOE_ROWS)
        nw2 = row(norm2_post)
        xp = _combine(y, dest[:tp], gp['tg'], gp['x1'], mod3, grp_p, nw2)
        xs = _combine(y, dest[tp:], gs['tg'], gs['x1'], mod3, grp_s, nw2)
        for o, v in zip(outs, gp['state'] + gs['state']):
            o.append(v)
    return (xp.reshape(bp, sp, d), xs.reshape(bs, ss, d)) + tuple(jnp.stack(o) for o in outs)
```

```python
import functools

import jax
import jax.numpy as jnp
from jax import lax
from jax.experimental import pallas as pl
from jax.experimental.pallas import tpu as pltpu

F32 = jnp.float32
BF16 = jnp.bfloat16
I32 = jnp.int32
U32 = jnp.uint32

EPS = 1e-6
GATE_CAP = 15.0
TOP_K = 4
SWIGLU_LIMIT = 7.0
SWIGLU_ALPHA = 1.702
NEG_BIG = -1e30

LANES = 128
SUBLANES = 8
VMEM_LIMIT = 56 * 1024 * 1024
MLSTM_CHUNK = 256
MOE_ROWS = 256
ROW_TILE = 256
HI_HALF = 0xFFFF0000


def _params(sem):
    return pltpu.CompilerParams(dimension_semantics=sem, vmem_limit_bytes=VMEM_LIMIT)


def _sigmoid(x):
    return 1.0 / (1.0 + jnp.exp(-x))


def _pick(n, prefs):
    for p in prefs:
        if n % p == 0:
            return p
    return n


def _pack_pair(lo, hi):
    lo_b = lax.bitcast_convert_type(lo.astype(BF16).astype(F32), U32) >> 16
    hi_b = lax.bitcast_convert_type(hi.astype(BF16).astype(F32), U32) & jnp.uint32(HI_HALF)
    return hi_b | lo_b


def _unpack_pair(p):
    lo = lax.bitcast_convert_type(p << 16, F32).astype(BF16)
    hi = lax.bitcast_convert_type(p & jnp.uint32(HI_HALF), F32).astype(BF16)
    return lo, hi


class _Group:
    def __init__(self, bsz, s, mod_row0, tile_rows):
        self.bsz, self.s = bsz, s
        if s >= tile_rows:
            self.nbq, self.st = 1, _pick(s, (tile_rows, 128, 64, 32, 16, 8))
            self.n_outer, self.n_inner = bsz, s // self.st
        else:
            self.nbq, self.st = _pick(bsz, (tile_rows // s, 8, 4, 2, 1)), s
            self.n_outer, self.n_inner = bsz // self.nbq, 1
        assert mod_row0 % self.nbq == 0
        self.mod_blk0 = mod_row0 // self.nbq
        self.ts = self.nbq * self.st
        self.grid = (self.n_outer, self.n_inner)

    def rows(self, width, col=0):
        ni = self.n_inner
        return pl.BlockSpec((self.ts, width), lambda o, i, *_: (o * ni + i, col))

    def mod(self, d, col):
        b0 = self.mod_blk0
        return pl.BlockSpec((self.nbq, 1, d), lambda o, i, *_: (b0 + o, 0, col))

    def const(self, shape):
        nd = len(shape)
        return pl.BlockSpec(shape, lambda o, i, *_: (0,) * nd)


def _mod_rows(m_ref, st):
    m = m_ref[...]
    nbq, _, d = m.shape
    if nbq == 1:
        return m[0]
    return jnp.broadcast_to(m, (nbq, st, d)).reshape(nbq * st, d)


def _ada_kernel(c_ref, w_ref, b_ref, o_ref):
    c = c_ref[...]
    s = (c * _sigmoid(c)).astype(BF16)
    o_ref[...] = jnp.dot(s, w_ref[...].astype(BF16), preferred_element_type=F32) + b_ref[...]


def _ada(c_all, w_ada, b_ada, l):
    mp, d = c_all.shape
    n = w_ada.shape[-1]
    tn = _pick(n, (512, 256, 128))
    return pl.pallas_call(
        _ada_kernel,
        grid=(n // tn,),
        in_specs=[pl.BlockSpec((mp, d), lambda j: (0, 0)),
                  pl.BlockSpec((None, d, tn), lambda j: (l, 0, j)),
                  pl.BlockSpec((None, 1, tn), lambda j: (l, 0, j))],
        out_specs=pl.BlockSpec((mp, tn), lambda j: (0, j)),
        out_shape=jax.ShapeDtypeStruct((mp, n), F32),
        compiler_params=_params(("parallel",)),
        name="ada",
    )(c_all, w_ada, b_ada.reshape(b_ada.shape[0], 1, n))


def _prenorm_kernel(x_ref, sc_ref, sh_ref, nw_ref, wg_ref, bg_ref, h_ref, g_ref, *, n_heads, st):
    x = x_ref[...]
    y = x * lax.rsqrt(jnp.mean(x * x, -1, keepdims=True) + EPS) * nw_ref[...]
    h = y * (1.0 + _mod_rows(sc_ref, st)) + _mod_rows(sh_ref, st)
    h_ref[...] = h.astype(BF16)
    z = jnp.dot(h, wg_ref[...], preferred_element_type=F32, precision=lax.Precision.HIGHEST) + bg_ref[...]
    cap = GATE_CAP * jnp.tanh(z / GATE_CAP)
    logsig = jnp.minimum(cap, 0.0) - jnp.log(1.0 + jnp.exp(-jnp.abs(cap)))
    lane = lax.broadcasted_iota(I32, z.shape, 1)
    g_ref[...] = jnp.where(lane < n_heads, cap, logsig)


def _prenorm(x2, mod3, grp, nw, wg, bg, n_heads):
    rows, d = x2.shape
    kern = functools.partial(_prenorm_kernel, n_heads=n_heads, st=grp.st)
    return pl.pallas_call(
        kern,
        grid=grp.grid,
        in_specs=[grp.rows(d), grp.mod(d, 1), grp.mod(d, 0), grp.const((1, d)),
                  grp.const((d, LANES)), grp.const((1, LANES))],
        out_specs=[grp.rows(d), grp.rows(LANES)],
        out_shape=[jax.ShapeDtypeStruct((rows, d), BF16), jax.ShapeDtypeStruct((rows, LANES), F32)],
        compiler_params=_params(("parallel", "parallel")),
        name="prenorm",
    )(x2, mod3, mod3, nw, wg, bg)


def _inproj_kernel(a_ref, w_ref, o_ref, wbf_ref):
    @pl.when(pl.program_id(1) == 0)
    def _():
        wbf_ref[...] = w_ref[...].astype(BF16)

    o_ref[...] = jnp.dot(a_ref[...], wbf_ref[...], preferred_element_type=F32)


def _inproj(h, w_in, l, n_main):
    m, d = h.shape
    tn = _pick(n_main, (512, 256, 128))
    tm = _pick(m, (512, 256, 128, 64, 32, 16, 8))
    return pl.pallas_call(
        _inproj_kernel,
        grid=(n_main // tn, m // tm),
        in_specs=[pl.BlockSpec((tm, d), lambda j, i: (i, 0)),
                  pl.BlockSpec((None, d, tn), lambda j, i: (l, 0, j))],
        out_specs=pl.BlockSpec((tm, tn), lambda j, i: (i, j)),
        out_shape=jax.ShapeDtypeStruct((m, n_main), F32),
        scratch_shapes=[pltpu.VMEM((d, tn), BF16)],
        compiler_params=_params(("arbitrary", "arbitrary")),
        name="inproj",
    )(h, w_in)


CONV_HALO = 32


def _conv_taps(f_ref, w_ref, base, rows, c0, cc, width):
    acc = jnp.zeros((rows, cc), F32)
    for ph in range(SUBLANES):
        if ph >= width:
            break
        n_al = (width - 1 - ph) // SUBLANES + 1
        gb = f_ref[base + ph: base + ph + rows + SUBLANES * (n_al - 1), c0:c0 + cc]
        for a in range(n_al):
            j = SUBLANES * a + ph
            acc = acc + w_ref[j:j + 1, c0:c0 + cc] * gb[SUBLANES * a:SUBLANES * a + rows]
    return acc


def _ln_swish(y, lw, lb):
    mu = jnp.mean(y, -1, keepdims=True)
    yc = y - mu
    yn = yc * lax.rsqrt(jnp.mean(yc * yc, -1, keepdims=True) + EPS) * lw + lb
    return yn * _sigmoid(yn)


def _conv_taps_strided(f_ref, w_ref, cb_ref, y_ref, off, ts, c, width):
    nseg = ts // SUBLANES
    lanes = slice(c * LANES, (c + 1) * LANES)
    wv = [jnp.broadcast_to(w_ref[j:j + 1, lanes], (SUBLANES, LANES)) for j in range(width)]
    acc = [None] * nseg
    for v in range(nseg + width - 1):
        yv = f_ref[c, pl.ds(v + off, SUBLANES, stride=nseg), :]
        for u in range(max(0, v - (width - 1)), min(nseg - 1, v) + 1):
            term = wv[v - u] * yv
            acc[u] = term if acc[u] is None else acc[u] + term
    cb = cb_ref[:, lanes]
    for u in range(nseg):
        y_ref[c, pl.ds(u, SUBLANES, stride=nseg), :] = acc[u] + cb


def _conv_seq_kernel(av_ref, ag_ref, st_ref, w_ref, cb_ref, lw_ref, lb_ref, o_ref, ns_ref, f_ref, y_ref,
                     *, ts, width):
    i = pl.program_id(1)
    off = CONV_HALO - (width - 1)
    nchunk = f_ref.shape[0]
    chunks = [slice(c * LANES, (c + 1) * LANES) for c in range(nchunk)]

    @pl.when(i == 0)
    def _():
        for c, lanes in enumerate(chunks):
            f_ref[c, 0:off, :] = jnp.zeros((off, LANES), F32)
            f_ref[c, off:CONV_HALO, :] = st_ref[:, lanes]

    u = av_ref[...] * _sigmoid(ag_ref[...])
    for c, lanes in enumerate(chunks):
        f_ref[c, CONV_HALO:CONV_HALO + ts, :] = u[:, lanes]
    for c in range(nchunk):
        _conv_taps_strided(f_ref, w_ref, cb_ref, y_ref, off, ts, c, width)
    y = jnp.concatenate([y_ref[c] for c in range(nchunk)], axis=1)
    o_ref[...] = _ln_swish(y, lw_ref[...], lb_ref[...]).astype(o_ref.dtype)

    @pl.when(i == pl.num_programs(1) - 1)
    def _():
        for c, lanes in enumerate(chunks):
            ns_ref[:, lanes] = f_ref[c, ts + off:ts + CONV_HALO, :]

    for c in range(nchunk):
        f_ref[c, 0:CONV_HALO, :] = f_ref[c, ts:ts + CONV_HALO, :]


def _conv_seq(proj, state, w, cb, lw, lb, bsz, s):
    width, ch = w.shape
    ts = _pick(s, (128, 64, 32))
    ns = s // ts
    assert ch % LANES == 0 and ts >= CONV_HALO >= width - 1
    kern = functools.partial(_conv_seq_kernel, ts=ts, width=width)
    vec = pl.BlockSpec((1, ch), lambda b, i: (0, 0))
    return pl.pallas_call(
        kern,
        grid=(bsz, ns),
        in_specs=[pl.BlockSpec((ts, ch), lambda b, i: (b * ns + i, 0)),
                  pl.BlockSpec((ts, ch), lambda b, i: (b * ns + i, 1)),
                  pl.BlockSpec((None, width - 1, ch), lambda b, i: (b, 0, 0)),
                  pl.BlockSpec((width, ch), lambda b, i: (0, 0)),
                  vec, vec, vec],
        out_specs=[pl.BlockSpec((ts, ch), lambda b, i: (b * ns + i, 0)),
                   pl.BlockSpec((None, width - 1, ch), lambda b, i: (b, 0, 0))],
        out_shape=[jax.ShapeDtypeStruct((bsz * s, ch), BF16),
                   jax.ShapeDtypeStruct((bsz, width - 1, ch), F32)],
        scratch_shapes=[pltpu.VMEM((ch // LANES, CONV_HALO + ts, LANES), F32),
                        pltpu.VMEM((ch // LANES, ts, LANES), F32)],
        compiler_params=_params(("arbitrary", "arbitrary")),
        name="conv_seq",
    )(proj, proj, state, w, cb, lw, lb)


def _conv_step_kernel(av_ref, ag_ref, st_ref, w_ref, cb_ref, lw_ref, lb_ref, o_ref, ns_ref, f_ref, y_ref,
                      *, nb, s, width, cc):
    nst = width - 1
    ch = f_ref.shape[1]

    def body(q, carry):
        r = pl.multiple_of(q * s, s)
        f_ref[0:nst, :] = st_ref[q]
        f_ref[nst:nst + s, :] = av_ref[pl.ds(r, s), :] * _sigmoid(ag_ref[pl.ds(r, s), :])
        for c0 in range(0, ch, cc):
            acc = _conv_taps(f_ref, w_ref, 0, s, c0, cc, width)
            y_ref[pl.ds(r, s), c0:c0 + cc] = acc + cb_ref[:, c0:c0 + cc]
        ns_ref[q] = f_ref[s:s + nst, :]
        return carry

    lax.fori_loop(0, nb, body, 0)
    o_ref[...] = _ln_swish(y_ref[...], lw_ref[...], lb_ref[...]).astype(o_ref.dtype)


def _conv_step(proj, state, w, cb, lw, lb, bsz, s):
    width, ch = w.shape
    assert s % SUBLANES == 0
    nb = _pick(bsz, (16, 8, 4, 2, 1))
    cc = _pick(ch, (512, 256, 128))
    kern = functools.partial(_conv_step_kernel, nb=nb, s=s, width=width, cc=cc)
    vec = pl.BlockSpec((1, ch), lambda b: (0, 0))
    frows = -(-(width - 1 + s) // SUBLANES) * SUBLANES
    return pl.pallas_call(
        kern,
        grid=(bsz // nb,),
        in_specs=[pl.BlockSpec((nb * s, ch), lambda b: (b, 0)),
                  pl.BlockSpec((nb * s, ch), lambda b: (b, 1)),
                  pl.BlockSpec((nb, width - 1, ch), lambda b: (b, 0, 0)),
                  pl.BlockSpec((width, ch), lambda b: (0, 0)),
                  vec, vec, vec],
        out_specs=[pl.BlockSpec((nb * s, ch), lambda b: (b, 0)),
                   pl.BlockSpec((nb, width - 1, ch), lambda b: (b, 0, 0))],
        out_shape=[jax.ShapeDtypeStruct((bsz * s, ch), BF16),
                   jax.ShapeDtypeStruct((bsz, width - 1, ch), F32)],
        scratch_shapes=[pltpu.VMEM((frows, ch), F32), pltpu.VMEM((nb * s, ch), F32)],
        compiler_params=_params(("parallel",)),
        name="conv_step",
    )(proj, proj, state, w, cb, lw, lb)


def _mlstm_kernel(q_ref, k_ref, v_ref, o_ref, g_ref, gt_ref, c0_ref, n0_ref, m0_ref, nw_ref,
                  hm_ref, c_out, n_out, m_out, c_s, n_s, m_s, *, n_heads, scale):
    c = pl.program_id(1)
    last = pl.num_programs(1) - 1
    ln = q_ref.shape[0]
    dqk = q_ref.shape[1] // n_heads
    dv = v_ref.shape[1] // n_heads

    @pl.when(c == 0)
    def _():
        c_s[...] = c0_ref[...]
        n_s[...] = n0_ref[...]
        m_s[...] = m0_ref[...]

    row = lax.broadcasted_iota(I32, (ln, ln), 0)
    col = lax.broadcasted_iota(I32, (ln, ln), 1)
    tri = row >= col
    tri_t = row <= col
    g = g_ref[...]
    gt = gt_ref[...]

    for hd in range(n_heads):
        li_c = g[:, hd:hd + 1]
        lf_c = g[:, n_heads + hd:n_heads + hd + 1]
        li_r = gt[hd:hd + 1, :]
        lf_r = gt[n_heads + hd:n_heads + hd + 1, :]
        b_c = jnp.sum(jnp.where(tri, lf_r, 0.0), axis=1, keepdims=True)
        b_r = jnp.sum(jnp.where(tri_t, lf_c, 0.0), axis=0, keepdims=True)
        b_l = jnp.sum(lf_r, axis=1, keepdims=True)
        m_prev = m_s[:, hd:hd + 1]

        dmat = jnp.where(tri, b_c - b_r + li_r, NEG_BIG)
        inter = b_c + m_prev
        m_t = jnp.maximum(inter, jnp.max(dmat, axis=1, keepdims=True))
        a = jnp.exp(inter - m_t)

        q = q_ref[:, hd * dqk:(hd + 1) * dqk]
        k = k_ref[:, hd * dqk:(hd + 1) * dqk] * scale
        qb = q.astype(BF16)
        vb = v_ref[:, hd * dv:(hd + 1) * dv].astype(BF16)
        cst = c_s[hd]
        nst = n_s[hd:hd + 1, :]
        s = lax.dot_general(qb, k.astype(BF16), (((1,), (1,)), ((), ())), preferred_element_type=F32)
        s = s * jnp.exp(dmat - m_t)
        num = a * jnp.dot(qb, cst.astype(BF16), preferred_element_type=F32) \
            + jnp.dot(s.astype(BF16), vb, preferred_element_type=F32)
        den = a * jnp.sum(q * nst, axis=1, keepdims=True) + jnp.sum(s, axis=1, keepdims=True)
        h = num / jnp.maximum(jnp.abs(den), jnp.exp(-m_t))
        hn = h * lax.rsqrt(jnp.mean(h * h, -1, keepdims=True) + EPS) * nw_ref[hd:hd + 1, :]
        hm_ref[:, hd * dv:(hd + 1) * dv] = (hn * _sigmoid(o_ref[:, hd * dv:(hd + 1) * dv])).astype(hm_ref.dtype)

        g_r = b_l - b_r + li_r
        g_c = b_l - b_c + li_c
        m_new = jnp.maximum(b_l + m_prev, jnp.max(g_r, axis=1, keepdims=True))
        decay = jnp.exp(b_l + m_prev - m_new)
        kw = k * jnp.exp(g_c - m_new)
        c_new = decay * cst + lax.dot_general(kw.astype(BF16), vb, (((0,), (0,)), ((), ())),
                                              preferred_element_type=F32)
        n_new = decay * nst + jnp.sum(kw, axis=0, keepdims=True)
        c_s[hd] = c_new
        n_s[hd:hd + 1, :] = n_new
        m_s[:, hd:hd + 1] = m_new

    @pl.when(c == last)
    def _():
        c_out[...] = c_s[...]
        n_out[...] = n_s[...]
        m_out[...] = m_s[...]


def _mlstm(proj, g, c0, n0, m0, norm_w, bsz, s, col_q):
    _, n_heads, dqk, dv = c0.shape
    ln = s if s <= MLSTM_CHUNK else MLSTM_CHUNK
    assert s % ln == 0 and ln % SUBLANES == 0
    nc = s // ln
    wqk, wv = n_heads * dqk, n_heads * dv
    assert col_q % wqk == 0 and (col_q + 2 * wqk) % wv == 0
    qb0 = col_q // wqk
    vb0 = (col_q + 2 * wqk) // wv
    gt3 = g[:, :2 * n_heads].reshape(bsz * nc, ln, 2 * n_heads).transpose(0, 2, 1)
    kern = functools.partial(_mlstm_kernel, n_heads=n_heads, scale=dqk ** -0.5)
    rows = lambda col: (lambda b, c: (b * nc + c, col))
    per_seq = lambda shape: pl.BlockSpec((None,) + shape, lambda b, c: (b,) + (0,) * len(shape))
    out = pl.pallas_call(
        kern,
        grid=(bsz, nc),
        in_specs=[pl.BlockSpec((ln, wqk), rows(qb0)), pl.BlockSpec((ln, wqk), rows(qb0 + 1)),
                  pl.BlockSpec((ln, wv), rows(vb0)), pl.BlockSpec((ln, wv), rows(vb0 + 1)),
                  pl.BlockSpec((ln, LANES), rows(0)),
                  pl.BlockSpec((None, 2 * n_heads, ln), lambda b, c: (b * nc + c, 0, 0)),
                  per_seq((n_heads, dqk, dv)), per_seq((n_heads, dqk)), per_seq((1, n_heads)),
                  pl.BlockSpec((n_heads, dv), lambda b, c: (0, 0))],
        out_specs=[pl.BlockSpec((ln, wv), rows(0)),
                   per_seq((n_heads, dqk, dv)), per_seq((n_heads, dqk)), per_seq((1, n_heads))],
        out_shape=[jax.ShapeDtypeStruct((bsz * s, wv), BF16),
                   jax.ShapeDtypeStruct((bsz, n_heads, dqk, dv), F32),
                   jax.ShapeDtypeStruct((bsz, n_heads, dqk), F32),
                   jax.ShapeDtypeStruct((bsz, 1, n_heads), F32)],
        scratch_shapes=[pltpu.VMEM((n_heads, dqk, dv), F32), pltpu.VMEM((n_heads, dqk), F32),
                        pltpu.VMEM((1, n_heads), F32)],
        compiler_params=_params(("arbitrary", "arbitrary")),
        name="mlstm",
    )(proj, proj, proj, proj, g, gt3, c0, n0, m0.reshape(bsz, 1, n_heads), norm_w)
    hm, c_new, n_new, m_new = out
    return hm, c_new, n_new, m_new.reshape(bsz, n_heads)


def _outproj_kernel(co_ref, hm_ref, w_ref, o_ref, wbf_ref):
    @pl.when(pl.program_id(1) == 0)
    def _():
        wbf_ref[...] = w_ref[...].astype(BF16)

    kc = co_ref.shape[1]
    o_ref[...] = jnp.dot(co_ref[...], wbf_ref[0:kc, :], preferred_element_type=F32) \
        + jnp.dot(hm_ref[...], wbf_ref[kc:2 * kc, :], preferred_element_type=F32)


def _outproj(co, hm, w_out, l):
    m, kc = co.shape
    d = w_out.shape[-1]
    assert hm.shape[1] == kc and w_out.shape[1] == 2 * kc
    tn = _pick(d, (512, 256, 128))
    tm = _pick(m, (512, 256, 128, 64, 32, 16, 8))
    return pl.pallas_call(
        _outproj_kernel,
        grid=(d // tn, m // tm),
        in_specs=[pl.BlockSpec((tm, kc), lambda j, i: (i, 0)),
                  pl.BlockSpec((tm, kc), lambda j, i: (i, 0)),
                  pl.BlockSpec((None, 2 * kc, tn), lambda j, i: (l, 0, j))],
        out_specs=pl.BlockSpec((tm, tn), lambda j, i: (i, j)),
        out_shape=jax.ShapeDtypeStruct((m, d), F32),
        scratch_shapes=[pltpu.VMEM((2 * kc, tn), BF16)],
        compiler_params=_params(("arbitrary", "arbitrary")),
        name="outproj",
    )(co, hm, w_out)


def _router_kernel(mix_ref, x_ref, g1_ref, sc2_ref, sh2_ref, n1_ref, n2_ref, wr_ref, br_ref,
                   x1_ref, h2_ref, ti_ref, tg_ref, *, st):
    mix = mix_ref[...]
    mn = mix * lax.rsqrt(jnp.mean(mix * mix, -1, keepdims=True) + EPS) * n1_ref[...]
    x1 = x_ref[...] + _mod_rows(g1_ref, st) * mn
    x1_ref[...] = x1
    y2 = x1 * lax.rsqrt(jnp.mean(x1 * x1, -1, keepdims=True) + EPS) * n2_ref[...]
    h2 = y2 * (1.0 + _mod_rows(sc2_ref, st)) + _mod_rows(sh2_ref, st)
    half = h2.shape[1] // 2
    h2_ref[...] = _pack_pair(h2[:, :half], h2[:, half:])
    logits = jnp.dot(h2, wr_ref[...], preferred_element_type=F32,
                     precision=lax.Precision.HIGHEST) + br_ref[...]
    n_exp = logits.shape[1]
    lane = lax.broadcasted_iota(I32, logits.shape, 1)
    lane_o = lax.broadcasted_iota(I32, ti_ref.shape, 1)
    idx_out = jnp.zeros(ti_ref.shape, I32)
    val_out = jnp.zeros(tg_ref.shape, F32)
    top = None
    den = jnp.zeros((logits.shape[0], 1), F32)
    for r in range(TOP_K):
        mx = jnp.max(logits, axis=1, keepdims=True)
        ix = jnp.min(jnp.where(logits == mx, lane, n_exp), axis=1, keepdims=True)
        if top is None:
            top = mx
        e = jnp.exp(mx - top)
        den = den + e
        idx_out = jnp.where(lane_o == r, ix, idx_out)
        val_out = jnp.where(lane_o == r, e, val_out)
        logits = jnp.where(lane == ix, NEG_BIG, logits)
    ti_ref[...] = idx_out
    tg_ref[...] = val_out / den


def _router(mix, x2, mod3, grp, n1, n2, wr, br):
    rows, d = x2.shape
    n_exp = wr.shape[1]
    kern = functools.partial(_router_kernel, st=grp.st)
    return pl.pallas_call(
        kern,
        grid=grp.grid,
        in_specs=[grp.rows(d), grp.rows(d), grp.mod(d, 2), grp.mod(d, 4), grp.mod(d, 3),
                  grp.const((1, d)), grp.const((1, d)), grp.const((d, n_exp)), grp.const((1, n_exp))],
        out_specs=[grp.rows(d), grp.rows(d // 2), grp.rows(LANES), grp.rows(LANES)],
        out_shape=[jax.ShapeDtypeStruct((rows, d), F32), jax.ShapeDtypeStruct((rows, d // 2), U32),
                   jax.ShapeDtypeStruct((rows, LANES), I32), jax.ShapeDtypeStruct((rows, LANES), F32)],
        compiler_params=_params(("parallel", "parallel")),
        name="router",
    )(mix, x2, mod3, mod3, mod3, n1, n2, wr, br)


def _rank_kernel(ti_ref, ps_ref, o_ref, carry_ref):
    @pl.when(pl.program_id(0) == 0)
    def _():
        carry_ref[...] = jnp.zeros(carry_ref.shape, F32)

    ti = ti_ref[...].astype(F32)
    tt = ti.shape[0]
    lane = lax.broadcasted_iota(I32, ti.shape, 1)
    lane_f = lane.astype(F32)
    cols = []
    member = jnp.zeros(ti.shape, F32)
    for kk in range(TOP_K):
        ek = jnp.sum(jnp.where(lane == kk, ti, 0.0), axis=1, keepdims=True)
        cols.append(ek)
        member = member + jnp.where(lane_f == ek, 1.0, 0.0)
    r = lax.broadcasted_iota(I32, (tt, tt), 0)
    c = lax.broadcasted_iota(I32, (tt, tt), 1)
    before = jnp.where(r > c, 1.0, 0.0).astype(BF16)
    base = jnp.dot(before, member.astype(BF16), preferred_element_type=F32) + carry_ref[...] + ps_ref[...]
    out = jnp.zeros(o_ref.shape, I32)
    for kk in range(TOP_K):
        dk = jnp.sum(jnp.where(lane_f == cols[kk], base, 0.0), axis=1, keepdims=True)
        out = jnp.where(lane == kk, dk.astype(I32), out)
    o_ref[...] = out
    carry_ref[...] += jnp.sum(member, axis=0, keepdims=True)


def _rank(top_i, pad_start):
    t = top_i.shape[0]
    tt = _pick(t, (512, 256, 128, 64, 32, 16, 8))
    return pl.pallas_call(
        _rank_kernel,
        grid=(t // tt,),
        in_specs=[pl.BlockSpec((tt, LANES), lambda i: (i, 0)), pl.BlockSpec((1, LANES), lambda i: (0, 0))],
        out_specs=pl.BlockSpec((tt, LANES), lambda i: (i, 0)),
        out_shape=jax.ShapeDtypeStruct((t, LANES), I32),
        scratch_shapes=[pltpu.VMEM((1, LANES), F32)],
        compiler_params=_params(("arbitrary",)),
        name="moe_rank",
    )(top_i, pad_start)


def _route(top_i, n_experts, tm):
    t = top_i.shape[0]
    a = t * TOP_K
    flat_e = top_i[:, :TOP_K].reshape(a)
    counts = jnp.sum((flat_e[:, None] == jnp.arange(n_experts, dtype=I32)[None, :]).astype(I32), axis=0)
    padded = (counts + tm - 1) // tm * tm
    pad_end = jnp.cumsum(padded)
    pad_start = pad_end - padded
    nt = a // tm + n_experts
    tile_start = jnp.arange(nt, dtype=I32) * tm
    tile_u = (tile_start < pad_end[-1]).astype(I32)
    n_used = jnp.sum(tile_u)
    tile_e = jnp.sum((pad_end[None, :] <= tile_start[:, None]).astype(I32), axis=1)
    tile_e = jnp.minimum(tile_e, n_experts - 1)
    tile_e = jnp.where(tile_u == 1, tile_e, tile_e[jnp.maximum(n_used - 1, 0)])
    idx = jnp.arange(nt, dtype=I32)
    starts = jnp.logical_and(tile_u == 1, jnp.logical_or(idx == 0, tile_e != jnp.roll(tile_e, 1)))
    seg = (jnp.cumsum(starts.astype(I32)) - 1).astype(I32)
    later = lax.cummin(jnp.where(starts, idx, nt)[::-1])[::-1]
    nxt = jnp.concatenate([later[1:], jnp.full((1,), nt, I32)])
    next_e = tile_e[jnp.where(nxt >= nt, 0, nxt)]
    n_seg = jnp.sum(starts.astype(I32)).reshape(1)
    tiles = (tile_e, tile_u, seg, next_e, n_seg)
    ps = jnp.zeros((1, LANES), F32).at[0, :n_experts].set(pad_start.astype(F32))
    dest = _rank(top_i, ps)
    return dest, tiles, nt


def _row_copy(src_ref, dst_ref, sem, src_row, dst_row):
    return pltpu.make_async_copy(src_ref.at[pl.ds(src_row, 1), :], dst_ref.at[pl.ds(dst_row, 1), :], sem)


def _dispatch_kernel(dest_ref, src_ref, xs_in, xs_out, sem):
    del xs_in
    tt = src_ref.shape[0]

    def start(r, carry):
        for kk in range(TOP_K):
            _row_copy(src_ref, xs_out, sem, r, dest_ref[0, r * TOP_K + kk]).start()
        return carry

    def wait(r, carry):
        for kk in range(TOP_K):
            _row_copy(src_ref, xs_out, sem, r, 0).wait()
        return carry

    lax.fori_loop(0, tt, start, 0)
    lax.fori_loop(0, tt, wait, 0)


def _dispatch(h2p, dest, xs):
    t, w = h2p.shape
    tt = _pick(t, (256, 128, 64, 32, 16, 8))
    nt = t // tt
    return pl.pallas_call(
        _dispatch_kernel,
        grid=(nt,),
        in_specs=[pl.BlockSpec((None, 1, tt * TOP_K), lambda i: (i, 0, 0), memory_space=pltpu.SMEM),
                  pl.BlockSpec((tt, w), lambda i: (i, 0)),
                  pl.BlockSpec(memory_space=pl.ANY)],
        out_specs=pl.BlockSpec(memory_space=pl.ANY),
        out_shape=jax.ShapeDtypeStruct(xs.shape, xs.dtype),
        scratch_shapes=[pltpu.SemaphoreType.DMA(())],
        input_output_aliases={2: 0},
        compiler_params=_params(("arbitrary",)),
        name="moe_dispatch",
    )(dest[:, :TOP_K].reshape(nt, 1, tt * TOP_K), h2p, xs)


N_TILE_TABLES = 5


def _segment_weights(tables, w_hbm, wst_ref, sem, *, l, tn, col_offs):
    te_ref, tu_ref, sg_ref, ne_ref, ns_ref = tables
    j = pl.program_id(0)
    i = pl.program_id(1)
    n_seg = ns_ref[0]
    first = jnp.logical_and(tu_ref[i] == 1,
                            jnp.logical_or(i == 0, te_ref[i] != te_ref[jnp.maximum(i - 1, 0)]))
    g = j * n_seg + sg_ref[i]
    slot = lax.rem(g, 2)

    def copies(e, jj, sl):
        return [pltpu.make_async_copy(
            w_hbm.at[l, e, :, pl.ds(pl.multiple_of((off + jj) * tn, tn), tn)], wst_ref.at[sl, m], sem.at[sl])
            for m, off in enumerate(col_offs)]

    @pl.when(jnp.logical_and(j == 0, i == 0))
    def _():
        for cp in copies(te_ref[0], 0, 0):
            cp.start()

    @pl.when(first)
    def _():
        for cp in copies(te_ref[i], j, slot):
            cp.wait()

        @pl.when(g + 1 < pl.num_programs(0) * n_seg)
        def _():
            jn = jnp.where(sg_ref[i] + 1 == n_seg, j + 1, j)
            for cp in copies(ne_ref[i], jn, 1 - slot):
                cp.start()

    return first, slot


def _gmm1_kernel(*refs, l, tn, nj):
    tables = refs[:N_TILE_TABLES]
    x_ref, w_hbm, bg_ref, bu_ref, o_ref, wst_ref, wgb_ref, wub_ref, sem = refs[N_TILE_TABLES:]
    used = tables[1][pl.program_id(1)] == 1
    first, slot = _segment_weights(tables, w_hbm, wst_ref, sem, l=l, tn=tn, col_offs=(0, nj))

    @pl.when(first)
    def _():
        wgb_ref[...] = wst_ref[slot, 0].astype(BF16)
        wub_ref[...] = wst_ref[slot, 1].astype(BF16)

    @pl.when(used)
    def _():
        half = x_ref.shape[1]
        xlo, xhi = _unpack_pair(x_ref[...])
        gt = jnp.dot(xlo, wgb_ref[0:half, :], preferred_element_type=F32) \
            + jnp.dot(xhi, wgb_ref[half:2 * half, :], preferred_element_type=F32) + bg_ref[...]
        up = jnp.dot(xlo, wub_ref[0:half, :], preferred_element_type=F32) \
            + jnp.dot(xhi, wub_ref[half:2 * half, :], preferred_element_type=F32) + bu_ref[...]
        gt = jnp.minimum(gt, SWIGLU_LIMIT)
        up = jnp.clip(up, -SWIGLU_LIMIT, SWIGLU_LIMIT)
        act = (up + 1.0) * gt * _sigmoid(SWIGLU_ALPHA * gt)
        o_ref[...] = act.astype(o_ref.dtype)

    @pl.when(jnp.logical_not(used))
    def _():
        o_ref[...] = jnp.zeros(o_ref.shape, o_ref.dtype)


def _gmm1(xs, w1, b1, tiles, l, tm):
    rows, half = xs.shape
    d = 2 * half
    dff = w1.shape[-1] // 2
    nt = rows // tm
    tn = _pick(dff, (512, 256, 128))
    nj = dff // tn
    b1r = b1.reshape(b1.shape[0], b1.shape[1], 1, 2 * dff)
    kern = functools.partial(_gmm1_kernel, l=l, tn=tn, nj=nj)
    grid_spec = pltpu.PrefetchScalarGridSpec(
        num_scalar_prefetch=N_TILE_TABLES,
        grid=(nj, nt),
        in_specs=[pl.BlockSpec((tm, half), lambda j, i, *t: (i, 0)),
                  pl.BlockSpec(memory_space=pl.ANY),
                  pl.BlockSpec((None, None, 1, tn), lambda j, i, te, *t: (l, te[i], 0, j)),
                  pl.BlockSpec((None, None, 1, tn), lambda j, i, te, *t: (l, te[i], 0, nj + j))],
        out_specs=pl.BlockSpec((tm, tn), lambda j, i, *t: (i, j)),
        scratch_shapes=[pltpu.VMEM((2, 2, d, tn), F32), pltpu.VMEM((d, tn), BF16), pltpu.VMEM((d, tn), BF16),
                        pltpu.SemaphoreType.DMA((2,))])
    return pl.pallas_call(
        kern,
        grid_spec=grid_spec,
        out_shape=jax.ShapeDtypeStruct((rows, dff), BF16),
        compiler_params=_params(("arbitrary", "arbitrary")),
        name="moe_gmm1",
    )(*tiles, xs, w1, b1r, b1r)


def _gmm2_kernel(*refs, l, tn):
    tables = refs[:N_TILE_TABLES]
    a_ref, w_hbm, b_ref, o_ref, wst_ref, wb_ref, sem = refs[N_TILE_TABLES:]
    used = tables[1][pl.program_id(1)] == 1
    first, slot = _segment_weights(tables, w_hbm, wst_ref, sem, l=l, tn=tn, col_offs=(0,))

    @pl.when(first)
    def _():
        wb_ref[...] = wst_ref[slot, 0].astype(BF16)

    @pl.when(used)
    def _():
        y = jnp.dot(a_ref[...], wb_ref[...], preferred_element_type=F32) + b_ref[...]
        o_ref[...] = _pack_pair(y[:, :tn // 2], y[:, tn // 2:])

    @pl.when(jnp.logical_not(used))
    def _():
        o_ref[...] = jnp.zeros(o_ref.shape, o_ref.dtype)


def _gmm2(act, w2, b2, tiles, l, tm):
    rows, dff = act.shape
    d = w2.shape[-1]
    nt = rows // tm
    tn = _pick(d, (2048, 1024, 512, 256, 128))
    b2r = b2.reshape(b2.shape[0], b2.shape[1], 1, d)
    kern = functools.partial(_gmm2_kernel, l=l, tn=tn)
    grid_spec = pltpu.PrefetchScalarGridSpec(
        num_scalar_prefetch=N_TILE_TABLES,
        grid=(d // tn, nt),
        in_specs=[pl.BlockSpec((tm, dff), lambda j, i, *t: (i, 0)),
                  pl.BlockSpec(memory_space=pl.ANY),
                  pl.BlockSpec((None, None, 1, tn), lambda j, i, te, *t: (l, te[i], 0, j))],
        out_specs=pl.BlockSpec((tm, tn // 2), lambda j, i, *t: (i, j)),
        scratch_shapes=[pltpu.VMEM((2, 1, dff, tn), F32), pltpu.VMEM((dff, tn), BF16),
                        pltpu.SemaphoreType.DMA((2,))])
    y = pl.pallas_call(
        kern,
        grid_spec=grid_spec,
        out_shape=jax.ShapeDtypeStruct((rows, d // 2), U32),
        compiler_params=_params(("arbitrary", "arbitrary")),
        name="moe_gmm2",
    )(*tiles, act, w2, b2r)
    return y, tn // 2


def _combine_kernel(pos_ref, posn_ref, y_hbm, tg_ref, x1_ref, g2_ref, nw_ref, o_ref, buf_ref, sem, *, st, pw):
    ni = pl.num_programs(1)
    n = pl.program_id(0) * ni + pl.program_id(1)
    total = pl.num_programs(0) * ni
    tt, d = o_ref.shape
    slot = lax.rem(n, 2)

    def rows_of(idx_ref, sl, start):
        def body(r, carry):
            for kk in range(TOP_K):
                src = idx_ref[0, r * TOP_K + kk] if start else 0
                cp = _row_copy(y_hbm, buf_ref.at[sl, kk], sem.at[sl], src, r)
                if start:
                    cp.start()
                else:
                    cp.wait()
            return carry
        lax.fori_loop(0, tt, body, 0)

    @pl.when(n == 0)
    def _():
        rows_of(pos_ref, 0, True)

    @pl.when(n + 1 < total)
    def _():
        rows_of(posn_ref, 1 - slot, True)

    rows_of(pos_ref, slot, False)

    gates = tg_ref[...]
    lane = lax.broadcasted_iota(I32, gates.shape, 1)
    gk = [jnp.sum(jnp.where(lane == kk, gates, 0.0), axis=1, keepdims=True) for kk in range(TOP_K)]
    ssq = jnp.zeros((tt, 1), F32)
    for blk in range(d // (2 * pw)):
        lo = jnp.zeros((tt, pw), F32)
        hi = jnp.zeros((tt, pw), F32)
        for kk in range(TOP_K):
            p = buf_ref[slot, kk, :, blk * pw:(blk + 1) * pw]
            lo = lo + gk[kk] * lax.bitcast_convert_type(p << 16, F32)
            hi = hi + gk[kk] * lax.bitcast_convert_type(p & jnp.uint32(HI_HALF), F32)
        ssq = ssq + jnp.sum(lo * lo, -1, keepdims=True) + jnp.sum(hi * hi, -1, keepdims=True)
        o_ref[:, 2 * pw * blk:2 * pw * blk + pw] = lo
        o_ref[:, 2 * pw * blk + pw:2 * pw * (blk + 1)] = hi
    fn = o_ref[...] * lax.rsqrt(ssq / d + EPS) * nw_ref[...]
    o_ref[...] = x1_ref[...] + _mod_rows(g2_ref, st) * fn


def _combine(y, pw, dest, tg, x1, mod3, grp, nw):
    rows, d = x1.shape
    tt = grp.ts
    nt = rows // tt
    ni = grp.n_inner
    kern = functools.partial(_combine_kernel, st=grp.st, pw=pw)
    dest3 = dest[:, :TOP_K].reshape(nt, 1, tt * TOP_K)
    idx_spec = lambda ahead: pl.BlockSpec(
        (None, 1, tt * TOP_K), lambda o, i: (jnp.minimum(o * ni + i + ahead, nt - 1), 0, 0),
        memory_space=pltpu.SMEM)
    return pl.pallas_call(
        kern,
        grid=grp.grid,
        in_specs=[idx_spec(0), idx_spec(1),
                  pl.BlockSpec(memory_space=pl.ANY),
                  grp.rows(LANES), grp.rows(d), grp.mod(d, 5), grp.const((1, d))],
        out_specs=grp.rows(d),
        out_shape=jax.ShapeDtypeStruct((rows, d), F32),
        scratch_shapes=[pltpu.VMEM((2, TOP_K, tt, d // 2), U32), pltpu.SemaphoreType.DMA((2,))],
        compiler_params=_params(("arbitrary", "arbitrary")),
        name="moe_combine",
    )(dest3, dest3, y, tg, x1, mod3, nw)


def _mixer(x2, mod3, grp, conv_buf, c0, n0, m0, p, l):
    n_heads = c0.shape[1]
    n_main = p['w_in'].shape[-1] - 2 * n_heads
    ch = p['conv_w'].shape[-1]
    h, g = _prenorm(x2, mod3, grp, p['norm1_pre'], p['w_gate'], p['b_gate'], n_heads)
    proj = _inproj(h, p['w_in'], l, n_main)
    conv = _conv_seq if grp.nbq == 1 else _conv_step
    co, new_buf = conv(proj, conv_buf, p['conv_w'], p['conv_b'], p['conv_ln_w'], p['conv_ln_b'], grp.bsz, grp.s)
    hm, c_new, n_new, m_new = _mlstm(proj, g, c0, n0, m0, p['mlstm_norm_w'], grp.bsz, grp.s, 2 * ch)
    mix = _outproj(co, hm, p['w_out'], l)
    x1, h2p, ti, tg = _router(mix, x2, mod3, grp, p['norm1_post'], p['norm2_pre'], p['w_router'], p['b_router'])
    return dict(x1=x1, h2p=h2p, ti=ti, tg=tg, state=(new_buf, c_new, n_new, m_new))


def kernel(x_prompt, x_sample, c_prompt, c_sample, state_conv, state_mlstm_C, state_mlstm_n, state_mlstm_m,
           w_ada, b_ada, norm1_pre, w_in, b_gates, conv_w, conv_b, conv_ln_w, conv_ln_b, mlstm_norm_w,
           w_out, norm1_post, norm2_pre, w_router, b_router, w1, b1, w2, b2, norm2_post):
    depth = w_ada.shape[0]
    bp, sp, d = x_prompt.shape
    bs, ss, _ = x_sample.shape
    n_heads, dqk, dv = state_mlstm_C.shape[2:]
    n_experts = w_router.shape[-1]
    nst, ch = state_conv.shape[2:]
    n_gate = 2 * n_heads
    assert n_gate <= LANES and n_experts <= LANES
    tp, tsmp = bp * sp, bs * ss

    mp = -(-(bp + bs) // SUBLANES) * SUBLANES
    c_all = jnp.zeros((mp, d), F32).at[:bs].set(c_sample).at[bs:bs + bp].set(c_prompt)
    grp_p = _Group(bp, sp, bs, ROW_TILE)
    grp_s = _Group(bs, ss, 0, ROW_TILE)

    xp, xs = x_prompt.reshape(tp, d), x_sample.reshape(tsmp, d)
    outs = [[] for _ in range(8)]
    for l in range(depth):
        row = lambda v: v[l].reshape(1, -1)
        p = dict(
            w_in=w_in, w_out=w_out,
            w_gate=jnp.pad(w_in[l, :, w_in.shape[-1] - n_gate:], ((0, 0), (0, LANES - n_gate))),
            b_gate=jnp.pad(b_gates[l], (0, LANES - n_gate)).reshape(1, LANES),
            norm1_pre=row(norm1_pre), conv_w=conv_w[l], conv_b=row(conv_b), conv_ln_w=row(conv_ln_w),
            conv_ln_b=row(conv_ln_b), mlstm_norm_w=mlstm_norm_w[l], norm1_post=row(norm1_post),
            norm2_pre=row(norm2_pre), w_router=w_router[l], b_router=row(b_router))
        mod3 = _ada(c_all, w_ada, b_ada, l).reshape(mp, 1, 6 * d)
        zero = lambda shape: jnp.zeros(shape, F32)
        gp = _mixer(xp, mod3, grp_p, zero((bp, nst, ch)), zero((bp, n_heads, dqk, dv)),
                    zero((bp, n_heads, dqk)), zero((bp, n_heads)), p, l)
        gs = _mixer(xs, mod3, grp_s, state_conv[l], state_mlstm_C[l], state_mlstm_n[l],
                    state_mlstm_m[l], p, l)

        top_i = jnp.concatenate([gp['ti'], gs['ti']], axis=0)
        dest, tiles, nt = _route(top_i, n_experts, MOE_ROWS)
        xsorted = jnp.zeros((nt * MOE_ROWS, d // 2), U32)
        xsorted = _dispatch(gp['h2p'], dest[:tp], xsorted)
        xsorted = _dispatch(gs['h2p'], dest[tp:], xsorted)
        act = _gmm1(xsorted, w1, b1, tiles, l, MOE_ROWS)
        y, pw = _gmm2(act, w2, b2, tiles, l, MOE_ROWS)
        nw2 = row(norm2_post)
        xp = _combine(y, pw, dest[:tp], gp['tg'], gp['x1'], mod3, grp_p, nw2)
        xs = _combine(y, pw, dest[tp:], gs['tg'], gs['x1'], mod3, grp_s, nw2)
        for o, v in zip(outs, gp['state'] + gs['state']):
            o.append(v)
    stack = (lambda o: o[0][None]) if depth == 1 else jnp.stack
    return (xp.reshape(bp, sp, d), xs.reshape(bs, ss, d)) + tuple(stack(o) for o in outs)
```

```python
import functools

import jax
import jax.numpy as jnp
from jax import lax
from jax.experimental import pallas as pl
from jax.experimental.pallas import tpu as pltpu

F32 = jnp.float32
BF16 = jnp.bfloat16
I32 = jnp.int32
U32 = jnp.uint32

EPS = 1e-6
GATE_CAP = 15.0
TOP_K = 4
SWIGLU_LIMIT = 7.0
SWIGLU_ALPHA = 1.702
NEG_BIG = -1e30

LANES = 128
SUBLANES = 8
VMEM_LIMIT = 56 * 1024 * 1024
MLSTM_CHUNK = 256
MOE_ROWS = 256
ROW_TILE = 256
HI_HALF = 0xFFFF0000
ISSUE_UNROLL = 4


def _params(sem):
    return pltpu.CompilerParams(dimension_semantics=sem, vmem_limit_bytes=VMEM_LIMIT)


def _sigmoid(x):
    return 1.0 / (1.0 + jnp.exp(-x))


def _pick(n, prefs):
    for p in prefs:
        if n % p == 0:
            return p
    return n


def _pack_pair(lo, hi):
    lo_b = lax.bitcast_convert_type(lo.astype(BF16).astype(F32), U32) >> 16
    hi_b = lax.bitcast_convert_type(hi.astype(BF16).astype(F32), U32) & jnp.uint32(HI_HALF)
    return hi_b | lo_b


def _unpack_pair(p):
    lo = lax.bitcast_convert_type(p << 16, F32).astype(BF16)
    hi = lax.bitcast_convert_type(p & jnp.uint32(HI_HALF), F32).astype(BF16)
    return lo, hi


class _Group:
    def __init__(self, bsz, s, mod_row0, tile_rows):
        self.bsz, self.s = bsz, s
        if s >= tile_rows:
            self.nbq, self.st = 1, _pick(s, (tile_rows, 128, 64, 32, 16, 8))
            self.n_outer, self.n_inner = bsz, s // self.st
        else:
            self.nbq, self.st = _pick(bsz, (tile_rows // s, 8, 4, 2, 1)), s
            self.n_outer, self.n_inner = bsz // self.nbq, 1
        assert mod_row0 % self.nbq == 0
        self.mod_blk0 = mod_row0 // self.nbq
        self.ts = self.nbq * self.st
        self.grid = (self.n_outer, self.n_inner)

    def rows(self, width, col=0):
        ni = self.n_inner
        return pl.BlockSpec((self.ts, width), lambda o, i, *_: (o * ni + i, col))

    def mod(self, d, col):
        b0 = self.mod_blk0
        return pl.BlockSpec((self.nbq, 1, d), lambda o, i, *_: (b0 + o, 0, col))

    def const(self, shape):
        nd = len(shape)
        return pl.BlockSpec(shape, lambda o, i, *_: (0,) * nd)


def _mod_rows(m_ref, st):
    m = m_ref[...]
    nbq, _, d = m.shape
    if nbq == 1:
        return m[0]
    return jnp.broadcast_to(m, (nbq, st, d)).reshape(nbq * st, d)


def _ada_kernel(c_ref, w_ref, b_ref, o_ref):
    c = c_ref[...]
    s = (c * _sigmoid(c)).astype(BF16)
    o_ref[...] = jnp.dot(s, w_ref[...].astype(BF16), preferred_element_type=F32) + b_ref[...]


def _ada(c_all, w_ada, b_ada, l):
    mp, d = c_all.shape
    n = w_ada.shape[-1]
    tn = _pick(n, (512, 256, 128))
    return pl.pallas_call(
        _ada_kernel,
        grid=(n // tn,),
        in_specs=[pl.BlockSpec((mp, d), lambda j: (0, 0)),
                  pl.BlockSpec((None, d, tn), lambda j: (l, 0, j)),
                  pl.BlockSpec((None, 1, tn), lambda j: (l, 0, j))],
        out_specs=pl.BlockSpec((mp, tn), lambda j: (0, j)),
        out_shape=jax.ShapeDtypeStruct((mp, n), F32),
        compiler_params=_params(("parallel",)),
        name="ada",
    )(c_all, w_ada, b_ada.reshape(b_ada.shape[0], 1, n))


def _prenorm_kernel(x_ref, sc_ref, sh_ref, nw_ref, wg_ref, bg_ref, h_ref, g_ref, *, n_heads, st):
    x = x_ref[...]
    y = x * lax.rsqrt(jnp.mean(x * x, -1, keepdims=True) + EPS) * nw_ref[...]
    h = y * (1.0 + _mod_rows(sc_ref, st)) + _mod_rows(sh_ref, st)
    h_ref[...] = h.astype(BF16)
    z = jnp.dot(h, wg_ref[...], preferred_element_type=F32, precision=lax.Precision.HIGHEST) + bg_ref[...]
    cap = GATE_CAP * jnp.tanh(z / GATE_CAP)
    logsig = jnp.minimum(cap, 0.0) - jnp.log(1.0 + jnp.exp(-jnp.abs(cap)))
    lane = lax.broadcasted_iota(I32, z.shape, 1)
    g_ref[...] = jnp.where(lane < n_heads, cap, logsig)


def _prenorm(x2, mod3, grp, nw, wg, bg, n_heads):
    rows, d = x2.shape
    kern = functools.partial(_prenorm_kernel, n_heads=n_heads, st=grp.st)
    return pl.pallas_call(
        kern,
        grid=grp.grid,
        in_specs=[grp.rows(d), grp.mod(d, 1), grp.mod(d, 0), grp.const((1, d)),
                  grp.const((d, LANES)), grp.const((1, LANES))],
        out_specs=[grp.rows(d), grp.rows(LANES)],
        out_shape=[jax.ShapeDtypeStruct((rows, d), BF16), jax.ShapeDtypeStruct((rows, LANES), F32)],
        compiler_params=_params(("parallel", "parallel")),
        name="prenorm",
    )(x2, mod3, mod3, nw, wg, bg)


def _inproj_kernel(a_ref, w_ref, o_ref):
    o_ref[...] = jnp.dot(a_ref[...], w_ref[...], preferred_element_type=F32)


def _inproj(h, w_bf):
    m, d = h.shape
    n = w_bf.shape[1]
    tn = _pick(n, (2048, 1024, 512, 256, 128))
    tm = _pick(m, (512, 256, 128, 64, 32, 16, 8))
    return pl.pallas_call(
        _inproj_kernel,
        grid=(n // tn, m // tm),
        in_specs=[pl.BlockSpec((tm, d), lambda j, i: (i, 0)),
                  pl.BlockSpec((d, tn), lambda j, i: (0, j))],
        out_specs=pl.BlockSpec((tm, tn), lambda j, i: (i, j)),
        out_shape=jax.ShapeDtypeStruct((m, n), F32),
        compiler_params=_params(("parallel", "parallel")),
        name="inproj",
    )(h, w_bf)


CONV_HALO = 32


def _conv_taps(f_ref, w_ref, base, rows, c0, cc, width):
    acc = jnp.zeros((rows, cc), F32)
    for ph in range(SUBLANES):
        if ph >= width:
            break
        n_al = (width - 1 - ph) // SUBLANES + 1
        gb = f_ref[base + ph: base + ph + rows + SUBLANES * (n_al - 1), c0:c0 + cc]
        for a in range(n_al):
            j = SUBLANES * a + ph
            acc = acc + w_ref[j:j + 1, c0:c0 + cc] * gb[SUBLANES * a:SUBLANES * a + rows]
    return acc


def _ln_swish(y, lw, lb):
    mu = jnp.mean(y, -1, keepdims=True)
    yc = y - mu
    yn = yc * lax.rsqrt(jnp.mean(yc * yc, -1, keepdims=True) + EPS) * lw + lb
    return yn * _sigmoid(yn)


def _conv_taps_strided(f_ref, w_ref, cb_ref, y_ref, off, ts, c, width):
    nseg = ts // SUBLANES
    lanes = slice(c * LANES, (c + 1) * LANES)
    wv = [jnp.broadcast_to(w_ref[j:j + 1, lanes], (SUBLANES, LANES)) for j in range(width)]
    acc = [None] * nseg
    for v in range(nseg + width - 1):
        yv = f_ref[c, pl.ds(v + off, SUBLANES, stride=nseg), :]
        for u in range(max(0, v - (width - 1)), min(nseg - 1, v) + 1):
            term = wv[v - u] * yv
            acc[u] = term if acc[u] is None else acc[u] + term
    cb = cb_ref[:, lanes]
    for u in range(nseg):
        y_ref[c, pl.ds(u, SUBLANES, stride=nseg), :] = acc[u] + cb


def _conv_seq_kernel(av_ref, ag_ref, st_ref, w_ref, cb_ref, lw_ref, lb_ref, o_ref, ns_ref, f_ref, y_ref,
                     *, ts, width):
    i = pl.program_id(1)
    off = CONV_HALO - (width - 1)
    nchunk = f_ref.shape[0]
    chunks = [slice(c * LANES, (c + 1) * LANES) for c in range(nchunk)]

    @pl.when(i == 0)
    def _():
        for c, lanes in enumerate(chunks):
            f_ref[c, 0:off, :] = jnp.zeros((off, LANES), F32)
            f_ref[c, off:CONV_HALO, :] = st_ref[:, lanes]

    u = av_ref[...] * _sigmoid(ag_ref[...])
    for c, lanes in enumerate(chunks):
        f_ref[c, CONV_HALO:CONV_HALO + ts, :] = u[:, lanes]
    for c in range(nchunk):
        _conv_taps_strided(f_ref, w_ref, cb_ref, y_ref, off, ts, c, width)
    y = jnp.concatenate([y_ref[c] for c in range(nchunk)], axis=1)
    o_ref[...] = _ln_swish(y, lw_ref[...], lb_ref[...]).astype(o_ref.dtype)

    @pl.when(i == pl.num_programs(1) - 1)
    def _():
        for c, lanes in enumerate(chunks):
            ns_ref[:, lanes] = f_ref[c, ts + off:ts + CONV_HALO, :]

    for c in range(nchunk):
        f_ref[c, 0:CONV_HALO, :] = f_ref[c, ts:ts + CONV_HALO, :]


def _conv_seq(proj, state, w, cb, lw, lb, bsz, s):
    width, ch = w.shape
    ts = _pick(s, (128, 64, 32))
    ns = s // ts
    assert ch % LANES == 0 and ts >= CONV_HALO >= width - 1
    kern = functools.partial(_conv_seq_kernel, ts=ts, width=width)
    vec = pl.BlockSpec((1, ch), lambda b, i: (0, 0))
    return pl.pallas_call(
        kern,
        grid=(bsz, ns),
        in_specs=[pl.BlockSpec((ts, ch), lambda b, i: (b * ns + i, 0)),
                  pl.BlockSpec((ts, ch), lambda b, i: (b * ns + i, 1)),
                  pl.BlockSpec((None, width - 1, ch), lambda b, i: (b, 0, 0)),
                  pl.BlockSpec((width, ch), lambda b, i: (0, 0)),
                  vec, vec, vec],
        out_specs=[pl.BlockSpec((ts, ch), lambda b, i: (b * ns + i, 0)),
                   pl.BlockSpec((None, width - 1, ch), lambda b, i: (b, 0, 0))],
        out_shape=[jax.ShapeDtypeStruct((bsz * s, ch), BF16),
                   jax.ShapeDtypeStruct((bsz, width - 1, ch), F32)],
        scratch_shapes=[pltpu.VMEM((ch // LANES, CONV_HALO + ts, LANES), F32),
                        pltpu.VMEM((ch // LANES, ts, LANES), F32)],
        compiler_params=_params(("arbitrary", "arbitrary")),
        name="conv_seq",
    )(proj, proj, state, w, cb, lw, lb)


def _conv_step_kernel(av_ref, ag_ref, st_ref, w_ref, cb_ref, lw_ref, lb_ref, o_ref, ns_ref, f_ref, y_ref,
                      *, nb, s, width, cc):
    nst = width - 1
    ch = f_ref.shape[1]

    def body(q, carry):
        r = pl.multiple_of(q * s, s)
        f_ref[0:nst, :] = st_ref[q]
        f_ref[nst:nst + s, :] = av_ref[pl.ds(r, s), :] * _sigmoid(ag_ref[pl.ds(r, s), :])
        for c0 in range(0, ch, cc):
            acc = _conv_taps(f_ref, w_ref, 0, s, c0, cc, width)
            y_ref[pl.ds(r, s), c0:c0 + cc] = acc + cb_ref[:, c0:c0 + cc]
        ns_ref[q] = f_ref[s:s + nst, :]
        return carry

    lax.fori_loop(0, nb, body, 0)
    o_ref[...] = _ln_swish(y_ref[...], lw_ref[...], lb_ref[...]).astype(o_ref.dtype)


def _conv_step(proj, state, w, cb, lw, lb, bsz, s):
    width, ch = w.shape
    assert s % SUBLANES == 0
    nb = _pick(bsz, (16, 8, 4, 2, 1))
    cc = _pick(ch, (512, 256, 128))
    kern = functools.partial(_conv_step_kernel, nb=nb, s=s, width=width, cc=cc)
    vec = pl.BlockSpec((1, ch), lambda b: (0, 0))
    frows = -(-(width - 1 + s) // SUBLANES) * SUBLANES
    return pl.pallas_call(
        kern,
        grid=(bsz // nb,),
        in_specs=[pl.BlockSpec((nb * s, ch), lambda b: (b, 0)),
                  pl.BlockSpec((nb * s, ch), lambda b: (b, 1)),
                  pl.BlockSpec((nb, width - 1, ch), lambda b: (b, 0, 0)),
                  pl.BlockSpec((width, ch), lambda b: (0, 0)),
                  vec, vec, vec],
        out_specs=[pl.BlockSpec((nb * s, ch), lambda b: (b, 0)),
                   pl.BlockSpec((nb, width - 1, ch), lambda b: (b, 0, 0))],
        out_shape=[jax.ShapeDtypeStruct((bsz * s, ch), BF16),
                   jax.ShapeDtypeStruct((bsz, width - 1, ch), F32)],
        scratch_shapes=[pltpu.VMEM((frows, ch), F32), pltpu.VMEM((nb * s, ch), F32)],
        compiler_params=_params(("parallel",)),
        name="conv_step",
    )(proj, proj, state, w, cb, lw, lb)


def _mlstm_kernel(q_ref, k_ref, v_ref, o_ref, g_ref, gt_ref, c0_ref, n0_ref, m0_ref, nw_ref,
                  hm_ref, c_out, n_out, m_out, c_s, n_s, m_s, *, n_heads, scale):
    c = pl.program_id(1)
    last = pl.num_programs(1) - 1
    ln = q_ref.shape[0]
    dqk = q_ref.shape[1] // n_heads
    dv = v_ref.shape[1] // n_heads

    @pl.when(c == 0)
    def _():
        c_s[...] = c0_ref[...]
        n_s[...] = n0_ref[...]
        m_s[...] = m0_ref[...]

    row = lax.broadcasted_iota(I32, (ln, ln), 0)
    col = lax.broadcasted_iota(I32, (ln, ln), 1)
    tri = row >= col
    tri_t = row <= col
    g = g_ref[...]
    gt = gt_ref[...]

    for hd in range(n_heads):
        li_c = g[:, hd:hd + 1]
        lf_c = g[:, n_heads + hd:n_heads + hd + 1]
        li_r = gt[hd:hd + 1, :]
        lf_r = gt[n_heads + hd:n_heads + hd + 1, :]
        b_c = jnp.sum(jnp.where(tri, lf_r, 0.0), axis=1, keepdims=True)
        b_r = jnp.sum(jnp.where(tri_t, lf_c, 0.0), axis=0, keepdims=True)
        b_l = jnp.sum(lf_r, axis=1, keepdims=True)
        m_prev = m_s[:, hd:hd + 1]

        dmat = jnp.where(tri, b_c - b_r + li_r, NEG_BIG)
        inter = b_c + m_prev
        m_t = jnp.maximum(inter, jnp.max(dmat, axis=1, keepdims=True))
        a = jnp.exp(inter - m_t)

        q = q_ref[:, hd * dqk:(hd + 1) * dqk]
        k = k_ref[:, hd * dqk:(hd + 1) * dqk] * scale
        qb = q.astype(BF16)
        vb = v_ref[:, hd * dv:(hd + 1) * dv].astype(BF16)
        cst = c_s[hd]
        nst = n_s[hd:hd + 1, :]
        s = lax.dot_general(qb, k.astype(BF16), (((1,), (1,)), ((), ())), preferred_element_type=F32)
        s = s * jnp.exp(dmat - m_t)
        num = a * jnp.dot(qb, cst.astype(BF16), preferred_element_type=F32) \
            + jnp.dot(s.astype(BF16), vb, preferred_element_type=F32)
        den = a * jnp.sum(q * nst, axis=1, keepdims=True) + jnp.sum(s, axis=1, keepdims=True)
        h = num / jnp.maximum(jnp.abs(den), jnp.exp(-m_t))
        hn = h * lax.rsqrt(jnp.mean(h * h, -1, keepdims=True) + EPS) * nw_ref[hd:hd + 1, :]
        hm_ref[:, hd * dv:(hd + 1) * dv] = (hn * _sigmoid(o_ref[:, hd * dv:(hd + 1) * dv])).astype(hm_ref.dtype)

        g_r = b_l - b_r + li_r
        g_c = b_l - b_c + li_c
        m_new = jnp.maximum(b_l + m_prev, jnp.max(g_r, axis=1, keepdims=True))
        decay = jnp.exp(b_l + m_prev - m_new)
        kw = k * jnp.exp(g_c - m_new)
        c_new = decay * cst + lax.dot_general(kw.astype(BF16), vb, (((0,), (0,)), ((), ())),
                                              preferred_element_type=F32)
        n_new = decay * nst + jnp.sum(kw, axis=0, keepdims=True)
        c_s[hd] = c_new
        n_s[hd:hd + 1, :] = n_new
        m_s[:, hd:hd + 1] = m_new

    @pl.when(c == last)
    def _():
        c_out[...] = c_s[...]
        n_out[...] = n_s[...]
        m_out[...] = m_s[...]


def _mlstm(proj, g, c0, n0, m0, norm_w, bsz, s, col_q):
    _, n_heads, dqk, dv = c0.shape
    ln = s if s <= MLSTM_CHUNK else MLSTM_CHUNK
    assert s % ln == 0 and ln % SUBLANES == 0
    nc = s // ln
    wqk, wv = n_heads * dqk, n_heads * dv
    assert col_q % wqk == 0 and (col_q + 2 * wqk) % wv == 0
    qb0 = col_q // wqk
    vb0 = (col_q + 2 * wqk) // wv
    gt3 = g[:, :2 * n_heads].reshape(bsz * nc, ln, 2 * n_heads).transpose(0, 2, 1)
    kern = functools.partial(_mlstm_kernel, n_heads=n_heads, scale=dqk ** -0.5)
    rows = lambda col: (lambda b, c: (b * nc + c, col))
    per_seq = lambda shape: pl.BlockSpec((None,) + shape, lambda b, c: (b,) + (0,) * len(shape))
    out = pl.pallas_call(
        kern,
        grid=(bsz, nc),
        in_specs=[pl.BlockSpec((ln, wqk), rows(qb0)), pl.BlockSpec((ln, wqk), rows(qb0 + 1)),
                  pl.BlockSpec((ln, wv), rows(vb0)), pl.BlockSpec((ln, wv), rows(vb0 + 1)),
                  pl.BlockSpec((ln, LANES), rows(0)),
                  pl.BlockSpec((None, 2 * n_heads, ln), lambda b, c: (b * nc + c, 0, 0)),
                  per_seq((n_heads, dqk, dv)), per_seq((n_heads, dqk)), per_seq((1, n_heads)),
                  pl.BlockSpec((n_heads, dv), lambda b, c: (0, 0))],
        out_specs=[pl.BlockSpec((ln, wv), rows(0)),
                   per_seq((n_heads, dqk, dv)), per_seq((n_heads, dqk)), per_seq((1, n_heads))],
        out_shape=[jax.ShapeDtypeStruct((bsz * s, wv), BF16),
                   jax.ShapeDtypeStruct((bsz, n_heads, dqk, dv), F32),
                   jax.ShapeDtypeStruct((bsz, n_heads, dqk), F32),
                   jax.ShapeDtypeStruct((bsz, 1, n_heads), F32)],
        scratch_shapes=[pltpu.VMEM((n_heads, dqk, dv), F32), pltpu.VMEM((n_heads, dqk), F32),
                        pltpu.VMEM((1, n_heads), F32)],
        compiler_params=_params(("arbitrary", "arbitrary")),
        name="mlstm",
    )(proj, proj, proj, proj, g, gt3, c0, n0, m0.reshape(bsz, 1, n_heads), norm_w)
    hm, c_new, n_new, m_new = out
    return hm, c_new, n_new, m_new.reshape(bsz, n_heads)


def _outproj_kernel(co_ref, hm_ref, w_ref, o_ref):
    kc = co_ref.shape[1]
    o_ref[...] = jnp.dot(co_ref[...], w_ref[0:kc, :], preferred_element_type=F32) \
        + jnp.dot(hm_ref[...], w_ref[kc:2 * kc, :], preferred_element_type=F32)


def _outproj(co, hm, w_bf):
    m, kc = co.shape
    d = w_bf.shape[-1]
    assert hm.shape[1] == kc and w_bf.shape[0] == 2 * kc
    tn = _pick(d, (2048, 1024, 512, 256, 128))
    tm = _pick(m, (512, 256, 128, 64, 32, 16, 8))
    return pl.pallas_call(
        _outproj_kernel,
        grid=(d // tn, m // tm),
        in_specs=[pl.BlockSpec((tm, kc), lambda j, i: (i, 0)),
                  pl.BlockSpec((tm, kc), lambda j, i: (i, 0)),
                  pl.BlockSpec((2 * kc, tn), lambda j, i: (0, j))],
        out_specs=pl.BlockSpec((tm, tn), lambda j, i: (i, j)),
        out_shape=jax.ShapeDtypeStruct((m, d), F32),
        compiler_params=_params(("parallel", "parallel")),
        name="outproj",
    )(co, hm, w_bf)


def _router_kernel(mix_ref, x_ref, g1_ref, sc2_ref, sh2_ref, n1_ref, n2_ref, wr_ref, br_ref,
                   x1_ref, h2_ref, ti_ref, tg_ref, *, st):
    mix = mix_ref[...]
    mn = mix * lax.rsqrt(jnp.mean(mix * mix, -1, keepdims=True) + EPS) * n1_ref[...]
    x1 = x_ref[...] + _mod_rows(g1_ref, st) * mn
    x1_ref[...] = x1
    y2 = x1 * lax.rsqrt(jnp.mean(x1 * x1, -1, keepdims=True) + EPS) * n2_ref[...]
    h2 = y2 * (1.0 + _mod_rows(sc2_ref, st)) + _mod_rows(sh2_ref, st)
    half = h2.shape[1] // 2
    h2_ref[...] = _pack_pair(h2[:, :half], h2[:, half:])
    logits = jnp.dot(h2, wr_ref[...], preferred_element_type=F32,
                     precision=lax.Precision.HIGHEST) + br_ref[...]
    n_exp = logits.shape[1]
    lane = lax.broadcasted_iota(I32, logits.shape, 1)
    lane_o = lax.broadcasted_iota(I32, ti_ref.shape, 1)
    idx_out = jnp.zeros(ti_ref.shape, I32)
    val_out = jnp.zeros(tg_ref.shape, F32)
    top = None
    den = jnp.zeros((logits.shape[0], 1), F32)
    for r in range(TOP_K):
        mx = jnp.max(logits, axis=1, keepdims=True)
        ix = jnp.min(jnp.where(logits == mx, lane, n_exp), axis=1, keepdims=True)
        if top is None:
            top = mx
        e = jnp.exp(mx - top)
        den = den + e
        idx_out = jnp.where(lane_o == r, ix, idx_out)
        val_out = jnp.where(lane_o == r, e, val_out)
        logits = jnp.where(lane == ix, NEG_BIG, logits)
    ti_ref[...] = idx_out
    tg_ref[...] = val_out / den


def _router(mix, x2, mod3, grp, n1, n2, wr, br):
    rows, d = x2.shape
    n_exp = wr.shape[1]
    kern = functools.partial(_router_kernel, st=grp.st)
    return pl.pallas_call(
        kern,
        grid=grp.grid,
        in_specs=[grp.rows(d), grp.rows(d), grp.mod(d, 2), grp.mod(d, 4), grp.mod(d, 3),
                  grp.const((1, d)), grp.const((1, d)), grp.const((d, n_exp)), grp.const((1, n_exp))],
        out_specs=[grp.rows(d), grp.rows(d // 2), grp.rows(LANES), grp.rows(LANES)],
        out_shape=[jax.ShapeDtypeStruct((rows, d), F32), jax.ShapeDtypeStruct((rows, d // 2), U32),
                   jax.ShapeDtypeStruct((rows, LANES), I32), jax.ShapeDtypeStruct((rows, LANES), F32)],
        compiler_params=_params(("parallel", "parallel")),
        name="router",
    )(mix, x2, mod3, mod3, mod3, n1, n2, wr, br)


def _rank_kernel(ti_ref, ps_ref, o_ref, carry_ref):
    @pl.when(pl.program_id(0) == 0)
    def _():
        carry_ref[...] = jnp.zeros(carry_ref.shape, F32)

    ti = ti_ref[...].astype(F32)
    tt = ti.shape[0]
    lane = lax.broadcasted_iota(I32, ti.shape, 1)
    lane_f = lane.astype(F32)
    cols = []
    member = jnp.zeros(ti.shape, F32)
    for kk in range(TOP_K):
        ek = jnp.sum(jnp.where(lane == kk, ti, 0.0), axis=1, keepdims=True)
        cols.append(ek)
        member = member + jnp.where(lane_f == ek, 1.0, 0.0)
    r = lax.broadcasted_iota(I32, (tt, tt), 0)
    c = lax.broadcasted_iota(I32, (tt, tt), 1)
    before = jnp.where(r > c, 1.0, 0.0).astype(BF16)
    base = jnp.dot(before, member.astype(BF16), preferred_element_type=F32) + carry_ref[...] + ps_ref[...]
    out = jnp.zeros(o_ref.shape, I32)
    for kk in range(TOP_K):
        dk = jnp.sum(jnp.where(lane_f == cols[kk], base, 0.0), axis=1, keepdims=True)
        out = jnp.where(lane == kk, dk.astype(I32), out)
    o_ref[...] = out
    carry_ref[...] += jnp.sum(member, axis=0, keepdims=True)


def _rank(top_i, pad_start):
    t = top_i.shape[0]
    tt = _pick(t, (512, 256, 128, 64, 32, 16, 8))
    return pl.pallas_call(
        _rank_kernel,
        grid=(t // tt,),
        in_specs=[pl.BlockSpec((tt, LANES), lambda i: (i, 0)), pl.BlockSpec((1, LANES), lambda i: (0, 0))],
        out_specs=pl.BlockSpec((tt, LANES), lambda i: (i, 0)),
        out_shape=jax.ShapeDtypeStruct((t, LANES), I32),
        scratch_shapes=[pltpu.VMEM((1, LANES), F32)],
        compiler_params=_params(("arbitrary",)),
        name="moe_rank",
    )(top_i, pad_start)


def _route(top_i, n_experts, tm):
    t = top_i.shape[0]
    a = t * TOP_K
    flat_e = top_i[:, :TOP_K].reshape(a)
    counts = jnp.sum((flat_e[:, None] == jnp.arange(n_experts, dtype=I32)[None, :]).astype(I32), axis=0)
    padded = (counts + tm - 1) // tm * tm
    pad_end = jnp.cumsum(padded)
    pad_start = pad_end - padded
    nt = a // tm + n_experts
    tile_start = jnp.arange(nt, dtype=I32) * tm
    tile_u = (tile_start < pad_end[-1]).astype(I32)
    n_used = jnp.sum(tile_u)
    tile_e = jnp.sum((pad_end[None, :] <= tile_start[:, None]).astype(I32), axis=1)
    tile_e = jnp.minimum(tile_e, n_experts - 1)
    tile_e = jnp.where(tile_u == 1, tile_e, tile_e[jnp.maximum(n_used - 1, 0)])
    idx = jnp.arange(nt, dtype=I32)
    starts = jnp.logical_and(tile_u == 1, jnp.logical_or(idx == 0, tile_e != jnp.roll(tile_e, 1)))
    seg = (jnp.cumsum(starts.astype(I32)) - 1).astype(I32)
    later = lax.cummin(jnp.where(starts, idx, nt)[::-1])[::-1]
    nxt = jnp.concatenate([later[1:], jnp.full((1,), nt, I32)])
    next_e = tile_e[jnp.where(nxt >= nt, 0, nxt)]
    n_seg = jnp.sum(starts.astype(I32)).reshape(1)
    tiles = (tile_e, tile_u, seg, next_e, n_seg)
    ps = jnp.zeros((1, LANES), F32).at[0, :n_experts].set(pad_start.astype(F32))
    dest = _rank(top_i, ps)
    return dest, tiles, nt


def _row_copy(src_ref, dst_ref, sem, src_row, dst_row):
    return pltpu.make_async_copy(src_ref.at[pl.ds(src_row, 1), :], dst_ref.at[pl.ds(dst_row, 1), :], sem)


def _dispatch_kernel(dest_ref, src_ref, xs_in, xs_out, sem):
    del xs_in
    tt = src_ref.shape[0]

    def start(r, carry):
        for kk in range(TOP_K):
            _row_copy(src_ref, xs_out, sem, r, dest_ref[0, r * TOP_K + kk]).start()
        return carry

    lax.fori_loop(0, tt, start, 0, unroll=ISSUE_UNROLL)
    for _ in range(TOP_K):
        pltpu.make_async_copy(src_ref, src_ref, sem).wait()


def _dispatch(h2p, dest, xs):
    t, w = h2p.shape
    tt = _pick(t, (256, 128, 64, 32, 16, 8))
    nt = t // tt
    return pl.pallas_call(
        _dispatch_kernel,
        grid=(nt,),
        in_specs=[pl.BlockSpec((None, 1, tt * TOP_K), lambda i: (i, 0, 0), memory_space=pltpu.SMEM),
                  pl.BlockSpec((tt, w), lambda i: (i, 0)),
                  pl.BlockSpec(memory_space=pl.ANY)],
        out_specs=pl.BlockSpec(memory_space=pl.ANY),
        out_shape=jax.ShapeDtypeStruct(xs.shape, xs.dtype),
        scratch_shapes=[pltpu.SemaphoreType.DMA(())],
        input_output_aliases={2: 0},
        compiler_params=_params(("arbitrary",)),
        name="moe_dispatch",
    )(dest[:, :TOP_K].reshape(nt, 1, tt * TOP_K), h2p, xs)


N_TILE_TABLES = 5


def _segment_weights(tables, w_hbm, wst_ref, sem, *, l, tn, col_offs):
    te_ref, tu_ref, sg_ref, ne_ref, ns_ref = tables
    j = pl.program_id(0)
    i = pl.program_id(1)
    n_seg = ns_ref[0]
    first = jnp.logical_and(tu_ref[i] == 1,
                            jnp.logical_or(i == 0, te_ref[i] != te_ref[jnp.maximum(i - 1, 0)]))
    g = j * n_seg + sg_ref[i]
    slot = lax.rem(g, 2)

    def copies(e, jj, sl):
        return [pltpu.make_async_copy(
            w_hbm.at[l, e, :, pl.ds(pl.multiple_of((off + jj) * tn, tn), tn)], wst_ref.at[sl, m], sem.at[sl])
            for m, off in enumerate(col_offs)]

    @pl.when(jnp.logical_and(j == 0, i == 0))
    def _():
        for cp in copies(te_ref[0], 0, 0):
            cp.start()

    @pl.when(first)
    def _():
        for cp in copies(te_ref[i], j, slot):
            cp.wait()

        @pl.when(g + 1 < pl.num_programs(0) * n_seg)
        def _():
            jn = jnp.where(sg_ref[i] + 1 == n_seg, j + 1, j)
            for cp in copies(ne_ref[i], jn, 1 - slot):
                cp.start()

    return first, slot


def _gmm1_kernel(*refs, l, tn, nj):
    tables = refs[:N_TILE_TABLES]
    x_ref, w_hbm, bg_ref, bu_ref, o_ref, wst_ref, wgb_ref, wub_ref, sem = refs[N_TILE_TABLES:]
    used = tables[1][pl.program_id(1)] == 1
    first, slot = _segment_weights(tables, w_hbm, wst_ref, sem, l=l, tn=tn, col_offs=(0, nj))

    @pl.when(first)
    def _():
        wgb_ref[...] = wst_ref[slot, 0].astype(BF16)
        wub_ref[...] = wst_ref[slot, 1].astype(BF16)

    @pl.when(used)
    def _():
        half = x_ref.shape[1]
        xlo, xhi = _unpack_pair(x_ref[...])
        gt = jnp.dot(xlo, wgb_ref[0:half, :], preferred_element_type=F32) \
            + jnp.dot(xhi, wgb_ref[half:2 * half, :], preferred_element_type=F32) + bg_ref[...]
        up = jnp.dot(xlo, wub_ref[0:half, :], preferred_element_type=F32) \
            + jnp.dot(xhi, wub_ref[half:2 * half, :], preferred_element_type=F32) + bu_ref[...]
        gt = jnp.minimum(gt, SWIGLU_LIMIT)
        up = jnp.clip(up, -SWIGLU_LIMIT, SWIGLU_LIMIT)
        act = (up + 1.0) * gt * _sigmoid(SWIGLU_ALPHA * gt)
        o_ref[...] = act.astype(o_ref.dtype)

    @pl.when(jnp.logical_not(used))
    def _():
        o_ref[...] = jnp.zeros(o_ref.shape, o_ref.dtype)


def _gmm1(xs, w1, b1, tiles, l, tm):
    rows, half = xs.shape
    d = 2 * half
    dff = w1.shape[-1] // 2
    nt = rows // tm
    tn = _pick(dff, (512, 256, 128))
    nj = dff // tn
    b1r = b1.reshape(b1.shape[0], b1.shape[1], 1, 2 * dff)
    kern = functools.partial(_gmm1_kernel, l=l, tn=tn, nj=nj)
    grid_spec = pltpu.PrefetchScalarGridSpec(
        num_scalar_prefetch=N_TILE_TABLES,
        grid=(nj, nt),
        in_specs=[pl.BlockSpec((tm, half), lambda j, i, *t: (i, 0)),
                  pl.BlockSpec(memory_space=pl.ANY),
                  pl.BlockSpec((None, None, 1, tn), lambda j, i, te, *t: (l, te[i], 0, j)),
                  pl.BlockSpec((None, None, 1, tn), lambda j, i, te, *t: (l, te[i], 0, nj + j))],
        out_specs=pl.BlockSpec((tm, tn), lambda j, i, *t: (i, j)),
        scratch_shapes=[pltpu.VMEM((2, 2, d, tn), F32), pltpu.VMEM((d, tn), BF16), pltpu.VMEM((d, tn), BF16),
                        pltpu.SemaphoreType.DMA((2,))])
    return pl.pallas_call(
        kern,
        grid_spec=grid_spec,
        out_shape=jax.ShapeDtypeStruct((rows, dff), BF16),
        compiler_params=_params(("arbitrary", "arbitrary")),
        name="moe_gmm1",
    )(*tiles, xs, w1, b1r, b1r)


def _gmm2_kernel(*refs, l, tn):
    tables = refs[:N_TILE_TABLES]
    a_ref, w_hbm, b_ref, o_ref, wst_ref, wb_ref, sem = refs[N_TILE_TABLES:]
    used = tables[1][pl.program_id(1)] == 1
    first, slot = _segment_weights(tables, w_hbm, wst_ref, sem, l=l, tn=tn, col_offs=(0,))

    @pl.when(first)
    def _():
        wb_ref[...] = wst_ref[slot, 0].astype(BF16)

    @pl.when(used)
    def _():
        y = jnp.dot(a_ref[...], wb_ref[...], preferred_element_type=F32) + b_ref[...]
        o_ref[...] = _pack_pair(y[:, :tn // 2], y[:, tn // 2:])

    @pl.when(jnp.logical_not(used))
    def _():
        o_ref[...] = jnp.zeros(o_ref.shape, o_ref.dtype)


def _gmm2(act, w2, b2, tiles, l, tm):
    rows, dff = act.shape
    d = w2.shape[-1]
    nt = rows // tm
    tn = _pick(d, (2048, 1024, 512, 256, 128))
    b2r = b2.reshape(b2.shape[0], b2.shape[1], 1, d)
    kern = functools.partial(_gmm2_kernel, l=l, tn=tn)
    grid_spec = pltpu.PrefetchScalarGridSpec(
        num_scalar_prefetch=N_TILE_TABLES,
        grid=(d // tn, nt),
        in_specs=[pl.BlockSpec((tm, dff), lambda j, i, *t: (i, 0)),
                  pl.BlockSpec(memory_space=pl.ANY),
                  pl.BlockSpec((None, None, 1, tn), lambda j, i, te, *t: (l, te[i], 0, j))],
        out_specs=pl.BlockSpec((tm, tn // 2), lambda j, i, *t: (i, j)),
        scratch_shapes=[pltpu.VMEM((2, 1, dff, tn), F32), pltpu.VMEM((dff, tn), BF16),
                        pltpu.SemaphoreType.DMA((2,))])
    y = pl.pallas_call(
        kern,
        grid_spec=grid_spec,
        out_shape=jax.ShapeDtypeStruct((rows, d // 2), U32),
        compiler_params=_params(("arbitrary", "arbitrary")),
        name="moe_gmm2",
    )(*tiles, act, w2, b2r)
    return y, tn // 2


def _combine_kernel(pos_ref, posn_ref, y_hbm, tg_ref, x1_ref, g2_ref, nw_ref, o_ref, buf_ref, sem, *, st, pw):
    ni = pl.num_programs(1)
    n = pl.program_id(0) * ni + pl.program_id(1)
    total = pl.num_programs(0) * ni
    tt, d = o_ref.shape
    slot = lax.rem(n, 2)

    def start_rows(idx_ref, sl):
        def body(r, carry):
            for kk in range(TOP_K):
                _row_copy(y_hbm, buf_ref.at[sl, kk], sem.at[sl], idx_ref[0, r * TOP_K + kk], r).start()
            return carry
        lax.fori_loop(0, tt, body, 0, unroll=ISSUE_UNROLL)

    @pl.when(n == 0)
    def _():
        start_rows(pos_ref, 0)

    @pl.when(n + 1 < total)
    def _():
        start_rows(posn_ref, 1 - slot)

    pltpu.make_async_copy(buf_ref.at[slot], buf_ref.at[slot], sem.at[slot]).wait()

    gates = tg_ref[...]
    lane = lax.broadcasted_iota(I32, gates.shape, 1)
    gk = [jnp.sum(jnp.where(lane == kk, gates, 0.0), axis=1, keepdims=True) for kk in range(TOP_K)]
    ssq = jnp.zeros((tt, 1), F32)
    for blk in range(d // (2 * pw)):
        lo = jnp.zeros((tt, pw), F32)
        hi = jnp.zeros((tt, pw), F32)
        for kk in range(TOP_K):
            p = buf_ref[slot, kk, :, blk * pw:(blk + 1) * pw]
            lo = lo + gk[kk] * lax.bitcast_convert_type(p << 16, F32)
            hi = hi + gk[kk] * lax.bitcast_convert_type(p & jnp.uint32(HI_HALF), F32)
        ssq = ssq + jnp.sum(lo * lo, -1, keepdims=True) + jnp.sum(hi * hi, -1, keepdims=True)
        o_ref[:, 2 * pw * blk:2 * pw * blk + pw] = lo
        o_ref[:, 2 * pw * blk + pw:2 * pw * (blk + 1)] = hi
    fn = o_ref[...] * lax.rsqrt(ssq / d + EPS) * nw_ref[...]
    o_ref[...] = x1_ref[...] + _mod_rows(g2_ref, st) * fn


def _combine(y, pw, dest, tg, x1, mod3, grp, nw):
    rows, d = x1.shape
    tt = grp.ts
    nt = rows // tt
    ni = grp.n_inner
    kern = functools.partial(_combine_kernel, st=grp.st, pw=pw)
    dest3 = dest[:, :TOP_K].reshape(nt, 1, tt * TOP_K)
    idx_spec = lambda ahead: pl.BlockSpec(
        (None, 1, tt * TOP_K), lambda o, i: (jnp.minimum(o * ni + i + ahead, nt - 1), 0, 0),
        memory_space=pltpu.SMEM)
    return pl.pallas_call(
        kern,
        grid=grp.grid,
        in_specs=[idx_spec(0), idx_spec(1),
                  pl.BlockSpec(memory_space=pl.ANY),
                  grp.rows(LANES), grp.rows(d), grp.mod(d, 5), grp.const((1, d))],
        out_specs=grp.rows(d),
        out_shape=jax.ShapeDtypeStruct((rows, d), F32),
        scratch_shapes=[pltpu.VMEM((2, TOP_K, tt, d // 2), U32), pltpu.SemaphoreType.DMA((2,))],
        compiler_params=_params(("arbitrary", "arbitrary")),
        name="moe_combine",
    )(dest3, dest3, y, tg, x1, mod3, nw)


def _mixer(x2, mod3, grp, conv_buf, c0, n0, m0, p, l):
    n_heads = c0.shape[1]
    ch = p['conv_w'].shape[-1]
    h, g = _prenorm(x2, mod3, grp, p['norm1_pre'], p['w_gate'], p['b_gate'], n_heads)
    proj = _inproj(h, p['w_in'])
    conv = _conv_seq if grp.nbq == 1 else _conv_step
    co, new_buf = conv(proj, conv_buf, p['conv_w'], p['conv_b'], p['conv_ln_w'], p['conv_ln_b'], grp.bsz, grp.s)
    hm, c_new, n_new, m_new = _mlstm(proj, g, c0, n0, m0, p['mlstm_norm_w'], grp.bsz, grp.s, 2 * ch)
    mix = _outproj(co, hm, p['w_out'])
    x1, h2p, ti, tg = _router(mix, x2, mod3, grp, p['norm1_post'], p['norm2_pre'], p['w_router'], p['b_router'])
    return dict(x1=x1, h2p=h2p, ti=ti, tg=tg, state=(new_buf, c_new, n_new, m_new))


def kernel(x_prompt, x_sample, c_prompt, c_sample, state_conv, state_mlstm_C, state_mlstm_n, state_mlstm_m,
           w_ada, b_ada, norm1_pre, w_in, b_gates, conv_w, conv_b, conv_ln_w, conv_ln_b, mlstm_norm_w,
           w_out, norm1_post, norm2_pre, w_router, b_router, w1, b1, w2, b2, norm2_post):
    depth = w_ada.shape[0]
    bp, sp, d = x_prompt.shape
    bs, ss, _ = x_sample.shape
    n_heads, dqk, dv = state_mlstm_C.shape[2:]
    n_experts = w_router.shape[-1]
    nst, ch = state_conv.shape[2:]
    n_gate = 2 * n_heads
    assert n_gate <= LANES and n_experts <= LANES
    tp, tsmp = bp * sp, bs * ss

    mp = -(-(bp + bs) // SUBLANES) * SUBLANES
    c_all = jnp.zeros((mp, d), F32).at[:bs].set(c_sample).at[bs:bs + bp].set(c_prompt)
    grp_p = _Group(bp, sp, bs, ROW_TILE)
    grp_s = _Group(bs, ss, 0, ROW_TILE)

    xp, xs = x_prompt.reshape(tp, d), x_sample.reshape(tsmp, d)
    outs = [[] for _ in range(8)]
    for l in range(depth):
        row = lambda v: v[l].reshape(1, -1)
        p = dict(
            w_in=w_in[l, :, :w_in.shape[-1] - n_gate].astype(BF16), w_out=w_out[l].astype(BF16),
            w_gate=jnp.pad(w_in[l, :, w_in.shape[-1] - n_gate:], ((0, 0), (0, LANES - n_gate))),
            b_gate=jnp.pad(b_gates[l], (0, LANES - n_gate)).reshape(1, LANES),
            norm1_pre=row(norm1_pre), conv_w=conv_w[l], conv_b=row(conv_b), conv_ln_w=row(conv_ln_w),
            conv_ln_b=row(conv_ln_b), mlstm_norm_w=mlstm_norm_w[l], norm1_post=row(norm1_post),
            norm2_pre=row(norm2_pre), w_router=w_router[l], b_router=row(b_router))
        mod3 = _ada(c_all, w_ada, b_ada, l).reshape(mp, 1, 6 * d)
        zero = lambda shape: jnp.zeros(shape, F32)
        gp = _mixer(xp, mod3, grp_p, zero((bp, nst, ch)), zero((bp, n_heads, dqk, dv)),
                    zero((bp, n_heads, dqk)), zero((bp, n_heads)), p, l)
        gs = _mixer(xs, mod3, grp_s, state_conv[l], state_mlstm_C[l], state_mlstm_n[l],
                    state_mlstm_m[l], p, l)

        top_i = jnp.concatenate([gp['ti'], gs['ti']], axis=0)
        dest, tiles, nt = _route(top_i, n_experts, MOE_ROWS)
        xsorted = jnp.zeros((nt * MOE_ROWS, d // 2), U32)
        xsorted = _dispatch(gp['h2p'], dest[:tp], xsorted)
        xsorted = _dispatch(gs['h2p'], dest[tp:], xsorted)
        act = _gmm1(xsorted, w1, b1, tiles, l, MOE_ROWS)
        y, pw = _gmm2(act, w2, b2, tiles, l, MOE_ROWS)
        nw2 = row(norm2_post)
        xp = _combine(y, pw, dest[:tp], gp['tg'], gp['x1'], mod3, grp_p, nw2)
        xs = _combine(y, pw, dest[tp:], gs['tg'], gs['x1'], mod3, grp_s, nw2)
        for o, v in zip(outs, gp['state'] + gs['state']):
            o.append(v)
    stack = (lambda o: o[0][None]) if depth == 1 else jnp.stack
    return (xp.reshape(bp, sp, d), xs.reshape(bs, ss, d)) + tuple(stack(o) for o in outs)
```

```python
import functools

import jax
import jax.numpy as jnp
from jax import lax
from jax.experimental import pallas as pl
from jax.experimental.pallas import tpu as pltpu

F32 = jnp.float32
BF16 = jnp.bfloat16
I32 = jnp.int32
U32 = jnp.uint32

EPS = 1e-6
GATE_CAP = 15.0
TOP_K = 4
SWIGLU_LIMIT = 7.0
SWIGLU_ALPHA = 1.702
NEG_BIG = -1e30

LANES = 128
SUBLANES = 8
VMEM_LIMIT = 56 * 1024 * 1024
MLSTM_CHUNK = 256
MOE_ROWS = 256
ROW_TILE = 256
HI_HALF = 0xFFFF0000
ISSUE_UNROLL = 4


def _params(sem):
    return pltpu.CompilerParams(dimension_semantics=sem, vmem_limit_bytes=VMEM_LIMIT)


def _sigmoid(x):
    return 1.0 / (1.0 + jnp.exp(-x))


def _pick(n, prefs):
    for p in prefs:
        if n % p == 0:
            return p
    return n


def _pack_pair(lo, hi):
    lo_b = lax.bitcast_convert_type(lo.astype(BF16).astype(F32), U32) >> 16
    hi_b = lax.bitcast_convert_type(hi.astype(BF16).astype(F32), U32) & jnp.uint32(HI_HALF)
    return hi_b | lo_b


def _unpack_pair(p):
    lo = lax.bitcast_convert_type(p << 16, F32).astype(BF16)
    hi = lax.bitcast_convert_type(p & jnp.uint32(HI_HALF), F32).astype(BF16)
    return lo, hi


class _Group:
    def __init__(self, bsz, s, mod_row0, tile_rows):
        self.bsz, self.s = bsz, s
        if s >= tile_rows:
            self.nbq, self.st = 1, _pick(s, (tile_rows, 128, 64, 32, 16, 8))
            self.n_outer, self.n_inner = bsz, s // self.st
        else:
            self.nbq, self.st = _pick(bsz, (tile_rows // s, 8, 4, 2, 1)), s
            self.n_outer, self.n_inner = bsz // self.nbq, 1
        assert mod_row0 % self.nbq == 0
        self.mod_blk0 = mod_row0 // self.nbq
        self.ts = self.nbq * self.st
        self.grid = (self.n_outer, self.n_inner)

    def rows(self, width, col=0):
        ni = self.n_inner
        return pl.BlockSpec((self.ts, width), lambda o, i, *_: (o * ni + i, col))

    def mod(self, d, col):
        b0 = self.mod_blk0
        return pl.BlockSpec((self.nbq, 1, d), lambda o, i, *_: (b0 + o, 0, col))

    def const(self, shape):
        nd = len(shape)
        return pl.BlockSpec(shape, lambda o, i, *_: (0,) * nd)


def _mod_rows(m_ref, st):
    m = m_ref[...]
    nbq, _, d = m.shape
    if nbq == 1:
        return m[0]
    return jnp.broadcast_to(m, (nbq, st, d)).reshape(nbq * st, d)


def _ada_kernel(c_ref, w_ref, b_ref, o_ref):
    c = c_ref[...]
    s = (c * _sigmoid(c)).astype(BF16)
    o_ref[...] = jnp.dot(s, w_ref[...].astype(BF16), preferred_element_type=F32) + b_ref[...]


def _ada(c_all, w_ada, b_ada, l):
    mp, d = c_all.shape
    n = w_ada.shape[-1]
    tn = _pick(n, (512, 256, 128))
    return pl.pallas_call(
        _ada_kernel,
        grid=(n // tn,),
        in_specs=[pl.BlockSpec((mp, d), lambda j: (0, 0)),
                  pl.BlockSpec((None, d, tn), lambda j: (l, 0, j)),
                  pl.BlockSpec((None, 1, tn), lambda j: (l, 0, j))],
        out_specs=pl.BlockSpec((mp, tn), lambda j: (0, j)),
        out_shape=jax.ShapeDtypeStruct((mp, n), F32),
        compiler_params=_params(("parallel",)),
        name="ada",
    )(c_all, w_ada, b_ada.reshape(b_ada.shape[0], 1, n))


def _prenorm_kernel(x_ref, sc_ref, sh_ref, nw_ref, wg_ref, bg_ref, h_ref, g_ref, *, n_heads, st):
    x = x_ref[...]
    y = x * lax.rsqrt(jnp.mean(x * x, -1, keepdims=True) + EPS) * nw_ref[...]
    h = y * (1.0 + _mod_rows(sc_ref, st)) + _mod_rows(sh_ref, st)
    hb = h.astype(BF16)
    h_ref[...] = hb
    z = jnp.dot(hb, wg_ref[...], preferred_element_type=F32) + bg_ref[...]
    cap = GATE_CAP * jnp.tanh(z / GATE_CAP)
    logsig = jnp.minimum(cap, 0.0) - jnp.log(1.0 + jnp.exp(-jnp.abs(cap)))
    lane = lax.broadcasted_iota(I32, z.shape, 1)
    g_ref[...] = jnp.where(lane < n_heads, cap, logsig)


def _prenorm(x2, mod3, grp, nw, wg, bg, n_heads):
    rows, d = x2.shape
    kern = functools.partial(_prenorm_kernel, n_heads=n_heads, st=grp.st)
    return pl.pallas_call(
        kern,
        grid=grp.grid,
        in_specs=[grp.rows(d), grp.mod(d, 1), grp.mod(d, 0), grp.const((1, d)),
                  grp.const((d, LANES)), grp.const((1, LANES))],
        out_specs=[grp.rows(d), grp.rows(LANES)],
        out_shape=[jax.ShapeDtypeStruct((rows, d), BF16), jax.ShapeDtypeStruct((rows, LANES), F32)],
        compiler_params=_params(("parallel", "parallel")),
        name="prenorm",
    )(x2, mod3, mod3, nw, wg, bg)


def _inproj_kernel(a_ref, w_ref, o_ref):
    o_ref[...] = jnp.dot(a_ref[...], w_ref[...], preferred_element_type=F32)


def _inproj(h, w_bf, slab):
    m, d = h.shape
    n = w_bf.shape[1]
    assert n % slab == 0
    tm = _pick(m, (512, 256, 128, 64, 32, 16, 8))
    return pl.pallas_call(
        _inproj_kernel,
        grid=(n // slab, m // tm),
        in_specs=[pl.BlockSpec((tm, d), lambda j, i: (i, 0)),
                  pl.BlockSpec((d, slab), lambda j, i: (0, j))],
        out_specs=pl.BlockSpec((None, tm, slab), lambda j, i: (j, i, 0)),
        out_shape=jax.ShapeDtypeStruct((n // slab, m, slab), F32),
        compiler_params=_params(("parallel", "parallel")),
        name="inproj",
    )(h, w_bf)


CONV_HALO = 32


def _conv_taps(f_ref, w_ref, base, rows, c0, cc, width):
    acc = jnp.zeros((rows, cc), F32)
    for ph in range(SUBLANES):
        if ph >= width:
            break
        n_al = (width - 1 - ph) // SUBLANES + 1
        gb = f_ref[base + ph: base + ph + rows + SUBLANES * (n_al - 1), c0:c0 + cc]
        for a in range(n_al):
            j = SUBLANES * a + ph
            acc = acc + w_ref[j:j + 1, c0:c0 + cc] * gb[SUBLANES * a:SUBLANES * a + rows]
    return acc


def _ln_swish(y, lw, lb):
    mu = jnp.mean(y, -1, keepdims=True)
    yc = y - mu
    yn = yc * lax.rsqrt(jnp.mean(yc * yc, -1, keepdims=True) + EPS) * lw + lb
    return yn * _sigmoid(yn)


def _conv_taps_strided(f_ref, w_ref, cb_ref, y_ref, off, ts, c, width):
    nseg = ts // SUBLANES
    lanes = slice(c * LANES, (c + 1) * LANES)
    wv = [jnp.broadcast_to(w_ref[j:j + 1, lanes], (SUBLANES, LANES)) for j in range(width)]
    acc = [None] * nseg
    for v in range(nseg + width - 1):
        yv = f_ref[c, pl.ds(v + off, SUBLANES, stride=nseg), :]
        for u in range(max(0, v - (width - 1)), min(nseg - 1, v) + 1):
            term = wv[v - u] * yv
            acc[u] = term if acc[u] is None else acc[u] + term
    cb = cb_ref[:, lanes]
    for u in range(nseg):
        y_ref[c, pl.ds(u, SUBLANES, stride=nseg), :] = acc[u] + cb


def _conv_seq_kernel(av_ref, ag_ref, st_ref, w_ref, cb_ref, lw_ref, lb_ref, o_ref, ns_ref, f_ref, y_ref,
                     *, ts, width):
    i = pl.program_id(1)
    off = CONV_HALO - (width - 1)
    nchunk = f_ref.shape[0]
    chunks = [slice(c * LANES, (c + 1) * LANES) for c in range(nchunk)]

    @pl.when(i == 0)
    def _():
        for c, lanes in enumerate(chunks):
            f_ref[c, 0:off, :] = jnp.zeros((off, LANES), F32)
            f_ref[c, off:CONV_HALO, :] = st_ref[:, lanes]

    u = av_ref[...] * _sigmoid(ag_ref[...])
    for c, lanes in enumerate(chunks):
        f_ref[c, CONV_HALO:CONV_HALO + ts, :] = u[:, lanes]
    for c in range(nchunk):
        _conv_taps_strided(f_ref, w_ref, cb_ref, y_ref, off, ts, c, width)
    y = jnp.concatenate([y_ref[c] for c in range(nchunk)], axis=1)
    o_ref[...] = _ln_swish(y, lw_ref[...], lb_ref[...]).astype(o_ref.dtype)

    @pl.when(i == pl.num_programs(1) - 1)
    def _():
        for c, lanes in enumerate(chunks):
            ns_ref[:, lanes] = f_ref[c, ts + off:ts + CONV_HALO, :]

    for c in range(nchunk):
        f_ref[c, 0:CONV_HALO, :] = f_ref[c, ts:ts + CONV_HALO, :]


def _conv_seq(proj, state, w, cb, lw, lb, bsz, s):
    width, ch = w.shape
    assert proj.shape[2] == ch
    ts = _pick(s, (128, 64, 32))
    ns = s // ts
    assert ch % LANES == 0 and ts >= CONV_HALO >= width - 1
    kern = functools.partial(_conv_seq_kernel, ts=ts, width=width)
    vec = pl.BlockSpec((1, ch), lambda b, i: (0, 0))
    return pl.pallas_call(
        kern,
        grid=(bsz, ns),
        in_specs=[pl.BlockSpec((None, ts, ch), lambda b, i: (0, b * ns + i, 0)),
                  pl.BlockSpec((None, ts, ch), lambda b, i: (1, b * ns + i, 0)),
                  pl.BlockSpec((None, width - 1, ch), lambda b, i: (b, 0, 0)),
                  pl.BlockSpec((width, ch), lambda b, i: (0, 0)),
                  vec, vec, vec],
        out_specs=[pl.BlockSpec((ts, ch), lambda b, i: (b * ns + i, 0)),
                   pl.BlockSpec((None, width - 1, ch), lambda b, i: (b, 0, 0))],
        out_shape=[jax.ShapeDtypeStruct((bsz * s, ch), BF16),
                   jax.ShapeDtypeStruct((bsz, width - 1, ch), F32)],
        scratch_shapes=[pltpu.VMEM((ch // LANES, CONV_HALO + ts, LANES), F32),
                        pltpu.VMEM((ch // LANES, ts, LANES), F32)],
        compiler_params=_params(("arbitrary", "arbitrary")),
        name="conv_seq",
    )(proj, proj, state, w, cb, lw, lb)


def _conv_step_kernel(av_ref, ag_ref, st_ref, w_ref, cb_ref, lw_ref, lb_ref, o_ref, ns_ref, f_ref, y_ref,
                      *, nb, s, width, cc):
    nst = width - 1
    ch = f_ref.shape[1]

    def body(q, carry):
        r = pl.multiple_of(q * s, s)
        f_ref[0:nst, :] = st_ref[q]
        f_ref[nst:nst + s, :] = av_ref[pl.ds(r, s), :] * _sigmoid(ag_ref[pl.ds(r, s), :])
        for c0 in range(0, ch, cc):
            acc = _conv_taps(f_ref, w_ref, 0, s, c0, cc, width)
            y_ref[pl.ds(r, s), c0:c0 + cc] = acc + cb_ref[:, c0:c0 + cc]
        ns_ref[q] = f_ref[s:s + nst, :]
        return carry

    lax.fori_loop(0, nb, body, 0)
    o_ref[...] = _ln_swish(y_ref[...], lw_ref[...], lb_ref[...]).astype(o_ref.dtype)


def _conv_step(proj, state, w, cb, lw, lb, bsz, s):
    width, ch = w.shape
    assert proj.shape[2] == ch
    assert s % SUBLANES == 0
    nb = _pick(bsz, (16, 8, 4, 2, 1))
    cc = _pick(ch, (512, 256, 128))
    kern = functools.partial(_conv_step_kernel, nb=nb, s=s, width=width, cc=cc)
    vec = pl.BlockSpec((1, ch), lambda b: (0, 0))
    frows = -(-(width - 1 + s) // SUBLANES) * SUBLANES
    return pl.pallas_call(
        kern,
        grid=(bsz // nb,),
        in_specs=[pl.BlockSpec((None, nb * s, ch), lambda b: (0, b, 0)),
                  pl.BlockSpec((None, nb * s, ch), lambda b: (1, b, 0)),
                  pl.BlockSpec((nb, width - 1, ch), lambda b: (b, 0, 0)),
                  pl.BlockSpec((width, ch), lambda b: (0, 0)),
                  vec, vec, vec],
        out_specs=[pl.BlockSpec((nb * s, ch), lambda b: (b, 0)),
                   pl.BlockSpec((nb, width - 1, ch), lambda b: (b, 0, 0))],
        out_shape=[jax.ShapeDtypeStruct((bsz * s, ch), BF16),
                   jax.ShapeDtypeStruct((bsz, width - 1, ch), F32)],
        scratch_shapes=[pltpu.VMEM((frows, ch), F32), pltpu.VMEM((nb * s, ch), F32)],
        compiler_params=_params(("parallel",)),
        name="conv_step",
    )(proj, proj, state, w, cb, lw, lb)


def _mlstm_kernel(qk_ref, v_ref, o_ref, g_ref, gt_ref, c0_ref, n0_ref, m0_ref, nw_ref,
                  hm_ref, c_out, n_out, m_out, c_s, n_s, m_s, *, n_heads, scale):
    c = pl.program_id(1)
    last = pl.num_programs(1) - 1
    ln = qk_ref.shape[0]
    wqk = qk_ref.shape[1] // 2
    dqk = wqk // n_heads
    dv = v_ref.shape[1] // n_heads

    @pl.when(c == 0)
    def _():
        c_s[...] = c0_ref[...]
        n_s[...] = n0_ref[...]
        m_s[...] = m0_ref[...]

    row = lax.broadcasted_iota(I32, (ln, ln), 0)
    col = lax.broadcasted_iota(I32, (ln, ln), 1)
    tri = row >= col
    tri_t = row <= col
    g = g_ref[...]
    gt = gt_ref[...]

    for hd in range(n_heads):
        li_c = g[:, hd:hd + 1]
        lf_c = g[:, n_heads + hd:n_heads + hd + 1]
        li_r = gt[hd:hd + 1, :]
        lf_r = gt[n_heads + hd:n_heads + hd + 1, :]
        b_c = jnp.sum(jnp.where(tri, lf_r, 0.0), axis=1, keepdims=True)
        b_r = jnp.sum(jnp.where(tri_t, lf_c, 0.0), axis=0, keepdims=True)
        b_l = jnp.sum(lf_r, axis=1, keepdims=True)
        m_prev = m_s[:, hd:hd + 1]

        dmat = jnp.where(tri, b_c - b_r + li_r, NEG_BIG)
        inter = b_c + m_prev
        m_t = jnp.maximum(inter, jnp.max(dmat, axis=1, keepdims=True))
        a = jnp.exp(inter - m_t)

        q = qk_ref[:, hd * dqk:(hd + 1) * dqk]
        k = qk_ref[:, wqk + hd * dqk:wqk + (hd + 1) * dqk] * scale
        qb = q.astype(BF16)
        vb = v_ref[:, hd * dv:(hd + 1) * dv].astype(BF16)
        cst = c_s[hd]
        nst = n_s[hd:hd + 1, :]
        s = lax.dot_general(qb, k.astype(BF16), (((1,), (1,)), ((), ())), preferred_element_type=F32)
        s = s * jnp.exp(dmat - m_t)
        num = a * jnp.dot(qb, cst.astype(BF16), preferred_element_type=F32) \
            + jnp.dot(s.astype(BF16), vb, preferred_element_type=F32)
        den = a * jnp.sum(q * nst, axis=1, keepdims=True) + jnp.sum(s, axis=1, keepdims=True)
        h = num / jnp.maximum(jnp.abs(den), jnp.exp(-m_t))
        hn = h * lax.rsqrt(jnp.mean(h * h, -1, keepdims=True) + EPS) * nw_ref[hd:hd + 1, :]
        hm_ref[:, hd * dv:(hd + 1) * dv] = (hn * _sigmoid(o_ref[:, hd * dv:(hd + 1) * dv])).astype(hm_ref.dtype)

        g_r = b_l - b_r + li_r
        g_c = b_l - b_c + li_c
        m_new = jnp.maximum(b_l + m_prev, jnp.max(g_r, axis=1, keepdims=True))
        decay = jnp.exp(b_l + m_prev - m_new)
        kw = k * jnp.exp(g_c - m_new)
        c_new = decay * cst + lax.dot_general(kw.astype(BF16), vb, (((0,), (0,)), ((), ())),
                                              preferred_element_type=F32)
        n_new = decay * nst + jnp.sum(kw, axis=0, keepdims=True)
        c_s[hd] = c_new
        n_s[hd:hd + 1, :] = n_new
        m_s[:, hd:hd + 1] = m_new

    @pl.when(c == last)
    def _():
        c_out[...] = c_s[...]
        n_out[...] = n_s[...]
        m_out[...] = m_s[...]


def _mlstm(proj, g, c0, n0, m0, norm_w, bsz, s, slab0):
    _, n_heads, dqk, dv = c0.shape
    ln = s if s <= MLSTM_CHUNK else MLSTM_CHUNK
    assert s % ln == 0 and ln % SUBLANES == 0
    nc = s // ln
    wqk, wv = n_heads * dqk, n_heads * dv
    slab = proj.shape[2]
    assert 2 * wqk == slab and wv == slab
    gt3 = g[:, :2 * n_heads].reshape(bsz * nc, ln, 2 * n_heads).transpose(0, 2, 1)
    kern = functools.partial(_mlstm_kernel, n_heads=n_heads, scale=dqk ** -0.5)
    rows = lambda col: (lambda b, c: (b * nc + c, col))
    slab_rows = lambda k: pl.BlockSpec((None, ln, slab), lambda b, c: (slab0 + k, b * nc + c, 0))
    per_seq = lambda shape: pl.BlockSpec((None,) + shape, lambda b, c: (b,) + (0,) * len(shape))
    out = pl.pallas_call(
        kern,
        grid=(bsz, nc),
        in_specs=[slab_rows(0), slab_rows(1), slab_rows(2),
                  pl.BlockSpec((ln, LANES), rows(0)),
                  pl.BlockSpec((None, 2 * n_heads, ln), lambda b, c: (b * nc + c, 0, 0)),
                  per_seq((n_heads, dqk, dv)), per_seq((n_heads, dqk)), per_seq((1, n_heads)),
                  pl.BlockSpec((n_heads, dv), lambda b, c: (0, 0))],
        out_specs=[pl.BlockSpec((ln, wv), rows(0)),
                   per_seq((n_heads, dqk, dv)), per_seq((n_heads, dqk)), per_seq((1, n_heads))],
        out_shape=[jax.ShapeDtypeStruct((bsz * s, wv), BF16),
                   jax.ShapeDtypeStruct((bsz, n_heads, dqk, dv), F32),
                   jax.ShapeDtypeStruct((bsz, n_heads, dqk), F32),
                   jax.ShapeDtypeStruct((bsz, 1, n_heads), F32)],
        scratch_shapes=[pltpu.VMEM((n_heads, dqk, dv), F32), pltpu.VMEM((n_heads, dqk), F32),
                        pltpu.VMEM((1, n_heads), F32)],
        compiler_params=_params(("arbitrary", "arbitrary")),
        name="mlstm",
    )(proj, proj, proj, g, gt3, c0, n0, m0.reshape(bsz, 1, n_heads), norm_w)
    hm, c_new, n_new, m_new = out
    return hm, c_new, n_new, m_new.reshape(bsz, n_heads)


def _outproj_kernel(co_ref, hm_ref, w_ref, o_ref):
    kc = co_ref.shape[1]
    o_ref[...] = jnp.dot(co_ref[...], w_ref[0:kc, :], preferred_element_type=F32) \
        + jnp.dot(hm_ref[...], w_ref[kc:2 * kc, :], preferred_element_type=F32)


def _outproj(co, hm, w_bf):
    m, kc = co.shape
    d = w_bf.shape[-1]
    assert hm.shape[1] == kc and w_bf.shape[0] == 2 * kc
    tn = _pick(d, (2048, 1024, 512, 256, 128))
    tm = _pick(m, (512, 256, 128, 64, 32, 16, 8))
    return pl.pallas_call(
        _outproj_kernel,
        grid=(d // tn, m // tm),
        in_specs=[pl.BlockSpec((tm, kc), lambda j, i: (i, 0)),
                  pl.BlockSpec((tm, kc), lambda j, i: (i, 0)),
                  pl.BlockSpec((2 * kc, tn), lambda j, i: (0, j))],
        out_specs=pl.BlockSpec((tm, tn), lambda j, i: (i, j)),
        out_shape=jax.ShapeDtypeStruct((m, d), F32),
        compiler_params=_params(("parallel", "parallel")),
        name="outproj",
    )(co, hm, w_bf)


def _router_kernel(mix_ref, x_ref, g1_ref, sc2_ref, sh2_ref, n1_ref, n2_ref, wr_ref, br_ref,
                   x1_ref, h2_ref, ti_ref, tg_ref, *, st):
    mix = mix_ref[...]
    mn = mix * lax.rsqrt(jnp.mean(mix * mix, -1, keepdims=True) + EPS) * n1_ref[...]
    x1 = x_ref[...] + _mod_rows(g1_ref, st) * mn
    x1_ref[...] = x1
    y2 = x1 * lax.rsqrt(jnp.mean(x1 * x1, -1, keepdims=True) + EPS) * n2_ref[...]
    h2 = y2 * (1.0 + _mod_rows(sc2_ref, st)) + _mod_rows(sh2_ref, st)
    half = h2.shape[1] // 2
    h2_ref[...] = _pack_pair(h2[:, :half], h2[:, half:])
    logits = jnp.dot(h2.astype(BF16), wr_ref[...], preferred_element_type=F32) + br_ref[...]
    n_exp = logits.shape[1]
    lane = lax.broadcasted_iota(I32, logits.shape, 1)
    lane_o = lax.broadcasted_iota(I32, ti_ref.shape, 1)
    idx_out = jnp.zeros(ti_ref.shape, I32)
    val_out = jnp.zeros(tg_ref.shape, F32)
    top = None
    den = jnp.zeros((logits.shape[0], 1), F32)
    for r in range(TOP_K):
        mx = jnp.max(logits, axis=1, keepdims=True)
        ix = jnp.min(jnp.where(logits == mx, lane, n_exp), axis=1, keepdims=True)
        if top is None:
            top = mx
        e = jnp.exp(mx - top)
        den = den + e
        idx_out = jnp.where(lane_o == r, ix, idx_out)
        val_out = jnp.where(lane_o == r, e, val_out)
        logits = jnp.where(lane == ix, NEG_BIG, logits)
    ti_ref[...] = idx_out
    tg_ref[...] = val_out / den


def _router(mix, x2, mod3, grp, n1, n2, wr, br):
    rows, d = x2.shape
    n_exp = wr.shape[1]
    kern = functools.partial(_router_kernel, st=grp.st)
    return pl.pallas_call(
        kern,
        grid=grp.grid,
        in_specs=[grp.rows(d), grp.rows(d), grp.mod(d, 2), grp.mod(d, 4), grp.mod(d, 3),
                  grp.const((1, d)), grp.const((1, d)), grp.const((d, n_exp)), grp.const((1, n_exp))],
        out_specs=[grp.rows(d), grp.rows(d // 2), grp.rows(LANES), grp.rows(LANES)],
        out_shape=[jax.ShapeDtypeStruct((rows, d), F32), jax.ShapeDtypeStruct((rows, d // 2), U32),
                   jax.ShapeDtypeStruct((rows, LANES), I32), jax.ShapeDtypeStruct((rows, LANES), F32)],
        compiler_params=_params(("parallel", "parallel")),
        name="router",
    )(mix, x2, mod3, mod3, mod3, n1, n2, wr, br)


def _rank_kernel(ti_ref, ps_ref, o_ref, carry_ref):
    @pl.when(pl.program_id(0) == 0)
    def _():
        carry_ref[...] = jnp.zeros(carry_ref.shape, F32)

    ti = ti_ref[...].astype(F32)
    tt = ti.shape[0]
    lane = lax.broadcasted_iota(I32, ti.shape, 1)
    lane_f = lane.astype(F32)
    cols = []
    member = jnp.zeros(ti.shape, F32)
    for kk in range(TOP_K):
        ek = jnp.sum(jnp.where(lane == kk, ti, 0.0), axis=1, keepdims=True)
        cols.append(ek)
        member = member + jnp.where(lane_f == ek, 1.0, 0.0)
    r = lax.broadcasted_iota(I32, (tt, tt), 0)
    c = lax.broadcasted_iota(I32, (tt, tt), 1)
    before = jnp.where(r > c, 1.0, 0.0).astype(BF16)
    base = jnp.dot(before, member.astype(BF16), preferred_element_type=F32) + carry_ref[...] + ps_ref[...]
    out = jnp.zeros(o_ref.shape, I32)
    for kk in range(TOP_K):
        dk = jnp.sum(jnp.where(lane_f == cols[kk], base, 0.0), axis=1, keepdims=True)
        out = jnp.where(lane == kk, dk.astype(I32), out)
    o_ref[...] = out
    carry_ref[...] += jnp.sum(member, axis=0, keepdims=True)


def _rank(top_i, pad_start):
    t = top_i.shape[0]
    tt = _pick(t, (512, 256, 128, 64, 32, 16, 8))
    return pl.pallas_call(
        _rank_kernel,
        grid=(t // tt,),
        in_specs=[pl.BlockSpec((tt, LANES), lambda i: (i, 0)), pl.BlockSpec((1, LANES), lambda i: (0, 0))],
        out_specs=pl.BlockSpec((tt, LANES), lambda i: (i, 0)),
        out_shape=jax.ShapeDtypeStruct((t, LANES), I32),
        scratch_shapes=[pltpu.VMEM((1, LANES), F32)],
        compiler_params=_params(("arbitrary",)),
        name="moe_rank",
    )(top_i, pad_start)


def _route(top_i, n_experts, tm):
    t = top_i.shape[0]
    a = t * TOP_K
    flat_e = top_i[:, :TOP_K].reshape(a)
    counts = jnp.sum((flat_e[:, None] == jnp.arange(n_experts, dtype=I32)[None, :]).astype(I32), axis=0)
    padded = (counts + tm - 1) // tm * tm
    pad_end = jnp.cumsum(padded)
    pad_start = pad_end - padded
    nt = a // tm + n_experts
    tile_start = jnp.arange(nt, dtype=I32) * tm
    tile_u = (tile_start < pad_end[-1]).astype(I32)
    n_used = jnp.sum(tile_u)
    tile_e = jnp.sum((pad_end[None, :] <= tile_start[:, None]).astype(I32), axis=1)
    tile_e = jnp.minimum(tile_e, n_experts - 1)
    tile_e = jnp.where(tile_u == 1, tile_e, tile_e[jnp.maximum(n_used - 1, 0)])
    idx = jnp.arange(nt, dtype=I32)
    starts = jnp.logical_and(tile_u == 1, jnp.logical_or(idx == 0, tile_e != jnp.roll(tile_e, 1)))
    seg = (jnp.cumsum(starts.astype(I32)) - 1).astype(I32)
    later = lax.cummin(jnp.where(starts, idx, nt)[::-1])[::-1]
    nxt = jnp.concatenate([later[1:], jnp.full((1,), nt, I32)])
    next_e = tile_e[jnp.where(nxt >= nt, 0, nxt)]
    n_seg = jnp.sum(starts.astype(I32)).reshape(1)
    tiles = (tile_e, tile_u, seg, next_e, n_seg)
    ps = jnp.zeros((1, LANES), F32).at[0, :n_experts].set(pad_start.astype(F32))
    dest = _rank(top_i, ps)
    return dest, tiles, nt


def _row_copy(src_ref, dst_ref, sem, src_row, dst_row):
    return pltpu.make_async_copy(src_ref.at[pl.ds(src_row, 1), :], dst_ref.at[pl.ds(dst_row, 1), :], sem)


def _dispatch_kernel(dest_ref, src_ref, xs_in, xs_out, sem):
    del xs_in
    tt = src_ref.shape[0]

    def start(r, carry):
        for kk in range(TOP_K):
            _row_copy(src_ref, xs_out, sem, r, dest_ref[0, r * TOP_K + kk]).start()
        return carry

    lax.fori_loop(0, tt, start, 0, unroll=ISSUE_UNROLL)
    for _ in range(TOP_K):
        pltpu.make_async_copy(src_ref, src_ref, sem).wait()


def _dispatch(h2p, dest, xs):
    t, w = h2p.shape
    tt = _pick(t, (256, 128, 64, 32, 16, 8))
    nt = t // tt
    return pl.pallas_call(
        _dispatch_kernel,
        grid=(nt,),
        in_specs=[pl.BlockSpec((None, 1, tt * TOP_K), lambda i: (i, 0, 0), memory_space=pltpu.SMEM),
                  pl.BlockSpec((tt, w), lambda i: (i, 0)),
                  pl.BlockSpec(memory_space=pl.ANY)],
        out_specs=pl.BlockSpec(memory_space=pl.ANY),
        out_shape=jax.ShapeDtypeStruct(xs.shape, xs.dtype),
        scratch_shapes=[pltpu.SemaphoreType.DMA(())],
        input_output_aliases={2: 0},
        compiler_params=_params(("arbitrary",)),
        name="moe_dispatch",
    )(dest[:, :TOP_K].reshape(nt, 1, tt * TOP_K), h2p, xs)


N_TILE_TABLES = 5


def _segment_weights(tables, w_hbm, wst_ref, sem, *, l, tn, col_offs):
    te_ref, tu_ref, sg_ref, ne_ref, ns_ref = tables
    j = pl.program_id(0)
    i = pl.program_id(1)
    n_seg = ns_ref[0]
    first = jnp.logical_and(tu_ref[i] == 1,
                            jnp.logical_or(i == 0, te_ref[i] != te_ref[jnp.maximum(i - 1, 0)]))
    g = j * n_seg + sg_ref[i]
    slot = lax.rem(g, 2)

    def copies(e, jj, sl):
        return [pltpu.make_async_copy(
            w_hbm.at[l, e, :, pl.ds(pl.multiple_of((off + jj) * tn, tn), tn)], wst_ref.at[sl, m], sem.at[sl])
            for m, off in enumerate(col_offs)]

    @pl.when(jnp.logical_and(j == 0, i == 0))
    def _():
        for cp in copies(te_ref[0], 0, 0):
            cp.start()

    @pl.when(first)
    def _():
        for cp in copies(te_ref[i], j, slot):
            cp.wait()

        @pl.when(g + 1 < pl.num_programs(0) * n_seg)
        def _():
            jn = jnp.where(sg_ref[i] + 1 == n_seg, j + 1, j)
            for cp in copies(ne_ref[i], jn, 1 - slot):
                cp.start()

    return first, slot


def _gmm1_kernel(*refs, l, tn, nj):
    tables = refs[:N_TILE_TABLES]
    x_ref, w_hbm, bg_ref, bu_ref, o_ref, wst_ref, wgb_ref, wub_ref, sem = refs[N_TILE_TABLES:]
    used = tables[1][pl.program_id(1)] == 1
    first, slot = _segment_weights(tables, w_hbm, wst_ref, sem, l=l, tn=tn, col_offs=(0, nj))
    half = x_ref.shape[1]

    def tile(wg_lo, wg_hi, wu_lo, wu_hi):
        xlo, xhi = _unpack_pair(x_ref[...])
        gt = jnp.dot(xlo, wg_lo, preferred_element_type=F32) \
            + jnp.dot(xhi, wg_hi, preferred_element_type=F32) + bg_ref[...]
        up = jnp.dot(xlo, wu_lo, preferred_element_type=F32) \
            + jnp.dot(xhi, wu_hi, preferred_element_type=F32) + bu_ref[...]
        gt = jnp.minimum(gt, SWIGLU_LIMIT)
        up = jnp.clip(up, -SWIGLU_LIMIT, SWIGLU_LIMIT)
        act = (up + 1.0) * gt * _sigmoid(SWIGLU_ALPHA * gt)
        o_ref[...] = act.astype(o_ref.dtype)

    @pl.when(first)
    def _():
        ws = []
        for m, dst in enumerate((wgb_ref, wub_ref)):
            for k0 in (0, half):
                wk = wst_ref[slot, m, k0:k0 + half, :].astype(BF16)
                dst[k0:k0 + half, :] = wk
                ws.append(wk)
        tile(*ws)

    @pl.when(jnp.logical_and(used, jnp.logical_not(first)))
    def _():
        tile(wgb_ref[0:half, :], wgb_ref[half:2 * half, :], wub_ref[0:half, :], wub_ref[half:2 * half, :])

    @pl.when(jnp.logical_not(used))
    def _():
        o_ref[...] = jnp.zeros(o_ref.shape, o_ref.dtype)


def _gmm1(xs, w1, b1, tiles, l, tm):
    rows, half = xs.shape
    d = 2 * half
    dff = w1.shape[-1] // 2
    nt = rows // tm
    tn = _pick(dff, (512, 256, 128))
    nj = dff // tn
    b1r = b1.reshape(b1.shape[0], b1.shape[1], 1, 2 * dff)
    kern = functools.partial(_gmm1_kernel, l=l, tn=tn, nj=nj)
    grid_spec = pltpu.PrefetchScalarGridSpec(
        num_scalar_prefetch=N_TILE_TABLES,
        grid=(nj, nt),
        in_specs=[pl.BlockSpec((tm, half), lambda j, i, *t: (i, 0)),
                  pl.BlockSpec(memory_space=pl.ANY),
                  pl.BlockSpec((None, None, 1, tn), lambda j, i, te, *t: (l, te[i], 0, j)),
                  pl.BlockSpec((None, None, 1, tn), lambda j, i, te, *t: (l, te[i], 0, nj + j))],
        out_specs=pl.BlockSpec((tm, tn), lambda j, i, *t: (i, j)),
        scratch_shapes=[pltpu.VMEM((2, 2, d, tn), F32), pltpu.VMEM((d, tn), BF16), pltpu.VMEM((d, tn), BF16),
                        pltpu.SemaphoreType.DMA((2,))])
    return pl.pallas_call(
        kern,
        grid_spec=grid_spec,
        out_shape=jax.ShapeDtypeStruct((rows, dff), BF16),
        compiler_params=_params(("arbitrary", "arbitrary")),
        name="moe_gmm1",
    )(*tiles, xs, w1, b1r, b1r)


def _gmm2_kernel(*refs, l, tn):
    tables = refs[:N_TILE_TABLES]
    a_ref, w_hbm, b_ref, o_ref, wst_ref, wb_ref, sem = refs[N_TILE_TABLES:]
    used = tables[1][pl.program_id(1)] == 1
    first, slot = _segment_weights(tables, w_hbm, wst_ref, sem, l=l, tn=tn, col_offs=(0,))

    def tile(w):
        y = jnp.dot(a_ref[...], w, preferred_element_type=F32) + b_ref[...]
        o_ref[...] = _pack_pair(y[:, :tn // 2], y[:, tn // 2:])

    @pl.when(first)
    def _():
        w = wst_ref[slot, 0].astype(BF16)
        wb_ref[...] = w
        tile(w)

    @pl.when(jnp.logical_and(used, jnp.logical_not(first)))
    def _():
        tile(wb_ref[...])

    @pl.when(jnp.logical_not(used))
    def _():
        o_ref[...] = jnp.zeros(o_ref.shape, o_ref.dtype)


def _gmm2(act, w2, b2, tiles, l, tm):
    rows, dff = act.shape
    d = w2.shape[-1]
    nt = rows // tm
    tn = _pick(d, (2048, 1024, 512, 256, 128))
    b2r = b2.reshape(b2.shape[0], b2.shape[1], 1, d)
    kern = functools.partial(_gmm2_kernel, l=l, tn=tn)
    grid_spec = pltpu.PrefetchScalarGridSpec(
        num_scalar_prefetch=N_TILE_TABLES,
        grid=(d // tn, nt),
        in_specs=[pl.BlockSpec((tm, dff), lambda j, i, *t: (i, 0)),
                  pl.BlockSpec(memory_space=pl.ANY),
                  pl.BlockSpec((None, None, 1, tn), lambda j, i, te, *t: (l, te[i], 0, j))],
        out_specs=pl.BlockSpec((tm, tn // 2), lambda j, i, *t: (i, j)),
        scratch_shapes=[pltpu.VMEM((2, 1, dff, tn), F32), pltpu.VMEM((dff, tn), BF16),
                        pltpu.SemaphoreType.DMA((2,))])
    y = pl.pallas_call(
        kern,
        grid_spec=grid_spec,
        out_shape=jax.ShapeDtypeStruct((rows, d // 2), U32),
        compiler_params=_params(("arbitrary", "arbitrary")),
        name="moe_gmm2",
    )(*tiles, act, w2, b2r)
    return y, tn // 2


def _combine_kernel(pos_ref, posn_ref, y_hbm, tg_ref, x1_ref, g2_ref, nw_ref, o_ref, buf_ref, sem, *, st, pw):
    ni = pl.num_programs(1)
    n = pl.program_id(0) * ni + pl.program_id(1)
    total = pl.num_programs(0) * ni
    tt, d = o_ref.shape
    slot = lax.rem(n, 2)

    def start_rows(idx_ref, sl):
        def body(r, carry):
            for kk in range(TOP_K):
                _row_copy(y_hbm, buf_ref.at[sl, kk], sem.at[sl], idx_ref[0, r * TOP_K + kk], r).start()
            return carry
        lax.fori_loop(0, tt, body, 0, unroll=ISSUE_UNROLL)

    @pl.when(n == 0)
    def _():
        start_rows(pos_ref, 0)

    @pl.when(n + 1 < total)
    def _():
        start_rows(posn_ref, 1 - slot)

    pltpu.make_async_copy(buf_ref.at[slot], buf_ref.at[slot], sem.at[slot]).wait()

    gates = tg_ref[...]
    lane = lax.broadcasted_iota(I32, gates.shape, 1)
    gk = [jnp.sum(jnp.where(lane == kk, gates, 0.0), axis=1, keepdims=True) for kk in range(TOP_K)]
    ssq = jnp.zeros((tt, 1), F32)
    for blk in range(d // (2 * pw)):
        lo = jnp.zeros((tt, pw), F32)
        hi = jnp.zeros((tt, pw), F32)
        for kk in range(TOP_K):
            p = buf_ref[slot, kk, :, blk * pw:(blk + 1) * pw]
            lo = lo + gk[kk] * lax.bitcast_convert_type(p << 16, F32)
            hi = hi + gk[kk] * lax.bitcast_convert_type(p & jnp.uint32(HI_HALF), F32)
        ssq = ssq + jnp.sum(lo * lo, -1, keepdims=True) + jnp.sum(hi * hi, -1, keepdims=True)
        o_ref[:, 2 * pw * blk:2 * pw * blk + pw] = lo
        o_ref[:, 2 * pw * blk + pw:2 * pw * (blk + 1)] = hi
    fn = o_ref[...] * lax.rsqrt(ssq / d + EPS) * nw_ref[...]
    o_ref[...] = x1_ref[...] + _mod_rows(g2_ref, st) * fn


def _combine(y, pw, dest, tg, x1, mod3, grp, nw):
    rows, d = x1.shape
    tt = grp.ts
    nt = rows // tt
    ni = grp.n_inner
    kern = functools.partial(_combine_kernel, st=grp.st, pw=pw)
    dest3 = dest[:, :TOP_K].reshape(nt, 1, tt * TOP_K)
    idx_spec = lambda ahead: pl.BlockSpec(
        (None, 1, tt * TOP_K), lambda o, i: (jnp.minimum(o * ni + i + ahead, nt - 1), 0, 0),
        memory_space=pltpu.SMEM)
    return pl.pallas_call(
        kern,
        grid=grp.grid,
        in_specs=[idx_spec(0), idx_spec(1),
                  pl.BlockSpec(memory_space=pl.ANY),
                  grp.rows(LANES), grp.rows(d), grp.mod(d, 5), grp.const((1, d))],
        out_specs=grp.rows(d),
        out_shape=jax.ShapeDtypeStruct((rows, d), F32),
        scratch_shapes=[pltpu.VMEM((2, TOP_K, tt, d // 2), U32), pltpu.SemaphoreType.DMA((2,))],
        compiler_params=_params(("arbitrary", "arbitrary")),
        name="moe_combine",
    )(dest3, dest3, y, tg, x1, mod3, nw)


def _mixer(x2, mod3, grp, conv_buf, c0, n0, m0, p, l):
    n_heads = c0.shape[1]
    ch = p['conv_w'].shape[-1]
    h, g = _prenorm(x2, mod3, grp, p['norm1_pre'], p['w_gate'], p['b_gate'], n_heads)
    proj = _inproj(h, p['w_in'], ch)
    conv = _conv_seq if grp.nbq == 1 else _conv_step
    co, new_buf = conv(proj, conv_buf, p['conv_w'], p['conv_b'], p['conv_ln_w'], p['conv_ln_b'], grp.bsz, grp.s)
    hm, c_new, n_new, m_new = _mlstm(proj, g, c0, n0, m0, p['mlstm_norm_w'], grp.bsz, grp.s, 2)
    mix = _outproj(co, hm, p['w_out'])
    x1, h2p, ti, tg = _router(mix, x2, mod3, grp, p['norm1_post'], p['norm2_pre'], p['w_router'], p['b_router'])
    return dict(x1=x1, h2p=h2p, ti=ti, tg=tg, state=(new_buf, c_new, n_new, m_new))


def kernel(x_prompt, x_sample, c_prompt, c_sample, state_conv, state_mlstm_C, state_mlstm_n, state_mlstm_m,
           w_ada, b_ada, norm1_pre, w_in, b_gates, conv_w, conv_b, conv_ln_w, conv_ln_b, mlstm_norm_w,
           w_out, norm1_post, norm2_pre, w_router, b_router, w1, b1, w2, b2, norm2_post):
    depth = w_ada.shape[0]
    bp, sp, d = x_prompt.shape
    bs, ss, _ = x_sample.shape
    n_heads, dqk, dv = state_mlstm_C.shape[2:]
    n_experts = w_router.shape[-1]
    nst, ch = state_conv.shape[2:]
    n_gate = 2 * n_heads
    assert n_gate <= LANES and n_experts <= LANES
    tp, tsmp = bp * sp, bs * ss

    mp = -(-(bp + bs) // SUBLANES) * SUBLANES
    c_all = jnp.zeros((mp, d), F32).at[:bs].set(c_sample).at[bs:bs + bp].set(c_prompt)
    grp_p = _Group(bp, sp, bs, ROW_TILE)
    grp_s = _Group(bs, ss, 0, ROW_TILE)

    xp, xs = x_prompt.reshape(tp, d), x_sample.reshape(tsmp, d)
    outs = [[] for _ in range(8)]
    for l in range(depth):
        row = lambda v: v[l].reshape(1, -1)
        p = dict(
            w_in=w_in[l, :, :w_in.shape[-1] - n_gate].astype(BF16), w_out=w_out[l].astype(BF16),
            w_gate=jnp.pad(w_in[l, :, w_in.shape[-1] - n_gate:], ((0, 0), (0, LANES - n_gate))).astype(BF16),
            b_gate=jnp.pad(b_gates[l], (0, LANES - n_gate)).reshape(1, LANES),
            norm1_pre=row(norm1_pre), conv_w=conv_w[l], conv_b=row(conv_b), conv_ln_w=row(conv_ln_w),
            conv_ln_b=row(conv_ln_b), mlstm_norm_w=mlstm_norm_w[l], norm1_post=row(norm1_post),
            norm2_pre=row(norm2_pre), w_router=w_router[l].astype(BF16), b_router=row(b_router))
        mod3 = _ada(c_all, w_ada, b_ada, l).reshape(mp, 1, 6 * d)
        zero = lambda shape: jnp.zeros(shape, F32)
        gp = _mixer(xp, mod3, grp_p, zero((bp, nst, ch)), zero((bp, n_heads, dqk, dv)),
                    zero((bp, n_heads, dqk)), zero((bp, n_heads)), p, l)
        gs = _mixer(xs, mod3, grp_s, state_conv[l], state_mlstm_C[l], state_mlstm_n[l],
                    state_mlstm_m[l], p, l)

        top_i = jnp.concatenate([gp['ti'], gs['ti']], axis=0)
        dest, tiles, nt = _route(top_i, n_experts, MOE_ROWS)
        xsorted = jnp.zeros((nt * MOE_ROWS, d // 2), U32)
        xsorted = _dispatch(gp['h2p'], dest[:tp], xsorted)
        xsorted = _dispatch(gs['h2p'], dest[tp:], xsorted)
        act = _gmm1(xsorted, w1, b1, tiles, l, MOE_ROWS)
        y, pw = _gmm2(act, w2, b2, tiles, l, MOE_ROWS)
        nw2 = row(norm2_post)
        xp = _combine(y, pw, dest[:tp], gp['tg'], gp['x1'], mod3, grp_p, nw2)
        xs = _combine(y, pw, dest[tp:], gs['tg'], gs['x1'], mod3, grp_s, nw2)
        for o, v in zip(outs, gp['state'] + gs['state']):
            o.append(v)
    stack = (lambda o: o[0][None]) if depth == 1 else jnp.stack
    return (xp.reshape(bp, sp, d), xs.reshape(bs, ss, d)) + tuple(stack(o) for o in outs)
```

```python
import functools

import jax
import jax.numpy as jnp
from jax import lax
from jax.experimental import pallas as pl
from jax.experimental.pallas import tpu as pltpu

F32 = jnp.float32
BF16 = jnp.bfloat16
I32 = jnp.int32
U32 = jnp.uint32

EPS = 1e-6
GATE_CAP = 15.0
TOP_K = 4
SWIGLU_LIMIT = 7.0
SWIGLU_ALPHA = 1.702
NEG_BIG = -1e30

LANES = 128
SUBLANES = 8
VMEM_LIMIT = 56 * 1024 * 1024
GMM_VMEM_LIMIT = 60 * 1024 * 1024
MLSTM_CHUNK = 256
MOE_ROWS = 256
ROW_TILE = 256
HI_HALF = 0xFFFF0000
ISSUE_UNROLL = 4


def _params(sem, vmem=VMEM_LIMIT):
    return pltpu.CompilerParams(dimension_semantics=sem, vmem_limit_bytes=vmem)


def _sigmoid(x):
    return 1.0 / (1.0 + jnp.exp(-x))


def _pick(n, prefs):
    for p in prefs:
        if n % p == 0:
            return p
    return n


def _pack_pair(lo, hi):
    lo_b = lax.bitcast_convert_type(lo.astype(BF16).astype(F32), U32) >> 16
    hi_b = lax.bitcast_convert_type(hi.astype(BF16).astype(F32), U32) & jnp.uint32(HI_HALF)
    return hi_b | lo_b


def _unpack_pair(p):
    lo = lax.bitcast_convert_type(p << 16, F32).astype(BF16)
    hi = lax.bitcast_convert_type(p & jnp.uint32(HI_HALF), F32).astype(BF16)
    return lo, hi


class _Group:
    def __init__(self, bsz, s, mod_row0, tile_rows):
        self.bsz, self.s = bsz, s
        if s >= tile_rows:
            self.nbq, self.st = 1, _pick(s, (tile_rows, 128, 64, 32, 16, 8))
            self.n_outer, self.n_inner = bsz, s // self.st
        else:
            self.nbq, self.st = _pick(bsz, (tile_rows // s, 8, 4, 2, 1)), s
            self.n_outer, self.n_inner = bsz // self.nbq, 1
        assert mod_row0 % self.nbq == 0
        self.mod_blk0 = mod_row0 // self.nbq
        self.ts = self.nbq * self.st
        self.grid = (self.n_outer, self.n_inner)

    def rows(self, width, col=0):
        ni = self.n_inner
        return pl.BlockSpec((self.ts, width), lambda o, i, *_: (o * ni + i, col))

    def mod(self, d, col):
        b0 = self.mod_blk0
        return pl.BlockSpec((self.nbq, 1, d), lambda o, i, *_: (b0 + o, 0, col))

    def const(self, shape):
        nd = len(shape)
        return pl.BlockSpec(shape, lambda o, i, *_: (0,) * nd)


def _mod_rows(m_ref, st):
    m = m_ref[...]
    nbq, _, d = m.shape
    if nbq == 1:
        return m[0]
    return jnp.broadcast_to(m, (nbq, st, d)).reshape(nbq * st, d)


def _ada_kernel(c_ref, w_ref, b_ref, o_ref):
    c = c_ref[...]
    s = (c * _sigmoid(c)).astype(BF16)
    o_ref[...] = jnp.dot(s, w_ref[...].astype(BF16), preferred_element_type=F32) + b_ref[...]


def _ada(c_all, w_ada, b_ada, l):
    mp, d = c_all.shape
    n = w_ada.shape[-1]
    tn = _pick(n, (512, 256, 128))
    return pl.pallas_call(
        _ada_kernel,
        grid=(n // tn,),
        in_specs=[pl.BlockSpec((mp, d), lambda j: (0, 0)),
                  pl.BlockSpec((None, d, tn), lambda j: (l, 0, j)),
                  pl.BlockSpec((None, 1, tn), lambda j: (l, 0, j))],
        out_specs=pl.BlockSpec((mp, tn), lambda j: (0, j)),
        out_shape=jax.ShapeDtypeStruct((mp, n), F32),
        compiler_params=_params(("parallel",)),
        name="ada",
    )(c_all, w_ada, b_ada.reshape(b_ada.shape[0], 1, n))


def _prenorm_kernel(x_ref, sc_ref, sh_ref, nw_ref, wg_ref, bg_ref, h_ref, g_ref, *, n_heads, st):
    x = x_ref[...]
    y = x * lax.rsqrt(jnp.mean(x * x, -1, keepdims=True) + EPS) * nw_ref[...]
    h = y * (1.0 + _mod_rows(sc_ref, st)) + _mod_rows(sh_ref, st)
    hb = h.astype(BF16)
    h_ref[...] = hb
    z = jnp.dot(hb, wg_ref[...].astype(BF16), preferred_element_type=F32) + bg_ref[...]
    cap = GATE_CAP * jnp.tanh(z / GATE_CAP)
    logsig = jnp.minimum(cap, 0.0) - jnp.log(1.0 + jnp.exp(-jnp.abs(cap)))
    lane = lax.broadcasted_iota(I32, z.shape, 1)
    g_ref[...] = jnp.where(lane < n_heads, cap, logsig)


def _prenorm(x2, mod3, grp, nw, wg, bg, n_heads):
    rows, d = x2.shape
    kern = functools.partial(_prenorm_kernel, n_heads=n_heads, st=grp.st)
    return pl.pallas_call(
        kern,
        grid=grp.grid,
        in_specs=[grp.rows(d), grp.mod(d, 1), grp.mod(d, 0), grp.const((1, d)),
                  grp.const((d, LANES)), grp.const((1, LANES))],
        out_specs=[grp.rows(d), grp.rows(LANES)],
        out_shape=[jax.ShapeDtypeStruct((rows, d), BF16), jax.ShapeDtypeStruct((rows, LANES), F32)],
        compiler_params=_params(("parallel", "parallel")),
        name="prenorm",
    )(x2, mod3, mod3, nw, wg, bg)


def _inproj_kernel(a_ref, w_ref, o_ref):
    o_ref[...] = jnp.dot(a_ref[...], w_ref[...], preferred_element_type=F32)


def _inproj(h, w_bf, slab):
    m, d = h.shape
    n = w_bf.shape[1]
    assert n % slab == 0
    tm = _pick(m, (512, 256, 128, 64, 32, 16, 8))
    return pl.pallas_call(
        _inproj_kernel,
        grid=(n // slab, m // tm),
        in_specs=[pl.BlockSpec((tm, d), lambda j, i: (i, 0)),
                  pl.BlockSpec((d, slab), lambda j, i: (0, j))],
        out_specs=pl.BlockSpec((None, tm, slab), lambda j, i: (j, i, 0)),
        out_shape=jax.ShapeDtypeStruct((n // slab, m, slab), F32),
        compiler_params=_params(("parallel", "parallel")),
        name="inproj",
    )(h, w_bf)


CONV_HALO = 32


def _conv_taps(f_ref, w_ref, base, rows, c0, cc, width):
    acc = jnp.zeros((rows, cc), F32)
    for ph in range(SUBLANES):
        if ph >= width:
            break
        n_al = (width - 1 - ph) // SUBLANES + 1
        gb = f_ref[base + ph: base + ph + rows + SUBLANES * (n_al - 1), c0:c0 + cc]
        for a in range(n_al):
            j = SUBLANES * a + ph
            acc = acc + w_ref[j:j + 1, c0:c0 + cc] * gb[SUBLANES * a:SUBLANES * a + rows]
    return acc


def _ln_swish(y, lw, lb):
    mu = jnp.mean(y, -1, keepdims=True)
    yc = y - mu
    yn = yc * lax.rsqrt(jnp.mean(yc * yc, -1, keepdims=True) + EPS) * lw + lb
    return yn * _sigmoid(yn)


def _conv_taps_strided(f_ref, w_ref, cb_ref, y_ref, off, ts, c, width):
    nseg = ts // SUBLANES
    lanes = slice(c * LANES, (c + 1) * LANES)
    wv = [jnp.broadcast_to(w_ref[j:j + 1, lanes], (SUBLANES, LANES)) for j in range(width)]
    acc = [None] * nseg
    for v in range(nseg + width - 1):
        yv = f_ref[c, pl.ds(v + off, SUBLANES, stride=nseg), :]
        for u in range(max(0, v - (width - 1)), min(nseg - 1, v) + 1):
            term = wv[v - u] * yv
            acc[u] = term if acc[u] is None else acc[u] + term
    cb = cb_ref[:, lanes]
    for u in range(nseg):
        y_ref[c, pl.ds(u, SUBLANES, stride=nseg), :] = acc[u] + cb


def _conv_seq_kernel(av_ref, ag_ref, st_ref, w_ref, cb_ref, lw_ref, lb_ref, o_ref, ns_ref, f_ref, y_ref,
                     *, ts, width):
    i = pl.program_id(1)
    off = CONV_HALO - (width - 1)
    nchunk = f_ref.shape[0]
    chunks = [slice(c * LANES, (c + 1) * LANES) for c in range(nchunk)]

    @pl.when(i == 0)
    def _():
        for c, lanes in enumerate(chunks):
            f_ref[c, 0:off, :] = jnp.zeros((off, LANES), F32)
            f_ref[c, off:CONV_HALO, :] = st_ref[:, lanes]

    u = av_ref[...] * _sigmoid(ag_ref[...])
    for c, lanes in enumerate(chunks):
        f_ref[c, CONV_HALO:CONV_HALO + ts, :] = u[:, lanes]
    for c in range(nchunk):
        _conv_taps_strided(f_ref, w_ref, cb_ref, y_ref, off, ts, c, width)
    y = jnp.concatenate([y_ref[c] for c in range(nchunk)], axis=1)
    o_ref[...] = _ln_swish(y, lw_ref[...], lb_ref[...]).astype(o_ref.dtype)

    @pl.when(i == pl.num_programs(1) - 1)
    def _():
        for c, lanes in enumerate(chunks):
            ns_ref[:, lanes] = f_ref[c, ts + off:ts + CONV_HALO, :]

    for c in range(nchunk):
        f_ref[c, 0:CONV_HALO, :] = f_ref[c, ts:ts + CONV_HALO, :]


def _conv_seq(proj, state, w, cb, lw, lb, bsz, s):
    width, ch = w.shape
    assert proj.shape[2] == ch
    ts = _pick(s, (128, 64, 32))
    ns = s // ts
    assert ch % LANES == 0 and ts >= CONV_HALO >= width - 1
    kern = functools.partial(_conv_seq_kernel, ts=ts, width=width)
    vec = pl.BlockSpec((1, ch), lambda b, i: (0, 0))
    return pl.pallas_call(
        kern,
        grid=(bsz, ns),
        in_specs=[pl.BlockSpec((None, ts, ch), lambda b, i: (0, b * ns + i, 0)),
                  pl.BlockSpec((None, ts, ch), lambda b, i: (1, b * ns + i, 0)),
                  pl.BlockSpec((None, width - 1, ch), lambda b, i: (b, 0, 0)),
                  pl.BlockSpec((width, ch), lambda b, i: (0, 0)),
                  vec, vec, vec],
        out_specs=[pl.BlockSpec((ts, ch), lambda b, i: (b * ns + i, 0)),
                   pl.BlockSpec((None, width - 1, ch), lambda b, i: (b, 0, 0))],
        out_shape=[jax.ShapeDtypeStruct((bsz * s, ch), BF16),
                   jax.ShapeDtypeStruct((bsz, width - 1, ch), F32)],
        scratch_shapes=[pltpu.VMEM((ch // LANES, CONV_HALO + ts, LANES), F32),
                        pltpu.VMEM((ch // LANES, ts, LANES), F32)],
        compiler_params=_params(("arbitrary", "arbitrary")),
        name="conv_seq",
    )(proj, proj, state, w, cb, lw, lb)


def _conv_step_kernel(av_ref, ag_ref, st_ref, w_ref, cb_ref, lw_ref, lb_ref, o_ref, ns_ref, f_ref, y_ref,
                      *, nb, s, width, cc):
    nst = width - 1
    ch = f_ref.shape[1]

    def body(q, carry):
        r = pl.multiple_of(q * s, s)
        f_ref[0:nst, :] = st_ref[q]
        f_ref[nst:nst + s, :] = av_ref[pl.ds(r, s), :] * _sigmoid(ag_ref[pl.ds(r, s), :])
        for c0 in range(0, ch, cc):
            acc = _conv_taps(f_ref, w_ref, 0, s, c0, cc, width)
            y_ref[pl.ds(r, s), c0:c0 + cc] = acc + cb_ref[:, c0:c0 + cc]
        ns_ref[q] = f_ref[s:s + nst, :]
        return carry

    lax.fori_loop(0, nb, body, 0)
    o_ref[...] = _ln_swish(y_ref[...], lw_ref[...], lb_ref[...]).astype(o_ref.dtype)


def _conv_step(proj, state, w, cb, lw, lb, bsz, s):
    width, ch = w.shape
    assert proj.shape[2] == ch
    assert s % SUBLANES == 0
    nb = _pick(bsz, (16, 8, 4, 2, 1))
    cc = _pick(ch, (512, 256, 128))
    kern = functools.partial(_conv_step_kernel, nb=nb, s=s, width=width, cc=cc)
    vec = pl.BlockSpec((1, ch), lambda b: (0, 0))
    frows = -(-(width - 1 + s) // SUBLANES) * SUBLANES
    return pl.pallas_call(
        kern,
        grid=(bsz // nb,),
        in_specs=[pl.BlockSpec((None, nb * s, ch), lambda b: (0, b, 0)),
                  pl.BlockSpec((None, nb * s, ch), lambda b: (1, b, 0)),
                  pl.BlockSpec((nb, width - 1, ch), lambda b: (b, 0, 0)),
                  pl.BlockSpec((width, ch), lambda b: (0, 0)),
                  vec, vec, vec],
        out_specs=[pl.BlockSpec((nb * s, ch), lambda b: (b, 0)),
                   pl.BlockSpec((nb, width - 1, ch), lambda b: (b, 0, 0))],
        out_shape=[jax.ShapeDtypeStruct((bsz * s, ch), BF16),
                   jax.ShapeDtypeStruct((bsz, width - 1, ch), F32)],
        scratch_shapes=[pltpu.VMEM((frows, ch), F32), pltpu.VMEM((nb * s, ch), F32)],
        compiler_params=_params(("parallel",)),
        name="conv_step",
    )(proj, proj, state, w, cb, lw, lb)


def _mlstm_kernel(qk_ref, v_ref, o_ref, g_ref, gt_ref, c0_ref, n0_ref, m0_ref, nw_ref,
                  hm_ref, c_out, n_out, m_out, c_s, n_s, m_s, *, n_heads, scale):
    c = pl.program_id(1)
    last = pl.num_programs(1) - 1
    ln = qk_ref.shape[0]
    wqk = qk_ref.shape[1] // 2
    dqk = wqk // n_heads
    dv = v_ref.shape[1] // n_heads

    @pl.when(c == 0)
    def _():
        c_s[...] = c0_ref[...]
        n_s[...] = n0_ref[...]
        m_s[...] = m0_ref[...]

    row = lax.broadcasted_iota(I32, (ln, ln), 0)
    col = lax.broadcasted_iota(I32, (ln, ln), 1)
    tri = row >= col
    tri_t = row <= col
    g = g_ref[...]
    gt = gt_ref[...]

    for hd in range(n_heads):
        li_c = g[:, hd:hd + 1]
        lf_c = g[:, n_heads + hd:n_heads + hd + 1]
        li_r = gt[hd:hd + 1, :]
        lf_r = gt[n_heads + hd:n_heads + hd + 1, :]
        b_c = jnp.sum(jnp.where(tri, lf_r, 0.0), axis=1, keepdims=True)
        b_r = jnp.sum(jnp.where(tri_t, lf_c, 0.0), axis=0, keepdims=True)
        b_l = jnp.sum(lf_r, axis=1, keepdims=True)
        m_prev = m_s[:, hd:hd + 1]

        dmat = jnp.where(tri, b_c - b_r + li_r, NEG_BIG)
        inter = b_c + m_prev
        m_t = jnp.maximum(inter, jnp.max(dmat, axis=1, keepdims=True))
        a = jnp.exp(inter - m_t)

        q = qk_ref[:, hd * dqk:(hd + 1) * dqk]
        k = qk_ref[:, wqk + hd * dqk:wqk + (hd + 1) * dqk] * scale
        qb = q.astype(BF16)
        vb = v_ref[:, hd * dv:(hd + 1) * dv].astype(BF16)
        cst = c_s[hd]
        nst = n_s[hd:hd + 1, :]
        s = lax.dot_general(qb, k.astype(BF16), (((1,), (1,)), ((), ())), preferred_element_type=F32)
        s = s * jnp.exp(dmat - m_t)
        num = a * jnp.dot(qb, cst.astype(BF16), preferred_element_type=F32) \
            + jnp.dot(s.astype(BF16), vb, preferred_element_type=F32)
        den = a * jnp.sum(q * nst, axis=1, keepdims=True) + jnp.sum(s, axis=1, keepdims=True)
        h = num / jnp.maximum(jnp.abs(den), jnp.exp(-m_t))
        hn = h * lax.rsqrt(jnp.mean(h * h, -1, keepdims=True) + EPS) * nw_ref[hd:hd + 1, :]
        hm_ref[:, hd * dv:(hd + 1) * dv] = (hn * _sigmoid(o_ref[:, hd * dv:(hd + 1) * dv])).astype(hm_ref.dtype)

        g_r = b_l - b_r + li_r
        g_c = b_l - b_c + li_c
        m_new = jnp.maximum(b_l + m_prev, jnp.max(g_r, axis=1, keepdims=True))
        decay = jnp.exp(b_l + m_prev - m_new)
        kw = k * jnp.exp(g_c - m_new)
        c_new = decay * cst + lax.dot_general(kw.astype(BF16), vb, (((0,), (0,)), ((), ())),
                                              preferred_element_type=F32)
        n_new = decay * nst + jnp.sum(kw, axis=0, keepdims=True)
        c_s[hd] = c_new
        n_s[hd:hd + 1, :] = n_new
        m_s[:, hd:hd + 1] = m_new

    @pl.when(c == last)
    def _():
        c_out[...] = c_s[...]
        n_out[...] = n_s[...]
        m_out[...] = m_s[...]


def _mlstm(proj, g, c0, n0, m0, norm_w, bsz, s, slab0):
    _, n_heads, dqk, dv = c0.shape
    ln = s if s <= MLSTM_CHUNK else MLSTM_CHUNK
    assert s % ln == 0 and ln % SUBLANES == 0
    nc = s // ln
    wqk, wv = n_heads * dqk, n_heads * dv
    slab = proj.shape[2]
    assert 2 * wqk == slab and wv == slab
    gt3 = g[:, :2 * n_heads].reshape(bsz * nc, ln, 2 * n_heads).transpose(0, 2, 1)
    kern = functools.partial(_mlstm_kernel, n_heads=n_heads, scale=dqk ** -0.5)
    rows = lambda col: (lambda b, c: (b * nc + c, col))
    slab_rows = lambda k: pl.BlockSpec((None, ln, slab), lambda b, c: (slab0 + k, b * nc + c, 0))
    per_seq = lambda shape: pl.BlockSpec((None,) + shape, lambda b, c: (b,) + (0,) * len(shape))
    out = pl.pallas_call(
        kern,
        grid=(bsz, nc),
        in_specs=[slab_rows(0), slab_rows(1), slab_rows(2),
                  pl.BlockSpec((ln, LANES), rows(0)),
                  pl.BlockSpec((None, 2 * n_heads, ln), lambda b, c: (b * nc + c, 0, 0)),
                  per_seq((n_heads, dqk, dv)), per_seq((n_heads, dqk)), per_seq((1, n_heads)),
                  pl.BlockSpec((n_heads, dv), lambda b, c: (0, 0))],
        out_specs=[pl.BlockSpec((ln, wv), rows(0)),
                   per_seq((n_heads, dqk, dv)), per_seq((n_heads, dqk)), per_seq((1, n_heads))],
        out_shape=[jax.ShapeDtypeStruct((bsz * s, wv), BF16),
                   jax.ShapeDtypeStruct((bsz, n_heads, dqk, dv), F32),
                   jax.ShapeDtypeStruct((bsz, n_heads, dqk), F32),
                   jax.ShapeDtypeStruct((bsz, 1, n_heads), F32)],
        scratch_shapes=[pltpu.VMEM((n_heads, dqk, dv), F32), pltpu.VMEM((n_heads, dqk), F32),
                        pltpu.VMEM((1, n_heads), F32)],
        compiler_params=_params(("arbitrary", "arbitrary")),
        name="mlstm",
    )(proj, proj, proj, g, gt3, c0, n0, m0.reshape(bsz, 1, n_heads), norm_w)
    hm, c_new, n_new, m_new = out
    return hm, c_new, n_new, m_new.reshape(bsz, n_heads)


def _outproj_kernel(co_ref, hm_ref, w_ref, o_ref):
    kc = co_ref.shape[1]
    o_ref[...] = jnp.dot(co_ref[...], w_ref[0:kc, :], preferred_element_type=F32) \
        + jnp.dot(hm_ref[...], w_ref[kc:2 * kc, :], preferred_element_type=F32)


def _outproj(co, hm, w_bf):
    m, kc = co.shape
    d = w_bf.shape[-1]
    assert hm.shape[1] == kc and w_bf.shape[0] == 2 * kc
    tn = _pick(d, (2048, 1024, 512, 256, 128))
    tm = _pick(m, (512, 256, 128, 64, 32, 16, 8))
    return pl.pallas_call(
        _outproj_kernel,
        grid=(d // tn, m // tm),
        in_specs=[pl.BlockSpec((tm, kc), lambda j, i: (i, 0)),
                  pl.BlockSpec((tm, kc), lambda j, i: (i, 0)),
                  pl.BlockSpec((2 * kc, tn), lambda j, i: (0, j))],
        out_specs=pl.BlockSpec((tm, tn), lambda j, i: (i, j)),
        out_shape=jax.ShapeDtypeStruct((m, d), F32),
        compiler_params=_params(("parallel", "parallel")),
        name="outproj",
    )(co, hm, w_bf)


def _router_kernel(mix_ref, x_ref, g1_ref, sc2_ref, sh2_ref, n1_ref, n2_ref, wr_ref, br_ref,
                   x1_ref, h2_ref, ti_ref, tg_ref, *, st):
    mix = mix_ref[...]
    mn = mix * lax.rsqrt(jnp.mean(mix * mix, -1, keepdims=True) + EPS) * n1_ref[...]
    x1 = x_ref[...] + _mod_rows(g1_ref, st) * mn
    x1_ref[...] = x1
    y2 = x1 * lax.rsqrt(jnp.mean(x1 * x1, -1, keepdims=True) + EPS) * n2_ref[...]
    h2 = y2 * (1.0 + _mod_rows(sc2_ref, st)) + _mod_rows(sh2_ref, st)
    half = h2.shape[1] // 2
    h2_ref[...] = _pack_pair(h2[:, :half], h2[:, half:])
    logits = jnp.dot(h2.astype(BF16), wr_ref[...], preferred_element_type=F32) + br_ref[...]
    n_exp = logits.shape[1]
    lane = lax.broadcasted_iota(I32, logits.shape, 1)
    lane_o = lax.broadcasted_iota(I32, ti_ref.shape, 1)
    idx_out = jnp.zeros(ti_ref.shape, I32)
    val_out = jnp.zeros(tg_ref.shape, F32)
    top = None
    den = jnp.zeros((logits.shape[0], 1), F32)
    for r in range(TOP_K):
        mx = jnp.max(logits, axis=1, keepdims=True)
        ix = jnp.min(jnp.where(logits == mx, lane, n_exp), axis=1, keepdims=True)
        if top is None:
            top = mx
        e = jnp.exp(mx - top)
        den = den + e
        idx_out = jnp.where(lane_o == r, ix, idx_out)
        val_out = jnp.where(lane_o == r, e, val_out)
        logits = jnp.where(lane == ix, NEG_BIG, logits)
    ti_ref[...] = idx_out
    tg_ref[...] = val_out / den


def _router(mix, x2, mod3, grp, n1, n2, wr, br):
    rows, d = x2.shape
    n_exp = wr.shape[1]
    kern = functools.partial(_router_kernel, st=grp.st)
    return pl.pallas_call(
        kern,
        grid=grp.grid,
        in_specs=[grp.rows(d), grp.rows(d), grp.mod(d, 2), grp.mod(d, 4), grp.mod(d, 3),
                  grp.const((1, d)), grp.const((1, d)), grp.const((d, n_exp)), grp.const((1, n_exp))],
        out_specs=[grp.rows(d), grp.rows(d // 2), grp.rows(LANES), grp.rows(LANES)],
        out_shape=[jax.ShapeDtypeStruct((rows, d), F32), jax.ShapeDtypeStruct((rows, d // 2), U32),
                   jax.ShapeDtypeStruct((rows, LANES), I32), jax.ShapeDtypeStruct((rows, LANES), F32)],
        compiler_params=_params(("parallel", "parallel")),
        name="router",
    )(mix, x2, mod3, mod3, mod3, n1, n2, wr, br)


def _rank_kernel(ti_ref, ps_ref, o_ref, carry_ref):
    @pl.when(pl.program_id(0) == 0)
    def _():
        carry_ref[...] = jnp.zeros(carry_ref.shape, F32)

    ti = ti_ref[...].astype(F32)
    tt = ti.shape[0]
    lane = lax.broadcasted_iota(I32, ti.shape, 1)
    lane_f = lane.astype(F32)
    cols = []
    member = jnp.zeros(ti.shape, F32)
    for kk in range(TOP_K):
        ek = jnp.sum(jnp.where(lane == kk, ti, 0.0), axis=1, keepdims=True)
        cols.append(ek)
        member = member + jnp.where(lane_f == ek, 1.0, 0.0)
    r = lax.broadcasted_iota(I32, (tt, tt), 0)
    c = lax.broadcasted_iota(I32, (tt, tt), 1)
    before = jnp.where(r > c, 1.0, 0.0).astype(BF16)
    base = jnp.dot(before, member.astype(BF16), preferred_element_type=F32) + carry_ref[...] + ps_ref[...]
    out = jnp.zeros(o_ref.shape, I32)
    for kk in range(TOP_K):
        dk = jnp.sum(jnp.where(lane_f == cols[kk], base, 0.0), axis=1, keepdims=True)
        out = jnp.where(lane == kk, dk.astype(I32), out)
    o_ref[...] = out
    carry_ref[...] += jnp.sum(member, axis=0, keepdims=True)


def _rank(top_i, pad_start):
    t = top_i.shape[0]
    tt = _pick(t, (512, 256, 128, 64, 32, 16, 8))
    return pl.pallas_call(
        _rank_kernel,
        grid=(t // tt,),
        in_specs=[pl.BlockSpec((tt, LANES), lambda i: (i, 0)), pl.BlockSpec((1, LANES), lambda i: (0, 0))],
        out_specs=pl.BlockSpec((tt, LANES), lambda i: (i, 0)),
        out_shape=jax.ShapeDtypeStruct((t, LANES), I32),
        scratch_shapes=[pltpu.VMEM((1, LANES), F32)],
        compiler_params=_params(("arbitrary",)),
        name="moe_rank",
    )(top_i, pad_start)


def _route(top_i, n_experts, tm):
    t = top_i.shape[0]
    a = t * TOP_K
    flat_e = top_i[:, :TOP_K].reshape(a)
    counts = jnp.sum((flat_e[:, None] == jnp.arange(n_experts, dtype=I32)[None, :]).astype(I32), axis=0)
    padded = (counts + tm - 1) // tm * tm
    pad_end = jnp.cumsum(padded)
    pad_start = pad_end - padded
    nt = a // tm + n_experts
    tile_start = jnp.arange(nt, dtype=I32) * tm
    tile_u = (tile_start < pad_end[-1]).astype(I32)
    n_used = jnp.sum(tile_u)
    tile_e = jnp.sum((pad_end[None, :] <= tile_start[:, None]).astype(I32), axis=1)
    tile_e = jnp.minimum(tile_e, n_experts - 1)
    tile_e = jnp.where(tile_u == 1, tile_e, tile_e[jnp.maximum(n_used - 1, 0)])
    idx = jnp.arange(nt, dtype=I32)
    starts = jnp.logical_and(tile_u == 1, jnp.logical_or(idx == 0, tile_e != jnp.roll(tile_e, 1)))
    seg = (jnp.cumsum(starts.astype(I32)) - 1).astype(I32)
    later = lax.cummin(jnp.where(starts, idx, nt)[::-1])[::-1]
    nxt = jnp.concatenate([later[1:], jnp.full((1,), nt, I32)])
    next_e = tile_e[jnp.where(nxt >= nt, 0, nxt)]
    n_seg = jnp.sum(starts.astype(I32)).reshape(1)
    x_tile = jnp.minimum(idx, jnp.maximum(n_used - 1, 0))
    tiles = (tile_e, tile_u, seg, next_e, n_seg, x_tile)
    ps = jnp.zeros((1, LANES), F32).at[0, :n_experts].set(pad_start.astype(F32))
    dest = _rank(top_i, ps)
    return dest, tiles, nt


def _row_copy(src_ref, dst_ref, sem, src_row, dst_row):
    return pltpu.make_async_copy(src_ref.at[pl.ds(src_row, 1), :], dst_ref.at[pl.ds(dst_row, 1), :], sem)


def _dispatch_kernel(dest_ref, src_ref, xs_in, xs_out, sem):
    del xs_in
    tt = src_ref.shape[0]

    def start(r, carry):
        for kk in range(TOP_K):
            _row_copy(src_ref, xs_out, sem, r, dest_ref[0, r * TOP_K + kk]).start()
        return carry

    lax.fori_loop(0, tt, start, 0, unroll=ISSUE_UNROLL)
    for _ in range(TOP_K):
        pltpu.make_async_copy(src_ref, src_ref, sem).wait()


def _dispatch(h2p, dest, xs):
    t, w = h2p.shape
    tt = _pick(t, (256, 128, 64, 32, 16, 8))
    nt = t // tt
    return pl.pallas_call(
        _dispatch_kernel,
        grid=(nt,),
        in_specs=[pl.BlockSpec((None, 1, tt * TOP_K), lambda i: (i, 0, 0), memory_space=pltpu.SMEM),
                  pl.BlockSpec((tt, w), lambda i: (i, 0)),
                  pl.BlockSpec(memory_space=pl.ANY)],
        out_specs=pl.BlockSpec(memory_space=pl.ANY),
        out_shape=jax.ShapeDtypeStruct(xs.shape, xs.dtype),
        scratch_shapes=[pltpu.SemaphoreType.DMA(())],
        input_output_aliases={2: 0},
        compiler_params=_params(("arbitrary",)),
        name="moe_dispatch",
    )(dest[:, :TOP_K].reshape(nt, 1, tt * TOP_K), h2p, xs)


N_TILE_TABLES = 6


def _segment_weights(tables, w_hbm, wst_ref, sem, *, l, tn, col_offs):
    te_ref, tu_ref, sg_ref, ne_ref, ns_ref = tables[:5]
    j = pl.program_id(0)
    i = pl.program_id(1)
    n_seg = ns_ref[0]
    first = jnp.logical_and(tu_ref[i] == 1,
                            jnp.logical_or(i == 0, te_ref[i] != te_ref[jnp.maximum(i - 1, 0)]))
    g = j * n_seg + sg_ref[i]

    def copies(e, jj):
        return [pltpu.make_async_copy(
            w_hbm.at[l, e, :, pl.ds(pl.multiple_of((off + jj) * tn, tn), tn)], wst_ref.at[m], sem)
            for m, off in enumerate(col_offs)]

    @pl.when(jnp.logical_and(j == 0, i == 0))
    def _():
        for cp in copies(te_ref[0], 0):
            cp.start()

    def wait_weights():
        for cp in copies(te_ref[i], j):
            cp.wait()

    def start_next():
        @pl.when(g + 1 < pl.num_programs(0) * n_seg)
        def _():
            jn = jnp.where(sg_ref[i] + 1 == n_seg, j + 1, j)
            for cp in copies(ne_ref[i], jn):
                cp.start()

    return first, wait_weights, start_next


def _gmm1_kernel(*refs, l, tn, nj):
    tables = refs[:N_TILE_TABLES]
    x_ref, w_hbm, bg_ref, bu_ref, o_ref, wst_ref, wgb_ref, wub_ref, sem = refs[N_TILE_TABLES:]
    used = tables[1][pl.program_id(1)] == 1
    first, wait_weights, start_next = _segment_weights(tables, w_hbm, wst_ref, sem, l=l, tn=tn,
                                                       col_offs=(0, nj))
    half = x_ref.shape[1]

    def tile(wg_lo, wg_hi, wu_lo, wu_hi):
        xlo, xhi = _unpack_pair(x_ref[...])
        gt = jnp.dot(xlo, wg_lo, preferred_element_type=F32) \
            + jnp.dot(xhi, wg_hi, preferred_element_type=F32) + bg_ref[...]
        up = jnp.dot(xlo, wu_lo, preferred_element_type=F32) \
            + jnp.dot(xhi, wu_hi, preferred_element_type=F32) + bu_ref[...]
        gt = jnp.minimum(gt, SWIGLU_LIMIT)
        up = jnp.clip(up, -SWIGLU_LIMIT, SWIGLU_LIMIT)
        act = (up + 1.0) * gt * _sigmoid(SWIGLU_ALPHA * gt)
        o_ref[...] = act.astype(o_ref.dtype)

    @pl.when(first)
    def _():
        wait_weights()
        ws = []
        for m, dst in enumerate((wgb_ref, wub_ref)):
            for k0 in (0, half):
                wk = wst_ref[m, k0:k0 + half, :].astype(BF16)
                dst[k0:k0 + half, :] = wk
                ws.append(wk)
        tile(*ws)
        start_next()

    @pl.when(jnp.logical_and(used, jnp.logical_not(first)))
    def _():
        tile(wgb_ref[0:half, :], wgb_ref[half:2 * half, :], wub_ref[0:half, :], wub_ref[half:2 * half, :])

    @pl.when(jnp.logical_not(used))
    def _():
        o_ref[...] = jnp.zeros(o_ref.shape, o_ref.dtype)


def _gmm1(xs, w1, b1, tiles, l, tm):
    rows, half = xs.shape
    d = 2 * half
    dff = w1.shape[-1] // 2
    nt = rows // tm
    tn = _pick(dff, (1024, 512, 256, 128))
    nj = dff // tn
    b1r = b1.reshape(b1.shape[0], b1.shape[1], 1, 2 * dff)
    kern = functools.partial(_gmm1_kernel, l=l, tn=tn, nj=nj)
    grid_spec = pltpu.PrefetchScalarGridSpec(
        num_scalar_prefetch=N_TILE_TABLES,
        grid=(nj, nt),
        in_specs=[pl.BlockSpec((tm, half), lambda j, i, *t: (t[5][i], 0)),
                  pl.BlockSpec(memory_space=pl.ANY),
                  pl.BlockSpec((None, None, 1, tn), lambda j, i, te, *t: (l, te[i], 0, j)),
                  pl.BlockSpec((None, None, 1, tn), lambda j, i, te, *t: (l, te[i], 0, nj + j))],
        out_specs=pl.BlockSpec((tm, tn), lambda j, i, *t: (i, j)),
        scratch_shapes=[pltpu.VMEM((2, d, tn), F32), pltpu.VMEM((d, tn), BF16), pltpu.VMEM((d, tn), BF16),
                        pltpu.SemaphoreType.DMA(())])
    return pl.pallas_call(
        kern,
        grid_spec=grid_spec,
        out_shape=jax.ShapeDtypeStruct((rows, dff), BF16),
        compiler_params=_params(("arbitrary", "arbitrary"), GMM_VMEM_LIMIT),
        name="moe_gmm1",
    )(*tiles, xs, w1, b1r, b1r)


def _gmm2_kernel(*refs, l, tn):
    tables = refs[:N_TILE_TABLES]
    a_ref, w_hbm, b_ref, o_ref, wst_ref, wb_ref, sem = refs[N_TILE_TABLES:]
    used = tables[1][pl.program_id(1)] == 1
    first, wait_weights, start_next = _segment_weights(tables, w_hbm, wst_ref, sem, l=l, tn=tn, col_offs=(0,))

    def tile(w):
        y = jnp.dot(a_ref[...], w, preferred_element_type=F32) + b_ref[...]
        o_ref[...] = _pack_pair(y[:, :tn // 2], y[:, tn // 2:])

    @pl.when(first)
    def _():
        wait_weights()
        w = wst_ref[0].astype(BF16)
        wb_ref[...] = w
        tile(w)
        start_next()

    @pl.when(jnp.logical_and(used, jnp.logical_not(first)))
    def _():
        tile(wb_ref[...])

    @pl.when(jnp.logical_not(used))
    def _():
        o_ref[...] = jnp.zeros(o_ref.shape, o_ref.dtype)


def _gmm2(act, w2, b2, tiles, l, tm):
    rows, dff = act.shape
    d = w2.shape[-1]
    nt = rows // tm
    tn = _pick(d, (4096, 2048, 1024, 512, 256, 128))
    b2r = b2.reshape(b2.shape[0], b2.shape[1], 1, d)
    kern = functools.partial(_gmm2_kernel, l=l, tn=tn)
    grid_spec = pltpu.PrefetchScalarGridSpec(
        num_scalar_prefetch=N_TILE_TABLES,
        grid=(d // tn, nt),
        in_specs=[pl.BlockSpec((tm, dff), lambda j, i, *t: (i, 0)),
                  pl.BlockSpec(memory_space=pl.ANY),
                  pl.BlockSpec((None, None, 1, tn), lambda j, i, te, *t: (l, te[i], 0, j))],
        out_specs=pl.BlockSpec((tm, tn // 2), lambda j, i, *t: (i, j)),
        scratch_shapes=[pltpu.VMEM((1, dff, tn), F32), pltpu.VMEM((dff, tn), BF16),
                        pltpu.SemaphoreType.DMA(())])
    y = pl.pallas_call(
        kern,
        grid_spec=grid_spec,
        out_shape=jax.ShapeDtypeStruct((rows, d // 2), U32),
        compiler_params=_params(("arbitrary", "arbitrary"), GMM_VMEM_LIMIT),
        name="moe_gmm2",
    )(*tiles, act, w2, b2r)
    return y, tn // 2


def _combine_kernel(pos_ref, posn_ref, y_hbm, tg_ref, x1_ref, g2_ref, nw_ref, o_ref, buf_ref, sem, *, st, pw):
    ni = pl.num_programs(1)
    n = pl.program_id(0) * ni + pl.program_id(1)
    total = pl.num_programs(0) * ni
    tt, d = o_ref.shape
    slot = lax.rem(n, 2)

    def start_rows(idx_ref, sl):
        def body(r, carry):
            for kk in range(TOP_K):
                _row_copy(y_hbm, buf_ref.at[sl, kk], sem.at[sl], idx_ref[0, r * TOP_K + kk], r).start()
            return carry
        lax.fori_loop(0, tt, body, 0, unroll=ISSUE_UNROLL)

    @pl.when(n == 0)
    def _():
        start_rows(pos_ref, 0)

    @pl.when(n + 1 < total)
    def _():
        start_rows(posn_ref, 1 - slot)

    pltpu.make_async_copy(buf_ref.at[slot], buf_ref.at[slot], sem.at[slot]).wait()

    gates = tg_ref[...]
    lane = lax.broadcasted_iota(I32, gates.shape, 1)
    gk = [jnp.sum(jnp.where(lane == kk, gates, 0.0), axis=1, keepdims=True) for kk in range(TOP_K)]
    ssq = jnp.zeros((tt, 1), F32)
    for blk in range(d // (2 * pw)):
        lo = jnp.zeros((tt, pw), F32)
        hi = jnp.zeros((tt, pw), F32)
        for kk in range(TOP_K):
            p = buf_ref[slot, kk, :, blk * pw:(blk + 1) * pw]
            lo = lo + gk[kk] * lax.bitcast_convert_type(p << 16, F32)
            hi = hi + gk[kk] * lax.bitcast_convert_type(p & jnp.uint32(HI_HALF), F32)
        ssq = ssq + jnp.sum(lo * lo, -1, keepdims=True) + jnp.sum(hi * hi, -1, keepdims=True)
        o_ref[:, 2 * pw * blk:2 * pw * blk + pw] = lo
        o_ref[:, 2 * pw * blk + pw:2 * pw * (blk + 1)] = hi
    fn = o_ref[...] * lax.rsqrt(ssq / d + EPS) * nw_ref[...]
    o_ref[...] = x1_ref[...] + _mod_rows(g2_ref, st) * fn


def _combine(y, pw, dest, tg, x1, mod3, grp, nw):
    rows, d = x1.shape
    tt = grp.ts
    nt = rows // tt
    ni = grp.n_inner
    kern = functools.partial(_combine_kernel, st=grp.st, pw=pw)
    dest3 = dest[:, :TOP_K].reshape(nt, 1, tt * TOP_K)
    idx_spec = lambda ahead: pl.BlockSpec(
        (None, 1, tt * TOP_K), lambda o, i: (jnp.minimum(o * ni + i + ahead, nt - 1), 0, 0),
        memory_space=pltpu.SMEM)
    return pl.pallas_call(
        kern,
        grid=grp.grid,
        in_specs=[idx_spec(0), idx_spec(1),
                  pl.BlockSpec(memory_space=pl.ANY),
                  grp.rows(LANES), grp.rows(d), grp.mod(d, 5), grp.const((1, d))],
        out_specs=grp.rows(d),
        out_shape=jax.ShapeDtypeStruct((rows, d), F32),
        scratch_shapes=[pltpu.VMEM((2, TOP_K, tt, d // 2), U32), pltpu.SemaphoreType.DMA((2,))],
        compiler_params=_params(("arbitrary", "arbitrary")),
        name="moe_combine",
    )(dest3, dest3, y, tg, x1, mod3, nw)


def _mixer(x2, mod3, grp, conv_buf, c0, n0, m0, p, l):
    n_heads = c0.shape[1]
    ch = p['conv_w'].shape[-1]
    h, g = _prenorm(x2, mod3, grp, p['norm1_pre'], p['w_gate'], p['b_gate'], n_heads)
    proj = _inproj(h, p['w_in'], ch)
    conv = _conv_seq if grp.nbq == 1 else _conv_step
    co, new_buf = conv(proj, conv_buf, p['conv_w'], p['conv_b'], p['conv_ln_w'], p['conv_ln_b'], grp.bsz, grp.s)
    hm, c_new, n_new, m_new = _mlstm(proj, g, c0, n0, m0, p['mlstm_norm_w'], grp.bsz, grp.s, 2)
    mix = _outproj(co, hm, p['w_out'])
    x1, h2p, ti, tg = _router(mix, x2, mod3, grp, p['norm1_post'], p['norm2_pre'], p['w_router'], p['b_router'])
    return dict(x1=x1, h2p=h2p, ti=ti, tg=tg, state=(new_buf, c_new, n_new, m_new))


def kernel(x_prompt, x_sample, c_prompt, c_sample, state_conv, state_mlstm_C, state_mlstm_n, state_mlstm_m,
           w_ada, b_ada, norm1_pre, w_in, b_gates, conv_w, conv_b, conv_ln_w, conv_ln_b, mlstm_norm_w,
           w_out, norm1_post, norm2_pre, w_router, b_router, w1, b1, w2, b2, norm2_post):
    depth = w_ada.shape[0]
    bp, sp, d = x_prompt.shape
    bs, ss, _ = x_sample.shape
    n_heads, dqk, dv = state_mlstm_C.shape[2:]
    n_experts = w_router.shape[-1]
    nst, ch = state_conv.shape[2:]
    n_gate = 2 * n_heads
    assert n_gate <= LANES and n_experts <= LANES
    tp, tsmp = bp * sp, bs * ss

    mp = -(-(bp + bs) // SUBLANES) * SUBLANES
    c_all = jnp.zeros((mp, d), F32).at[:bs].set(c_sample).at[bs:bs + bp].set(c_prompt)
    grp_p = _Group(bp, sp, bs, ROW_TILE)
    grp_s = _Group(bs, ss, 0, ROW_TILE)

    xp, xs = x_prompt.reshape(tp, d), x_sample.reshape(tsmp, d)
    outs = [[] for _ in range(8)]
    for l in range(depth):
        row = lambda v: v[l].reshape(1, -1)
        p = dict(
            w_in=w_in[l, :, :w_in.shape[-1] - n_gate].astype(BF16), w_out=w_out[l].astype(BF16),
            w_gate=jnp.pad(w_in[l, :, w_in.shape[-1] - n_gate:], ((0, 0), (0, LANES - n_gate))),
            b_gate=jnp.pad(b_gates[l], (0, LANES - n_gate)).reshape(1, LANES),
            norm1_pre=row(norm1_pre), conv_w=conv_w[l], conv_b=row(conv_b), conv_ln_w=row(conv_ln_w),
            conv_ln_b=row(conv_ln_b), mlstm_norm_w=mlstm_norm_w[l], norm1_post=row(norm1_post),
            norm2_pre=row(norm2_pre), w_router=w_router[l].astype(BF16), b_router=row(b_router))
        mod3 = _ada(c_all, w_ada, b_ada, l).reshape(mp, 1, 6 * d)
        zero = lambda shape: jnp.zeros(shape, F32)
        gp = _mixer(xp, mod3, grp_p, zero((bp, nst, ch)), zero((bp, n_heads, dqk, dv)),
                    zero((bp, n_heads, dqk)), zero((bp, n_heads)), p, l)
        gs = _mixer(xs, mod3, grp_s, state_conv[l], state_mlstm_C[l], state_mlstm_n[l],
                    state_mlstm_m[l], p, l)

        top_i = jnp.concatenate([gp['ti'], gs['ti']], axis=0)
        dest, tiles, nt = _route(top_i, n_experts, MOE_ROWS)
        xsorted = jnp.zeros((nt * MOE_ROWS, d // 2), U32)
        xsorted = _dispatch(gp['h2p'], dest[:tp], xsorted)
        xsorted = _dispatch(gs['h2p'], dest[tp:], xsorted)
        act = _gmm1(xsorted, w1, b1, tiles, l, MOE_ROWS)
        y, pw = _gmm2(act, w2, b2, tiles, l, MOE_ROWS)
        nw2 = row(norm2_post)
        xp = _combine(y, pw, dest[:tp], gp['tg'], gp['x1'], mod3, grp_p, nw2)
        xs = _combine(y, pw, dest[tp:], gs['tg'], gs['x1'], mod3, grp_s, nw2)
        for o, v in zip(outs, gp['state'] + gs['state']):
            o.append(v)
    stack = (lambda o: o[0][None]) if depth == 1 else jnp.stack
    return (xp.reshape(bp, sp, d), xs.reshape(bs, ss, d)) + tuple(stack(o) for o in outs)
```

```python
import functools

import jax
import jax.numpy as jnp
from jax import lax
from jax.experimental import pallas as pl
from jax.experimental.pallas import tpu as pltpu

F32 = jnp.float32
BF16 = jnp.bfloat16
I32 = jnp.int32
U32 = jnp.uint32

EPS = 1e-6
GATE_CAP = 15.0
TOP_K = 4
SWIGLU_LIMIT = 7.0
SWIGLU_ALPHA = 1.702
NEG_BIG = -1e30

LANES = 128
SUBLANES = 8
VMEM_LIMIT = 56 * 1024 * 1024
GMM_VMEM_LIMIT = 60 * 1024 * 1024
MLSTM_CHUNK = 256
MOE_ROWS = 256
ROW_TILE = 256
HI_HALF = 0xFFFF0000
ISSUE_UNROLL = 4
WEIGHT_DMA_PRIORITY = 1


def _params(sem, vmem=VMEM_LIMIT):
    return pltpu.CompilerParams(dimension_semantics=sem, vmem_limit_bytes=vmem)


def _sigmoid(x):
    return 1.0 / (1.0 + jnp.exp(-x))


def _pick(n, prefs):
    for p in prefs:
        if n % p == 0:
            return p
    return n


def _pack_pair(lo, hi):
    lo_b = lax.bitcast_convert_type(lo.astype(BF16).astype(F32), U32) >> 16
    hi_b = lax.bitcast_convert_type(hi.astype(BF16).astype(F32), U32) & jnp.uint32(HI_HALF)
    return hi_b | lo_b


def _unpack_pair(p):
    lo = lax.bitcast_convert_type(p << 16, F32).astype(BF16)
    hi = lax.bitcast_convert_type(p & jnp.uint32(HI_HALF), F32).astype(BF16)
    return lo, hi


class _Group:
    def __init__(self, bsz, s, mod_row0, tile_rows):
        self.bsz, self.s = bsz, s
        if s >= tile_rows:
            self.nbq, self.st = 1, _pick(s, (tile_rows, 128, 64, 32, 16, 8))
            self.n_outer, self.n_inner = bsz, s // self.st
        else:
            self.nbq, self.st = _pick(bsz, (tile_rows // s, 8, 4, 2, 1)), s
            self.n_outer, self.n_inner = bsz // self.nbq, 1
        assert mod_row0 % self.nbq == 0
        self.mod_blk0 = mod_row0 // self.nbq
        self.ts = self.nbq * self.st
        self.grid = (self.n_outer, self.n_inner)

    def rows(self, width, col=0):
        ni = self.n_inner
        return pl.BlockSpec((self.ts, width), lambda o, i, *_: (o * ni + i, col))

    def mod(self, d, col):
        b0 = self.mod_blk0
        return pl.BlockSpec((self.nbq, 1, d), lambda o, i, *_: (b0 + o, 0, col))

    def const(self, shape):
        nd = len(shape)
        return pl.BlockSpec(shape, lambda o, i, *_: (0,) * nd)


def _mod_rows(m_ref, st):
    m = m_ref[...]
    nbq, _, d = m.shape
    if nbq == 1:
        return m[0]
    return jnp.broadcast_to(m, (nbq, st, d)).reshape(nbq * st, d)


def _ada_kernel(c_ref, w_ref, b_ref, o_ref):
    c = c_ref[...]
    s = (c * _sigmoid(c)).astype(BF16)
    o_ref[...] = jnp.dot(s, w_ref[...].astype(BF16), preferred_element_type=F32) + b_ref[...]


def _ada(c_all, w_ada, b_ada, l):
    mp, d = c_all.shape
    n = w_ada.shape[-1]
    tn = _pick(n, (512, 256, 128))
    return pl.pallas_call(
        _ada_kernel,
        grid=(n // tn,),
        in_specs=[pl.BlockSpec((mp, d), lambda j: (0, 0)),
                  pl.BlockSpec((None, d, tn), lambda j: (l, 0, j)),
                  pl.BlockSpec((None, 1, tn), lambda j: (l, 0, j))],
        out_specs=pl.BlockSpec((mp, tn), lambda j: (0, j)),
        out_shape=jax.ShapeDtypeStruct((mp, n), F32),
        compiler_params=_params(("parallel",)),
        name="ada",
    )(c_all, w_ada, b_ada.reshape(b_ada.shape[0], 1, n))


def _prenorm_kernel(x_ref, sc_ref, sh_ref, nw_ref, wg_ref, bg_ref, h_ref, g_ref, *, n_heads, st):
    x = x_ref[...]
    y = x * lax.rsqrt(jnp.mean(x * x, -1, keepdims=True) + EPS) * nw_ref[...]
    h = y * (1.0 + _mod_rows(sc_ref, st)) + _mod_rows(sh_ref, st)
    hb = h.astype(BF16)
    h_ref[...] = hb
    z = jnp.dot(hb, wg_ref[...].astype(BF16), preferred_element_type=F32) + bg_ref[...]
    cap = GATE_CAP * jnp.tanh(z / GATE_CAP)
    logsig = jnp.minimum(cap, 0.0) - jnp.log(1.0 + jnp.exp(-jnp.abs(cap)))
    lane = lax.broadcasted_iota(I32, z.shape, 1)
    g_ref[...] = jnp.where(lane < n_heads, cap, logsig)


def _prenorm(x2, mod3, grp, nw, wg, bg, n_heads):
    rows, d = x2.shape
    kern = functools.partial(_prenorm_kernel, n_heads=n_heads, st=grp.st)
    return pl.pallas_call(
        kern,
        grid=grp.grid,
        in_specs=[grp.rows(d), grp.mod(d, 1), grp.mod(d, 0), grp.const((1, d)),
                  grp.const((d, LANES)), grp.const((1, LANES))],
        out_specs=[grp.rows(d), grp.rows(LANES)],
        out_shape=[jax.ShapeDtypeStruct((rows, d), BF16), jax.ShapeDtypeStruct((rows, LANES), F32)],
        compiler_params=_params(("parallel", "parallel")),
        name="prenorm",
    )(x2, mod3, mod3, nw, wg, bg)


def _inproj_kernel(a_ref, w_ref, o_ref):
    o_ref[...] = jnp.dot(a_ref[...], w_ref[...], preferred_element_type=F32)


def _inproj(h, w_bf, slab):
    m, d = h.shape
    n = w_bf.shape[1]
    assert n % slab == 0
    tm = _pick(m, (512, 256, 128, 64, 32, 16, 8))
    return pl.pallas_call(
        _inproj_kernel,
        grid=(n // slab, m // tm),
        in_specs=[pl.BlockSpec((tm, d), lambda j, i: (i, 0)),
                  pl.BlockSpec((d, slab), lambda j, i: (0, j))],
        out_specs=pl.BlockSpec((None, tm, slab), lambda j, i: (j, i, 0)),
        out_shape=jax.ShapeDtypeStruct((n // slab, m, slab), F32),
        compiler_params=_params(("parallel", "parallel")),
        name="inproj",
    )(h, w_bf)


CONV_HALO = 32


def _conv_taps(f_ref, w_ref, base, rows, c0, cc, width):
    acc = jnp.zeros((rows, cc), F32)
    for ph in range(SUBLANES):
        if ph >= width:
            break
        n_al = (width - 1 - ph) // SUBLANES + 1
        gb = f_ref[base + ph: base + ph + rows + SUBLANES * (n_al - 1), c0:c0 + cc]
        for a in range(n_al):
            j = SUBLANES * a + ph
            acc = acc + w_ref[j:j + 1, c0:c0 + cc] * gb[SUBLANES * a:SUBLANES * a + rows]
    return acc


def _ln_swish(y, lw, lb):
    mu = jnp.mean(y, -1, keepdims=True)
    yc = y - mu
    yn = yc * lax.rsqrt(jnp.mean(yc * yc, -1, keepdims=True) + EPS) * lw + lb
    return yn * _sigmoid(yn)


def _conv_taps_strided(f_ref, w_ref, cb_ref, y_ref, off, ts, c, width):
    nseg = ts // SUBLANES
    lanes = slice(c * LANES, (c + 1) * LANES)
    wv = [jnp.broadcast_to(w_ref[j:j + 1, lanes], (SUBLANES, LANES)) for j in range(width)]
    acc = [None] * nseg
    for v in range(nseg + width - 1):
        yv = f_ref[c, pl.ds(v + off, SUBLANES, stride=nseg), :]
        for u in range(max(0, v - (width - 1)), min(nseg - 1, v) + 1):
            term = wv[v - u] * yv
            acc[u] = term if acc[u] is None else acc[u] + term
    cb = cb_ref[:, lanes]
    for u in range(nseg):
        y_ref[c, pl.ds(u, SUBLANES, stride=nseg), :] = acc[u] + cb


def _conv_seq_kernel(av_ref, ag_ref, st_ref, w_ref, cb_ref, lw_ref, lb_ref, o_ref, ns_ref, f_ref, y_ref,
                     *, ts, width):
    i = pl.program_id(1)
    off = CONV_HALO - (width - 1)
    nchunk = f_ref.shape[0]
    chunks = [slice(c * LANES, (c + 1) * LANES) for c in range(nchunk)]

    @pl.when(i == 0)
    def _():
        for c, lanes in enumerate(chunks):
            f_ref[c, 0:off, :] = jnp.zeros((off, LANES), F32)
            f_ref[c, off:CONV_HALO, :] = st_ref[:, lanes]

    u = av_ref[...] * _sigmoid(ag_ref[...])
    for c, lanes in enumerate(chunks):
        f_ref[c, CONV_HALO:CONV_HALO + ts, :] = u[:, lanes]
    for c in range(nchunk):
        _conv_taps_strided(f_ref, w_ref, cb_ref, y_ref, off, ts, c, width)
    y = jnp.concatenate([y_ref[c] for c in range(nchunk)], axis=1)
    o_ref[...] = _ln_swish(y, lw_ref[...], lb_ref[...]).astype(o_ref.dtype)

    @pl.when(i == pl.num_programs(1) - 1)
    def _():
        for c, lanes in enumerate(chunks):
            ns_ref[:, lanes] = f_ref[c, ts + off:ts + CONV_HALO, :]

    for c in range(nchunk):
        f_ref[c, 0:CONV_HALO, :] = f_ref[c, ts:ts + CONV_HALO, :]


def _conv_seq(proj, state, w, cb, lw, lb, bsz, s):
    width, ch = w.shape
    assert proj.shape[2] == ch
    ts = _pick(s, (128, 64, 32))
    ns = s // ts
    assert ch % LANES == 0 and ts >= CONV_HALO >= width - 1
    kern = functools.partial(_conv_seq_kernel, ts=ts, width=width)
    vec = pl.BlockSpec((1, ch), lambda b, i: (0, 0))
    return pl.pallas_call(
        kern,
        grid=(bsz, ns),
        in_specs=[pl.BlockSpec((None, ts, ch), lambda b, i: (0, b * ns + i, 0)),
                  pl.BlockSpec((None, ts, ch), lambda b, i: (1, b * ns + i, 0)),
                  pl.BlockSpec((None, width - 1, ch), lambda b, i: (b, 0, 0)),
                  pl.BlockSpec((width, ch), lambda b, i: (0, 0)),
                  vec, vec, vec],
        out_specs=[pl.BlockSpec((ts, ch), lambda b, i: (b * ns + i, 0)),
                   pl.BlockSpec((None, width - 1, ch), lambda b, i: (b, 0, 0))],
        out_shape=[jax.ShapeDtypeStruct((bsz * s, ch), BF16),
                   jax.ShapeDtypeStruct((bsz, width - 1, ch), F32)],
        scratch_shapes=[pltpu.VMEM((ch // LANES, CONV_HALO + ts, LANES), F32),
                        pltpu.VMEM((ch // LANES, ts, LANES), F32)],
        compiler_params=_params(("arbitrary", "arbitrary")),
        name="conv_seq",
    )(proj, proj, state, w, cb, lw, lb)


def _conv_step_kernel(av_ref, ag_ref, st_ref, w_ref, cb_ref, lw_ref, lb_ref, o_ref, ns_ref, f_ref, y_ref,
                      *, nb, s, width, cc):
    nst = width - 1
    ch = f_ref.shape[1]

    def body(q, carry):
        r = pl.multiple_of(q * s, s)
        f_ref[0:nst, :] = st_ref[q]
        f_ref[nst:nst + s, :] = av_ref[pl.ds(r, s), :] * _sigmoid(ag_ref[pl.ds(r, s), :])
        for c0 in range(0, ch, cc):
            acc = _conv_taps(f_ref, w_ref, 0, s, c0, cc, width)
            y_ref[pl.ds(r, s), c0:c0 + cc] = acc + cb_ref[:, c0:c0 + cc]
        ns_ref[q] = f_ref[s:s + nst, :]
        return carry

    lax.fori_loop(0, nb, body, 0)
    o_ref[...] = _ln_swish(y_ref[...], lw_ref[...], lb_ref[...]).astype(o_ref.dtype)


def _conv_step(proj, state, w, cb, lw, lb, bsz, s):
    width, ch = w.shape
    assert proj.shape[2] == ch
    assert s % SUBLANES == 0
    nb = _pick(bsz, (16, 8, 4, 2, 1))
    cc = _pick(ch, (512, 256, 128))
    kern = functools.partial(_conv_step_kernel, nb=nb, s=s, width=width, cc=cc)
    vec = pl.BlockSpec((1, ch), lambda b: (0, 0))
    frows = -(-(width - 1 + s) // SUBLANES) * SUBLANES
    return pl.pallas_call(
        kern,
        grid=(bsz // nb,),
        in_specs=[pl.BlockSpec((None, nb * s, ch), lambda b: (0, b, 0)),
                  pl.BlockSpec((None, nb * s, ch), lambda b: (1, b, 0)),
                  pl.BlockSpec((nb, width - 1, ch), lambda b: (b, 0, 0)),
                  pl.BlockSpec((width, ch), lambda b: (0, 0)),
                  vec, vec, vec],
        out_specs=[pl.BlockSpec((nb * s, ch), lambda b: (b, 0)),
                   pl.BlockSpec((nb, width - 1, ch), lambda b: (b, 0, 0))],
        out_shape=[jax.ShapeDtypeStruct((bsz * s, ch), BF16),
                   jax.ShapeDtypeStruct((bsz, width - 1, ch), F32)],
        scratch_shapes=[pltpu.VMEM((frows, ch), F32), pltpu.VMEM((nb * s, ch), F32)],
        compiler_params=_params(("parallel",)),
        name="conv_step",
    )(proj, proj, state, w, cb, lw, lb)


def _mlstm_kernel(qk_ref, v_ref, o_ref, g_ref, gt_ref, c0_ref, n0_ref, m0_ref, nw_ref,
                  hm_ref, c_out, n_out, m_out, c_s, n_s, m_s, *, n_heads, scale):
    c = pl.program_id(1)
    last = pl.num_programs(1) - 1
    ln = qk_ref.shape[0]
    wqk = qk_ref.shape[1] // 2
    dqk = wqk // n_heads
    dv = v_ref.shape[1] // n_heads

    @pl.when(c == 0)
    def _():
        c_s[...] = c0_ref[...]
        n_s[...] = n0_ref[...]
        m_s[...] = m0_ref[...]

    row = lax.broadcasted_iota(I32, (ln, ln), 0)
    col = lax.broadcasted_iota(I32, (ln, ln), 1)
    tri = row >= col
    tri_t = row <= col
    g = g_ref[...]
    gt = gt_ref[...]

    for hd in range(n_heads):
        li_c = g[:, hd:hd + 1]
        lf_c = g[:, n_heads + hd:n_heads + hd + 1]
        li_r = gt[hd:hd + 1, :]
        lf_r = gt[n_heads + hd:n_heads + hd + 1, :]
        b_c = jnp.sum(jnp.where(tri, lf_r, 0.0), axis=1, keepdims=True)
        b_r = jnp.sum(jnp.where(tri_t, lf_c, 0.0), axis=0, keepdims=True)
        b_l = jnp.sum(lf_r, axis=1, keepdims=True)
        m_prev = m_s[:, hd:hd + 1]

        dmat = jnp.where(tri, b_c - b_r + li_r, NEG_BIG)
        inter = b_c + m_prev
        m_t = jnp.maximum(inter, jnp.max(dmat, axis=1, keepdims=True))
        a = jnp.exp(inter - m_t)

        q = qk_ref[:, hd * dqk:(hd + 1) * dqk]
        k = qk_ref[:, wqk + hd * dqk:wqk + (hd + 1) * dqk] * scale
        qb = q.astype(BF16)
        vb = v_ref[:, hd * dv:(hd + 1) * dv].astype(BF16)
        cst = c_s[hd]
        nst = n_s[hd:hd + 1, :]
        s = lax.dot_general(qb, k.astype(BF16), (((1,), (1,)), ((), ())), preferred_element_type=F32)
        s = s * jnp.exp(dmat - m_t)
        num = a * jnp.dot(qb, cst.astype(BF16), preferred_element_type=F32) \
            + jnp.dot(s.astype(BF16), vb, preferred_element_type=F32)
        den = a * jnp.sum(q * nst, axis=1, keepdims=True) + jnp.sum(s, axis=1, keepdims=True)
        h = num / jnp.maximum(jnp.abs(den), jnp.exp(-m_t))
        hn = h * lax.rsqrt(jnp.mean(h * h, -1, keepdims=True) + EPS) * nw_ref[hd:hd + 1, :]
        hm_ref[:, hd * dv:(hd + 1) * dv] = (hn * _sigmoid(o_ref[:, hd * dv:(hd + 1) * dv])).astype(hm_ref.dtype)

        g_r = b_l - b_r + li_r
        g_c = b_l - b_c + li_c
        m_new = jnp.maximum(b_l + m_prev, jnp.max(g_r, axis=1, keepdims=True))
        decay = jnp.exp(b_l + m_prev - m_new)
        kw = k * jnp.exp(g_c - m_new)
        c_new = decay * cst + lax.dot_general(kw.astype(BF16), vb, (((0,), (0,)), ((), ())),
                                              preferred_element_type=F32)
        n_new = decay * nst + jnp.sum(kw, axis=0, keepdims=True)
        c_s[hd] = c_new
        n_s[hd:hd + 1, :] = n_new
        m_s[:, hd:hd + 1] = m_new

    @pl.when(c == last)
    def _():
        c_out[...] = c_s[...]
        n_out[...] = n_s[...]
        m_out[...] = m_s[...]


def _mlstm(proj, g, c0, n0, m0, norm_w, bsz, s, slab0):
    _, n_heads, dqk, dv = c0.shape
    ln = s if s <= MLSTM_CHUNK else MLSTM_CHUNK
    assert s % ln == 0 and ln % SUBLANES == 0
    nc = s // ln
    wqk, wv = n_heads * dqk, n_heads * dv
    slab = proj.shape[2]
    assert 2 * wqk == slab and wv == slab
    gt3 = g[:, :2 * n_heads].reshape(bsz * nc, ln, 2 * n_heads).transpose(0, 2, 1)
    kern = functools.partial(_mlstm_kernel, n_heads=n_heads, scale=dqk ** -0.5)
    rows = lambda col: (lambda b, c: (b * nc + c, col))
    slab_rows = lambda k: pl.BlockSpec((None, ln, slab), lambda b, c: (slab0 + k, b * nc + c, 0))
    per_seq = lambda shape: pl.BlockSpec((None,) + shape, lambda b, c: (b,) + (0,) * len(shape))
    out = pl.pallas_call(
        kern,
        grid=(bsz, nc),
        in_specs=[slab_rows(0), slab_rows(1), slab_rows(2),
                  pl.BlockSpec((ln, LANES), rows(0)),
                  pl.BlockSpec((None, 2 * n_heads, ln), lambda b, c: (b * nc + c, 0, 0)),
                  per_seq((n_heads, dqk, dv)), per_seq((n_heads, dqk)), per_seq((1, n_heads)),
                  pl.BlockSpec((n_heads, dv), lambda b, c: (0, 0))],
        out_specs=[pl.BlockSpec((ln, wv), rows(0)),
                   per_seq((n_heads, dqk, dv)), per_seq((n_heads, dqk)), per_seq((1, n_heads))],
        out_shape=[jax.ShapeDtypeStruct((bsz * s, wv), BF16),
                   jax.ShapeDtypeStruct((bsz, n_heads, dqk, dv), F32),
                   jax.ShapeDtypeStruct((bsz, n_heads, dqk), F32),
                   jax.ShapeDtypeStruct((bsz, 1, n_heads), F32)],
        scratch_shapes=[pltpu.VMEM((n_heads, dqk, dv), F32), pltpu.VMEM((n_heads, dqk), F32),
                        pltpu.VMEM((1, n_heads), F32)],
        compiler_params=_params(("arbitrary", "arbitrary")),
        name="mlstm",
    )(proj, proj, proj, g, gt3, c0, n0, m0.reshape(bsz, 1, n_heads), norm_w)
    hm, c_new, n_new, m_new = out
    return hm, c_new, n_new, m_new.reshape(bsz, n_heads)


def _outproj_kernel(co_ref, hm_ref, w_ref, o_ref):
    kc = co_ref.shape[1]
    o_ref[...] = jnp.dot(co_ref[...], w_ref[0:kc, :], preferred_element_type=F32) \
        + jnp.dot(hm_ref[...], w_ref[kc:2 * kc, :], preferred_element_type=F32)


def _outproj(co, hm, w_bf):
    m, kc = co.shape
    d = w_bf.shape[-1]
    assert hm.shape[1] == kc and w_bf.shape[0] == 2 * kc
    tn = _pick(d, (2048, 1024, 512, 256, 128))
    tm = _pick(m, (512, 256, 128, 64, 32, 16, 8))
    return pl.pallas_call(
        _outproj_kernel,
        grid=(d // tn, m // tm),
        in_specs=[pl.BlockSpec((tm, kc), lambda j, i: (i, 0)),
                  pl.BlockSpec((tm, kc), lambda j, i: (i, 0)),
                  pl.BlockSpec((2 * kc, tn), lambda j, i: (0, j))],
        out_specs=pl.BlockSpec((tm, tn), lambda j, i: (i, j)),
        out_shape=jax.ShapeDtypeStruct((m, d), F32),
        compiler_params=_params(("parallel", "parallel")),
        name="outproj",
    )(co, hm, w_bf)


def _router_kernel(mix_ref, x_ref, g1_ref, sc2_ref, sh2_ref, n1_ref, n2_ref, wr_ref, br_ref,
                   x1_ref, h2_ref, ti_ref, tg_ref, *, st):
    mix = mix_ref[...]
    mn = mix * lax.rsqrt(jnp.mean(mix * mix, -1, keepdims=True) + EPS) * n1_ref[...]
    x1 = x_ref[...] + _mod_rows(g1_ref, st) * mn
    x1_ref[...] = x1
    y2 = x1 * lax.rsqrt(jnp.mean(x1 * x1, -1, keepdims=True) + EPS) * n2_ref[...]
    h2 = y2 * (1.0 + _mod_rows(sc2_ref, st)) + _mod_rows(sh2_ref, st)
    half = h2.shape[1] // 2
    h2_ref[...] = _pack_pair(h2[:, :half], h2[:, half:])
    logits = jnp.dot(h2.astype(BF16), wr_ref[...], preferred_element_type=F32) + br_ref[...]
    n_exp = logits.shape[1]
    lane = lax.broadcasted_iota(I32, logits.shape, 1)
    lane_o = lax.broadcasted_iota(I32, ti_ref.shape, 1)
    idx_out = jnp.zeros(ti_ref.shape, I32)
    val_out = jnp.zeros(tg_ref.shape, F32)
    top = None
    den = jnp.zeros((logits.shape[0], 1), F32)
    for r in range(TOP_K):
        mx = jnp.max(logits, axis=1, keepdims=True)
        ix = jnp.min(jnp.where(logits == mx, lane, n_exp), axis=1, keepdims=True)
        if top is None:
            top = mx
        e = jnp.exp(mx - top)
        den = den + e
        idx_out = jnp.where(lane_o == r, ix, idx_out)
        val_out = jnp.where(lane_o == r, e, val_out)
        logits = jnp.where(lane == ix, NEG_BIG, logits)
    ti_ref[...] = idx_out
    tg_ref[...] = val_out / den


def _router(mix, x2, mod3, grp, n1, n2, wr, br):
    rows, d = x2.shape
    n_exp = wr.shape[1]
    kern = functools.partial(_router_kernel, st=grp.st)
    return pl.pallas_call(
        kern,
        grid=grp.grid,
        in_specs=[grp.rows(d), grp.rows(d), grp.mod(d, 2), grp.mod(d, 4), grp.mod(d, 3),
                  grp.const((1, d)), grp.const((1, d)), grp.const((d, n_exp)), grp.const((1, n_exp))],
        out_specs=[grp.rows(d), grp.rows(d // 2), grp.rows(LANES), grp.rows(LANES)],
        out_shape=[jax.ShapeDtypeStruct((rows, d), F32), jax.ShapeDtypeStruct((rows, d // 2), U32),
                   jax.ShapeDtypeStruct((rows, LANES), I32), jax.ShapeDtypeStruct((rows, LANES), F32)],
        compiler_params=_params(("parallel", "parallel")),
        name="router",
    )(mix, x2, mod3, mod3, mod3, n1, n2, wr, br)


def _rank_kernel(ti_ref, ps_ref, o_ref, carry_ref):
    @pl.when(pl.program_id(0) == 0)
    def _():
        carry_ref[...] = jnp.zeros(carry_ref.shape, F32)

    ti = ti_ref[...].astype(F32)
    tt = ti.shape[0]
    lane = lax.broadcasted_iota(I32, ti.shape, 1)
    lane_f = lane.astype(F32)
    cols = []
    member = jnp.zeros(ti.shape, F32)
    for kk in range(TOP_K):
        ek = jnp.sum(jnp.where(lane == kk, ti, 0.0), axis=1, keepdims=True)
        cols.append(ek)
        member = member + jnp.where(lane_f == ek, 1.0, 0.0)
    r = lax.broadcasted_iota(I32, (tt, tt), 0)
    c = lax.broadcasted_iota(I32, (tt, tt), 1)
    before = jnp.where(r > c, 1.0, 0.0).astype(BF16)
    base = jnp.dot(before, member.astype(BF16), preferred_element_type=F32) + carry_ref[...] + ps_ref[...]
    out = jnp.zeros(o_ref.shape, I32)
    for kk in range(TOP_K):
        dk = jnp.sum(jnp.where(lane_f == cols[kk], base, 0.0), axis=1, keepdims=True)
        out = jnp.where(lane == kk, dk.astype(I32), out)
    o_ref[...] = out
    carry_ref[...] += jnp.sum(member, axis=0, keepdims=True)


def _rank(top_i, pad_start):
    t = top_i.shape[0]
    tt = _pick(t, (512, 256, 128, 64, 32, 16, 8))
    return pl.pallas_call(
        _rank_kernel,
        grid=(t // tt,),
        in_specs=[pl.BlockSpec((tt, LANES), lambda i: (i, 0)), pl.BlockSpec((1, LANES), lambda i: (0, 0))],
        out_specs=pl.BlockSpec((tt, LANES), lambda i: (i, 0)),
        out_shape=jax.ShapeDtypeStruct((t, LANES), I32),
        scratch_shapes=[pltpu.VMEM((1, LANES), F32)],
        compiler_params=_params(("arbitrary",)),
        name="moe_rank",
    )(top_i, pad_start)


def _route(top_i, n_experts, tm):
    t = top_i.shape[0]
    a = t * TOP_K
    flat_e = top_i[:, :TOP_K].reshape(a)
    counts = jnp.sum((flat_e[:, None] == jnp.arange(n_experts, dtype=I32)[None, :]).astype(I32), axis=0)
    padded = (counts + tm - 1) // tm * tm
    pad_end = jnp.cumsum(padded)
    pad_start = pad_end - padded
    nt = a // tm + n_experts
    tile_start = jnp.arange(nt, dtype=I32) * tm
    tile_u = (tile_start < pad_end[-1]).astype(I32)
    n_used = jnp.sum(tile_u)
    tile_e = jnp.sum((pad_end[None, :] <= tile_start[:, None]).astype(I32), axis=1)
    tile_e = jnp.minimum(tile_e, n_experts - 1)
    tile_e = jnp.where(tile_u == 1, tile_e, tile_e[jnp.maximum(n_used - 1, 0)])
    idx = jnp.arange(nt, dtype=I32)
    starts = jnp.logical_and(tile_u == 1, jnp.logical_or(idx == 0, tile_e != jnp.roll(tile_e, 1)))
    seg = (jnp.cumsum(starts.astype(I32)) - 1).astype(I32)
    later = lax.cummin(jnp.where(starts, idx, nt)[::-1])[::-1]
    nxt = jnp.concatenate([later[1:], jnp.full((1,), nt, I32)])
    next_e = tile_e[jnp.where(nxt >= nt, 0, nxt)]
    n_seg = jnp.sum(starts.astype(I32)).reshape(1)
    x_tile = jnp.minimum(idx, jnp.maximum(n_used - 1, 0))
    tiles = (tile_e, tile_u, seg, next_e, n_seg, x_tile)
    ps = jnp.zeros((1, LANES), F32).at[0, :n_experts].set(pad_start.astype(F32))
    dest = _rank(top_i, ps)
    return dest, tiles, nt


def _row_copy(src_ref, dst_ref, sem, src_row, dst_row):
    return pltpu.make_async_copy(src_ref.at[pl.ds(src_row, 1), :], dst_ref.at[pl.ds(dst_row, 1), :], sem)


def _dispatch_kernel(dest_ref, src_ref, xs_in, xs_out, sem):
    del xs_in
    tt = src_ref.shape[0]

    def start(r, carry):
        for kk in range(TOP_K):
            _row_copy(src_ref, xs_out, sem, r, dest_ref[0, r * TOP_K + kk]).start()
        return carry

    lax.fori_loop(0, tt, start, 0, unroll=ISSUE_UNROLL)
    for _ in range(TOP_K):
        pltpu.make_async_copy(src_ref, src_ref, sem).wait()


def _dispatch(h2p, dest, xs):
    t, w = h2p.shape
    tt = _pick(t, (256, 128, 64, 32, 16, 8))
    nt = t // tt
    return pl.pallas_call(
        _dispatch_kernel,
        grid=(nt,),
        in_specs=[pl.BlockSpec((None, 1, tt * TOP_K), lambda i: (i, 0, 0), memory_space=pltpu.SMEM),
                  pl.BlockSpec((tt, w), lambda i: (i, 0)),
                  pl.BlockSpec(memory_space=pl.ANY)],
        out_specs=pl.BlockSpec(memory_space=pl.ANY),
        out_shape=jax.ShapeDtypeStruct(xs.shape, xs.dtype),
        scratch_shapes=[pltpu.SemaphoreType.DMA(())],
        input_output_aliases={2: 0},
        compiler_params=_params(("arbitrary",)),
        name="moe_dispatch",
    )(dest[:, :TOP_K].reshape(nt, 1, tt * TOP_K), h2p, xs)


N_TILE_TABLES = 6


def _segment_weights(tables, w_hbm, wst_ref, sem, *, l, tn, col_offs):
    te_ref, tu_ref, sg_ref, ne_ref, ns_ref = tables[:5]
    j = pl.program_id(0)
    i = pl.program_id(1)
    n_seg = ns_ref[0]
    first = jnp.logical_and(tu_ref[i] == 1,
                            jnp.logical_or(i == 0, te_ref[i] != te_ref[jnp.maximum(i - 1, 0)]))
    g = j * n_seg + sg_ref[i]

    def copies(e, jj):
        return [pltpu.make_async_copy(
            w_hbm.at[l, e, :, pl.ds(pl.multiple_of((off + jj) * tn, tn), tn)], wst_ref.at[m], sem)
            for m, off in enumerate(col_offs)]

    @pl.when(jnp.logical_and(j == 0, i == 0))
    def _():
        for cp in copies(te_ref[0], 0):
            cp.start(priority=WEIGHT_DMA_PRIORITY)

    def wait_weights():
        for cp in copies(te_ref[i], j):
            cp.wait()

    def start_next():
        @pl.when(g + 1 < pl.num_programs(0) * n_seg)
        def _():
            jn = jnp.where(sg_ref[i] + 1 == n_seg, j + 1, j)
            for cp in copies(ne_ref[i], jn):
                cp.start(priority=WEIGHT_DMA_PRIORITY)

    return first, wait_weights, start_next


def _gmm1_kernel(*refs, l, tn, nj):
    tables = refs[:N_TILE_TABLES]
    x_ref, w_hbm, bg_ref, bu_ref, o_ref, wst_ref, wgb_ref, wub_ref, sem = refs[N_TILE_TABLES:]
    used = tables[1][pl.program_id(1)] == 1
    first, wait_weights, start_next = _segment_weights(tables, w_hbm, wst_ref, sem, l=l, tn=tn,
                                                       col_offs=(0, nj))
    half = x_ref.shape[1]

    def tile(wg_lo, wg_hi, wu_lo, wu_hi):
        xlo, xhi = _unpack_pair(x_ref[...])
        gt = jnp.dot(xlo, wg_lo, preferred_element_type=F32) \
            + jnp.dot(xhi, wg_hi, preferred_element_type=F32) + bg_ref[...]
        up = jnp.dot(xlo, wu_lo, preferred_element_type=F32) \
            + jnp.dot(xhi, wu_hi, preferred_element_type=F32) + bu_ref[...]
        gt = jnp.minimum(gt, SWIGLU_LIMIT)
        up = jnp.clip(up, -SWIGLU_LIMIT, SWIGLU_LIMIT)
        act = (up + 1.0) * gt * _sigmoid(SWIGLU_ALPHA * gt)
        o_ref[...] = act.astype(o_ref.dtype)

    @pl.when(first)
    def _():
        wait_weights()
        ws = []
        for m, dst in enumerate((wgb_ref, wub_ref)):
            for k0 in (0, half):
                wk = wst_ref[m, k0:k0 + half, :].astype(BF16)
                dst[k0:k0 + half, :] = wk
                ws.append(wk)
        tile(*ws)
        start_next()

    @pl.when(jnp.logical_and(used, jnp.logical_not(first)))
    def _():
        tile(wgb_ref[0:half, :], wgb_ref[half:2 * half, :], wub_ref[0:half, :], wub_ref[half:2 * half, :])

    @pl.when(jnp.logical_not(used))
    def _():
        o_ref[...] = jnp.zeros(o_ref.shape, o_ref.dtype)


def _gmm1(xs, w1, b1, tiles, l, tm):
    rows, half = xs.shape
    d = 2 * half
    dff = w1.shape[-1] // 2
    nt = rows // tm
    tn = _pick(dff, (1024, 512, 256, 128))
    nj = dff // tn
    b1r = b1.reshape(b1.shape[0], b1.shape[1], 1, 2 * dff)
    kern = functools.partial(_gmm1_kernel, l=l, tn=tn, nj=nj)
    grid_spec = pltpu.PrefetchScalarGridSpec(
        num_scalar_prefetch=N_TILE_TABLES,
        grid=(nj, nt),
        in_specs=[pl.BlockSpec((tm, half), lambda j, i, *t: (t[5][i], 0)),
                  pl.BlockSpec(memory_space=pl.ANY),
                  pl.BlockSpec((None, None, 1, tn), lambda j, i, te, *t: (l, te[i], 0, j)),
                  pl.BlockSpec((None, None, 1, tn), lambda j, i, te, *t: (l, te[i], 0, nj + j))],
        out_specs=pl.BlockSpec((tm, tn), lambda j, i, *t: (i, j)),
        scratch_shapes=[pltpu.VMEM((2, d, tn), F32), pltpu.VMEM((d, tn), BF16), pltpu.VMEM((d, tn), BF16),
                        pltpu.SemaphoreType.DMA(())])
    return pl.pallas_call(
        kern,
        grid_spec=grid_spec,
        out_shape=jax.ShapeDtypeStruct((rows, dff), BF16),
        compiler_params=_params(("arbitrary", "arbitrary"), GMM_VMEM_LIMIT),
        name="moe_gmm1",
    )(*tiles, xs, w1, b1r, b1r)


def _gmm2_kernel(*refs, l, tn):
    tables = refs[:N_TILE_TABLES]
    a_ref, w_hbm, b_ref, o_ref, wst_ref, wb_ref, sem = refs[N_TILE_TABLES:]
    used = tables[1][pl.program_id(1)] == 1
    first, wait_weights, start_next = _segment_weights(tables, w_hbm, wst_ref, sem, l=l, tn=tn, col_offs=(0,))

    def tile(w):
        y = jnp.dot(a_ref[...], w, preferred_element_type=F32) + b_ref[...]
        o_ref[...] = _pack_pair(y[:, :tn // 2], y[:, tn // 2:])

    @pl.when(first)
    def _():
        wait_weights()
        w = wst_ref[0].astype(BF16)
        wb_ref[...] = w
        tile(w)
        start_next()

    @pl.when(jnp.logical_and(used, jnp.logical_not(first)))
    def _():
        tile(wb_ref[...])

    @pl.when(jnp.logical_not(used))
    def _():
        o_ref[...] = jnp.zeros(o_ref.shape, o_ref.dtype)


def _gmm2(act, w2, b2, tiles, l, tm):
    rows, dff = act.shape
    d = w2.shape[-1]
    nt = rows // tm
    tn = _pick(d, (4096, 2048, 1024, 512, 256, 128))
    b2r = b2.reshape(b2.shape[0], b2.shape[1], 1, d)
    kern = functools.partial(_gmm2_kernel, l=l, tn=tn)
    grid_spec = pltpu.PrefetchScalarGridSpec(
        num_scalar_prefetch=N_TILE_TABLES,
        grid=(d // tn, nt),
        in_specs=[pl.BlockSpec((tm, dff), lambda j, i, *t: (i, 0)),
                  pl.BlockSpec(memory_space=pl.ANY),
                  pl.BlockSpec((None, None, 1, tn), lambda j, i, te, *t: (l, te[i], 0, j))],
        out_specs=pl.BlockSpec((tm, tn // 2), lambda j, i, *t: (i, j)),
        scratch_shapes=[pltpu.VMEM((1, dff, tn), F32), pltpu.VMEM((dff, tn), BF16),
                        pltpu.SemaphoreType.DMA(())])
    y = pl.pallas_call(
        kern,
        grid_spec=grid_spec,
        out_shape=jax.ShapeDtypeStruct((rows, d // 2), U32),
        compiler_params=_params(("arbitrary", "arbitrary"), GMM_VMEM_LIMIT),
        name="moe_gmm2",
    )(*tiles, act, w2, b2r)
    return y, tn // 2


def _combine_kernel(pos_ref, posn_ref, y_hbm, tg_ref, x1_ref, g2_ref, nw_ref, o_ref, buf_ref, sem, *, st, pw):
    ni = pl.num_programs(1)
    n = pl.program_id(0) * ni + pl.program_id(1)
    total = pl.num_programs(0) * ni
    tt, d = o_ref.shape
    slot = lax.rem(n, 2)

    def start_rows(idx_ref, sl):
        def body(r, carry):
            for kk in range(TOP_K):
                _row_copy(y_hbm, buf_ref.at[sl, kk], sem.at[sl], idx_ref[0, r * TOP_K + kk], r).start()
            return carry
        lax.fori_loop(0, tt, body, 0, unroll=ISSUE_UNROLL)

    @pl.when(n == 0)
    def _():
        start_rows(pos_ref, 0)

    @pl.when(n + 1 < total)
    def _():
        start_rows(posn_ref, 1 - slot)

    pltpu.make_async_copy(buf_ref.at[slot], buf_ref.at[slot], sem.at[slot]).wait()

    gates = tg_ref[...]
    lane = lax.broadcasted_iota(I32, gates.shape, 1)
    gk = [jnp.sum(jnp.where(lane == kk, gates, 0.0), axis=1, keepdims=True) for kk in range(TOP_K)]
    ssq = jnp.zeros((tt, 1), F32)
    for blk in range(d // (2 * pw)):
        lo = jnp.zeros((tt, pw), F32)
        hi = jnp.zeros((tt, pw), F32)
        for kk in range(TOP_K):
            p = buf_ref[slot, kk, :, blk * pw:(blk + 1) * pw]
            lo = lo + gk[kk] * lax.bitcast_convert_type(p << 16, F32)
            hi = hi + gk[kk] * lax.bitcast_convert_type(p & jnp.uint32(HI_HALF), F32)
        ssq = ssq + jnp.sum(lo * lo, -1, keepdims=True) + jnp.sum(hi * hi, -1, keepdims=True)
        o_ref[:, 2 * pw * blk:2 * pw * blk + pw] = lo
        o_ref[:, 2 * pw * blk + pw:2 * pw * (blk + 1)] = hi
    fn = o_ref[...] * lax.rsqrt(ssq / d + EPS) * nw_ref[...]
    o_ref[...] = x1_ref[...] + _mod_rows(g2_ref, st) * fn


def _combine(y, pw, dest, tg, x1, mod3, grp, nw):
    rows, d = x1.shape
    tt = grp.ts
    nt = rows // tt
    ni = grp.n_inner
    kern = functools.partial(_combine_kernel, st=grp.st, pw=pw)
    dest3 = dest[:, :TOP_K].reshape(nt, 1, tt * TOP_K)
    idx_spec = lambda ahead: pl.BlockSpec(
        (None, 1, tt * TOP_K), lambda o, i: (jnp.minimum(o * ni + i + ahead, nt - 1), 0, 0),
        memory_space=pltpu.SMEM)
    return pl.pallas_call(
        kern,
        grid=grp.grid,
        in_specs=[idx_spec(0), idx_spec(1),
                  pl.BlockSpec(memory_space=pl.ANY),
                  grp.rows(LANES), grp.rows(d), grp.mod(d, 5), grp.const((1, d))],
        out_specs=grp.rows(d),
        out_shape=jax.ShapeDtypeStruct((rows, d), F32),
        scratch_shapes=[pltpu.VMEM((2, TOP_K, tt, d // 2), U32), pltpu.SemaphoreType.DMA((2,))],
        compiler_params=_params(("arbitrary", "arbitrary")),
        name="moe_combine",
    )(dest3, dest3, y, tg, x1, mod3, nw)


def _mixer(x2, mod3, grp, conv_buf, c0, n0, m0, p, l):
    n_heads = c0.shape[1]
    ch = p['conv_w'].shape[-1]
    h, g = _prenorm(x2, mod3, grp, p['norm1_pre'], p['w_gate'], p['b_gate'], n_heads)
    proj = _inproj(h, p['w_in'], ch)
    conv = _conv_seq if grp.nbq == 1 else _conv_step
    co, new_buf = conv(proj, conv_buf, p['conv_w'], p['conv_b'], p['conv_ln_w'], p['conv_ln_b'], grp.bsz, grp.s)
    hm, c_new, n_new, m_new = _mlstm(proj, g, c0, n0, m0, p['mlstm_norm_w'], grp.bsz, grp.s, 2)
    mix = _outproj(co, hm, p['w_out'])
    x1, h2p, ti, tg = _router(mix, x2, mod3, grp, p['norm1_post'], p['norm2_pre'], p['w_router'], p['b_router'])
    return dict(x1=x1, h2p=h2p, ti=ti, tg=tg, state=(new_buf, c_new, n_new, m_new))


def kernel(x_prompt, x_sample, c_prompt, c_sample, state_conv, state_mlstm_C, state_mlstm_n, state_mlstm_m,
           w_ada, b_ada, norm1_pre, w_in, b_gates, conv_w, conv_b, conv_ln_w, conv_ln_b, mlstm_norm_w,
           w_out, norm1_post, norm2_pre, w_router, b_router, w1, b1, w2, b2, norm2_post):
    depth = w_ada.shape[0]
    bp, sp, d = x_prompt.shape
    bs, ss, _ = x_sample.shape
    n_heads, dqk, dv = state_mlstm_C.shape[2:]
    n_experts = w_router.shape[-1]
    nst, ch = state_conv.shape[2:]
    n_gate = 2 * n_heads
    assert n_gate <= LANES and n_experts <= LANES
    tp, tsmp = bp * sp, bs * ss

    mp = -(-(bp + bs) // SUBLANES) * SUBLANES
    c_all = jnp.zeros((mp, d), F32).at[:bs].set(c_sample).at[bs:bs + bp].set(c_prompt)
    grp_p = _Group(bp, sp, bs, ROW_TILE)
    grp_s = _Group(bs, ss, 0, ROW_TILE)

    xp, xs = x_prompt.reshape(tp, d), x_sample.reshape(tsmp, d)
    outs = [[] for _ in range(8)]
    for l in range(depth):
        row = lambda v: v[l].reshape(1, -1)
        p = dict(
            w_in=w_in[l, :, :w_in.shape[-1] - n_gate].astype(BF16), w_out=w_out[l].astype(BF16),
            w_gate=jnp.pad(w_in[l, :, w_in.shape[-1] - n_gate:], ((0, 0), (0, LANES - n_gate))),
            b_gate=jnp.pad(b_gates[l], (0, LANES - n_gate)).reshape(1, LANES),
            norm1_pre=row(norm1_pre), conv_w=conv_w[l], conv_b=row(conv_b), conv_ln_w=row(conv_ln_w),
            conv_ln_b=row(conv_ln_b), mlstm_norm_w=mlstm_norm_w[l], norm1_post=row(norm1_post),
            norm2_pre=row(norm2_pre), w_router=w_router[l].astype(BF16), b_router=row(b_router))
        mod3 = _ada(c_all, w_ada, b_ada, l).reshape(mp, 1, 6 * d)
        zero = lambda shape: jnp.zeros(shape, F32)
        gp = _mixer(xp, mod3, grp_p, zero((bp, nst, ch)), zero((bp, n_heads, dqk, dv)),
                    zero((bp, n_heads, dqk)), zero((bp, n_heads)), p, l)
        gs = _mixer(xs, mod3, grp_s, state_conv[l], state_mlstm_C[l], state_mlstm_n[l],
                    state_mlstm_m[l], p, l)

        top_i = jnp.concatenate([gp['ti'], gs['ti']], axis=0)
        dest, tiles, nt = _route(top_i, n_experts, MOE_ROWS)
        xsorted = jnp.zeros((nt * MOE_ROWS, d // 2), U32)
        xsorted = _dispatch(gp['h2p'], dest[:tp], xsorted)
        xsorted = _dispatch(gs['h2p'], dest[tp:], xsorted)
        act = _gmm1(xsorted, w1, b1, tiles, l, MOE_ROWS)
        y, pw = _gmm2(act, w2, b2, tiles, l, MOE_ROWS)
        nw2 = row(norm2_post)
        xp = _combine(y, pw, dest[:tp], gp['tg'], gp['x1'], mod3, grp_p, nw2)
        xs = _combine(y, pw, dest[tp:], gs['tg'], gs['x1'], mod3, grp_s, nw2)
        for o, v in zip(outs, gp['state'] + gs['state']):
            o.append(v)
    stack = (lambda o: o[0][None]) if depth == 1 else jnp.stack
    return (xp.reshape(bp, sp, d), xs.reshape(bs, ss, d)) + tuple(stack(o) for o in outs)
```

```python
import functools

import jax
import jax.numpy as jnp
from jax import lax
from jax.experimental import pallas as pl
from jax.experimental.pallas import tpu as pltpu

F32 = jnp.float32
BF16 = jnp.bfloat16
I32 = jnp.int32
U32 = jnp.uint32

EPS = 1e-6
GATE_CAP = 15.0
TOP_K = 4
SWIGLU_LIMIT = 7.0
SWIGLU_ALPHA = 1.702
NEG_BIG = -1e30

LANES = 128
SUBLANES = 8
VMEM_LIMIT = 56 * 1024 * 1024
GMM_VMEM_LIMIT = 60 * 1024 * 1024
MLSTM_CHUNK = 256
MLSTM_SEQS = 4
MOE_ROWS = 256
ROW_TILE = 256
HI_HALF = 0xFFFF0000
ISSUE_UNROLL = 4
WEIGHT_DMA_PRIORITY = 1


def _params(sem, vmem=VMEM_LIMIT):
    return pltpu.CompilerParams(dimension_semantics=sem, vmem_limit_bytes=vmem)


def _sigmoid(x):
    return 1.0 / (1.0 + jnp.exp(-x))


def _pick(n, prefs):
    for p in prefs:
        if n % p == 0:
            return p
    return n


def _pack_pair(lo, hi):
    lo_b = lax.bitcast_convert_type(lo.astype(BF16).astype(F32), U32) >> 16
    hi_b = lax.bitcast_convert_type(hi.astype(BF16).astype(F32), U32) & jnp.uint32(HI_HALF)
    return hi_b | lo_b


def _unpack_pair(p):
    lo = lax.bitcast_convert_type(p << 16, F32).astype(BF16)
    hi = lax.bitcast_convert_type(p & jnp.uint32(HI_HALF), F32).astype(BF16)
    return lo, hi


class _Group:
    def __init__(self, bsz, s, mod_row0, tile_rows):
        self.bsz, self.s = bsz, s
        if s >= tile_rows:
            self.nbq, self.st = 1, _pick(s, (tile_rows, 128, 64, 32, 16, 8))
            self.n_outer, self.n_inner = bsz, s // self.st
        else:
            self.nbq, self.st = _pick(bsz, (tile_rows // s, 8, 4, 2, 1)), s
            self.n_outer, self.n_inner = bsz // self.nbq, 1
        assert mod_row0 % self.nbq == 0
        self.mod_blk0 = mod_row0 // self.nbq
        self.ts = self.nbq * self.st
        self.grid = (self.n_outer, self.n_inner)

    def rows(self, width, col=0):
        ni = self.n_inner
        return pl.BlockSpec((self.ts, width), lambda o, i, *_: (o * ni + i, col))

    def mod(self, d, col):
        b0 = self.mod_blk0
        return pl.BlockSpec((self.nbq, 1, d), lambda o, i, *_: (b0 + o, 0, col))

    def const(self, shape):
        nd = len(shape)
        return pl.BlockSpec(shape, lambda o, i, *_: (0,) * nd)


def _mod_rows(m_ref, st):
    m = m_ref[...]
    nbq, _, d = m.shape
    if nbq == 1:
        return m[0]
    return jnp.broadcast_to(m, (nbq, st, d)).reshape(nbq * st, d)


def _ada_kernel(c_ref, w_ref, b_ref, o_ref):
    c = c_ref[...]
    s = (c * _sigmoid(c)).astype(BF16)
    o_ref[...] = jnp.dot(s, w_ref[...].astype(BF16), preferred_element_type=F32) + b_ref[...]


def _ada(c_all, w_ada, b_ada, l):
    mp, d = c_all.shape
    n = w_ada.shape[-1]
    tn = _pick(n, (512, 256, 128))
    return pl.pallas_call(
        _ada_kernel,
        grid=(n // tn,),
        in_specs=[pl.BlockSpec((mp, d), lambda j: (0, 0)),
                  pl.BlockSpec((None, d, tn), lambda j: (l, 0, j)),
                  pl.BlockSpec((None, 1, tn), lambda j: (l, 0, j))],
        out_specs=pl.BlockSpec((mp, tn), lambda j: (0, j)),
        out_shape=jax.ShapeDtypeStruct((mp, n), F32),
        compiler_params=_params(("parallel",)),
        name="ada",
    )(c_all, w_ada, b_ada.reshape(b_ada.shape[0], 1, n))


def _prenorm_kernel(x_ref, sc_ref, sh_ref, nw_ref, wg_ref, bg_ref, h_ref, g_ref, *, n_heads, st):
    x = x_ref[...]
    y = x * lax.rsqrt(jnp.mean(x * x, -1, keepdims=True) + EPS) * nw_ref[...]
    h = y * (1.0 + _mod_rows(sc_ref, st)) + _mod_rows(sh_ref, st)
    hb = h.astype(BF16)
    h_ref[...] = hb
    z = jnp.dot(hb, wg_ref[...].astype(BF16), preferred_element_type=F32) + bg_ref[...]
    cap = GATE_CAP * jnp.tanh(z / GATE_CAP)
    logsig = jnp.minimum(cap, 0.0) - jnp.log(1.0 + jnp.exp(-jnp.abs(cap)))
    lane = lax.broadcasted_iota(I32, z.shape, 1)
    g_ref[...] = jnp.where(lane < n_heads, cap, logsig)


def _prenorm(x2, mod3, grp, nw, wg, bg, n_heads):
    rows, d = x2.shape
    kern = functools.partial(_prenorm_kernel, n_heads=n_heads, st=grp.st)
    return pl.pallas_call(
        kern,
        grid=grp.grid,
        in_specs=[grp.rows(d), grp.mod(d, 1), grp.mod(d, 0), grp.const((1, d)),
                  grp.const((d, LANES)), grp.const((1, LANES))],
        out_specs=[grp.rows(d), grp.rows(LANES)],
        out_shape=[jax.ShapeDtypeStruct((rows, d), BF16), jax.ShapeDtypeStruct((rows, LANES), F32)],
        compiler_params=_params(("parallel", "parallel")),
        name="prenorm",
    )(x2, mod3, mod3, nw, wg, bg)


def _inproj_kernel(a_ref, w_ref, o_ref):
    o_ref[...] = jnp.dot(a_ref[...], w_ref[...], preferred_element_type=F32)


def _inproj(h, w_bf, slab):
    m, d = h.shape
    n = w_bf.shape[1]
    assert n % slab == 0
    tm = _pick(m, (512, 256, 128, 64, 32, 16, 8))
    return pl.pallas_call(
        _inproj_kernel,
        grid=(n // slab, m // tm),
        in_specs=[pl.BlockSpec((tm, d), lambda j, i: (i, 0)),
                  pl.BlockSpec((d, slab), lambda j, i: (0, j))],
        out_specs=pl.BlockSpec((None, tm, slab), lambda j, i: (j, i, 0)),
        out_shape=jax.ShapeDtypeStruct((n // slab, m, slab), F32),
        compiler_params=_params(("parallel", "parallel")),
        name="inproj",
    )(h, w_bf)


CONV_HALO = 32


def _conv_taps(f_ref, w_ref, base, rows, c0, cc, width):
    acc = jnp.zeros((rows, cc), F32)
    for ph in range(SUBLANES):
        if ph >= width:
            break
        n_al = (width - 1 - ph) // SUBLANES + 1
        gb = f_ref[base + ph: base + ph + rows + SUBLANES * (n_al - 1), c0:c0 + cc]
        for a in range(n_al):
            j = SUBLANES * a + ph
            acc = acc + w_ref[j:j + 1, c0:c0 + cc] * gb[SUBLANES * a:SUBLANES * a + rows]
    return acc


def _ln_swish(y, lw, lb):
    mu = jnp.mean(y, -1, keepdims=True)
    yc = y - mu
    yn = yc * lax.rsqrt(jnp.mean(yc * yc, -1, keepdims=True) + EPS) * lw + lb
    return yn * _sigmoid(yn)


def _conv_taps_strided(f_ref, w_ref, cb_ref, y_ref, off, ts, c, width):
    nseg = ts // SUBLANES
    lanes = slice(c * LANES, (c + 1) * LANES)
    wv = [jnp.broadcast_to(w_ref[j:j + 1, lanes], (SUBLANES, LANES)) for j in range(width)]
    acc = [None] * nseg
    for v in range(nseg + width - 1):
        yv = f_ref[c, pl.ds(v + off, SUBLANES, stride=nseg), :]
        for u in range(max(0, v - (width - 1)), min(nseg - 1, v) + 1):
            term = wv[v - u] * yv
            acc[u] = term if acc[u] is None else acc[u] + term
    cb = cb_ref[:, lanes]
    for u in range(nseg):
        y_ref[c, pl.ds(u, SUBLANES, stride=nseg), :] = acc[u] + cb


def _conv_seq_kernel(av_ref, ag_ref, st_ref, w_ref, cb_ref, lw_ref, lb_ref, o_ref, ns_ref, f_ref, y_ref,
                     *, ts, width):
    i = pl.program_id(1)
    off = CONV_HALO - (width - 1)
    nchunk = f_ref.shape[0]
    chunks = [slice(c * LANES, (c + 1) * LANES) for c in range(nchunk)]

    @pl.when(i == 0)
    def _():
        for c, lanes in enumerate(chunks):
            f_ref[c, 0:off, :] = jnp.zeros((off, LANES), F32)
            f_ref[c, off:CONV_HALO, :] = st_ref[:, lanes]

    u = av_ref[...] * _sigmoid(ag_ref[...])
    for c, lanes in enumerate(chunks):
        f_ref[c, CONV_HALO:CONV_HALO + ts, :] = u[:, lanes]
    for c in range(nchunk):
        _conv_taps_strided(f_ref, w_ref, cb_ref, y_ref, off, ts, c, width)
    y = jnp.concatenate([y_ref[c] for c in range(nchunk)], axis=1)
    o_ref[...] = _ln_swish(y, lw_ref[...], lb_ref[...]).astype(o_ref.dtype)

    @pl.when(i == pl.num_programs(1) - 1)
    def _():
        for c, lanes in enumerate(chunks):
            ns_ref[:, lanes] = f_ref[c, ts + off:ts + CONV_HALO, :]

    for c in range(nchunk):
        f_ref[c, 0:CONV_HALO, :] = f_ref[c, ts:ts + CONV_HALO, :]


def _conv_seq(proj, state, w, cb, lw, lb, bsz, s):
    width, ch = w.shape
    assert proj.shape[2] == ch
    ts = _pick(s, (128, 64, 32))
    ns = s // ts
    assert ch % LANES == 0 and ts >= CONV_HALO >= width - 1
    kern = functools.partial(_conv_seq_kernel, ts=ts, width=width)
    vec = pl.BlockSpec((1, ch), lambda b, i: (0, 0))
    return pl.pallas_call(
        kern,
        grid=(bsz, ns),
        in_specs=[pl.BlockSpec((None, ts, ch), lambda b, i: (0, b * ns + i, 0)),
                  pl.BlockSpec((None, ts, ch), lambda b, i: (1, b * ns + i, 0)),
                  pl.BlockSpec((None, width - 1, ch), lambda b, i: (b, 0, 0)),
                  pl.BlockSpec((width, ch), lambda b, i: (0, 0)),
                  vec, vec, vec],
        out_specs=[pl.BlockSpec((ts, ch), lambda b, i: (b * ns + i, 0)),
                   pl.BlockSpec((None, width - 1, ch), lambda b, i: (b, 0, 0))],
        out_shape=[jax.ShapeDtypeStruct((bsz * s, ch), BF16),
                   jax.ShapeDtypeStruct((bsz, width - 1, ch), F32)],
        scratch_shapes=[pltpu.VMEM((ch // LANES, CONV_HALO + ts, LANES), F32),
                        pltpu.VMEM((ch // LANES, ts, LANES), F32)],
        compiler_params=_params(("arbitrary", "arbitrary")),
        name="conv_seq",
    )(proj, proj, state, w, cb, lw, lb)


def _conv_step_kernel(av_ref, ag_ref, st_ref, w_ref, cb_ref, lw_ref, lb_ref, o_ref, ns_ref, f_ref, y_ref,
                      *, nb, s, width, cc):
    nst = width - 1
    ch = f_ref.shape[1]

    def body(q, carry):
        r = pl.multiple_of(q * s, s)
        f_ref[0:nst, :] = st_ref[q]
        f_ref[nst:nst + s, :] = av_ref[pl.ds(r, s), :] * _sigmoid(ag_ref[pl.ds(r, s), :])
        for c0 in range(0, ch, cc):
            acc = _conv_taps(f_ref, w_ref, 0, s, c0, cc, width)
            y_ref[pl.ds(r, s), c0:c0 + cc] = acc + cb_ref[:, c0:c0 + cc]
        ns_ref[q] = f_ref[s:s + nst, :]
        return carry

    lax.fori_loop(0, nb, body, 0)
    o_ref[...] = _ln_swish(y_ref[...], lw_ref[...], lb_ref[...]).astype(o_ref.dtype)


def _conv_step(proj, state, w, cb, lw, lb, bsz, s):
    width, ch = w.shape
    assert proj.shape[2] == ch
    assert s % SUBLANES == 0
    nb = _pick(bsz, (16, 8, 4, 2, 1))
    cc = _pick(ch, (512, 256, 128))
    kern = functools.partial(_conv_step_kernel, nb=nb, s=s, width=width, cc=cc)
    vec = pl.BlockSpec((1, ch), lambda b: (0, 0))
    frows = -(-(width - 1 + s) // SUBLANES) * SUBLANES
    return pl.pallas_call(
        kern,
        grid=(bsz // nb,),
        in_specs=[pl.BlockSpec((None, nb * s, ch), lambda b: (0, b, 0)),
                  pl.BlockSpec((None, nb * s, ch), lambda b: (1, b, 0)),
                  pl.BlockSpec((nb, width - 1, ch), lambda b: (b, 0, 0)),
                  pl.BlockSpec((width, ch), lambda b: (0, 0)),
                  vec, vec, vec],
        out_specs=[pl.BlockSpec((nb * s, ch), lambda b: (b, 0)),
                   pl.BlockSpec((nb, width - 1, ch), lambda b: (b, 0, 0))],
        out_shape=[jax.ShapeDtypeStruct((bsz * s, ch), BF16),
                   jax.ShapeDtypeStruct((bsz, width - 1, ch), F32)],
        scratch_shapes=[pltpu.VMEM((frows, ch), F32), pltpu.VMEM((nb * s, ch), F32)],
        compiler_params=_params(("parallel",)),
        name="conv_step",
    )(proj, proj, state, w, cb, lw, lb)


def _mlstm_kernel(qk_ref, v_ref, o_ref, g_ref, gt_ref, c0_ref, n0_ref, m0_ref, nw_ref,
                  hm_ref, c_out, n_out, m_out, c_s, n_s, m_s, *, n_heads, scale, nb):
    c = pl.program_id(1)
    last = pl.num_programs(1) - 1
    ln = qk_ref.shape[0] // nb
    wqk = qk_ref.shape[1] // 2
    dqk = wqk // n_heads
    dv = v_ref.shape[1] // n_heads

    @pl.when(c == 0)
    def _():
        c_s[...] = c0_ref[...]
        n_s[...] = n0_ref[...]
        m_s[...] = m0_ref[...]

    row = lax.broadcasted_iota(I32, (ln, ln), 0)
    col = lax.broadcasted_iota(I32, (ln, ln), 1)
    tri = row >= col
    tri_t = row <= col

    def chain(bi, hd):
        rows = slice(bi * ln, (bi + 1) * ln)
        g = g_ref[rows, :]
        gt = gt_ref[bi]
        li_c = g[:, hd:hd + 1]
        lf_c = g[:, n_heads + hd:n_heads + hd + 1]
        li_r = gt[hd:hd + 1, :]
        lf_r = gt[n_heads + hd:n_heads + hd + 1, :]
        b_c = jnp.sum(jnp.where(tri, lf_r, 0.0), axis=1, keepdims=True)
        b_r = jnp.sum(jnp.where(tri_t, lf_c, 0.0), axis=0, keepdims=True)
        b_l = jnp.sum(lf_r, axis=1, keepdims=True)
        m_prev = m_s[bi, :, hd:hd + 1]

        dmat = jnp.where(tri, b_c - b_r + li_r, NEG_BIG)
        inter = b_c + m_prev
        m_t = jnp.maximum(inter, jnp.max(dmat, axis=1, keepdims=True))
        a = jnp.exp(inter - m_t)

        q = qk_ref[rows, hd * dqk:(hd + 1) * dqk]
        k = qk_ref[rows, wqk + hd * dqk:wqk + (hd + 1) * dqk] * scale
        qb = q.astype(BF16)
        vb = v_ref[rows, hd * dv:(hd + 1) * dv].astype(BF16)
        cst = c_s[bi, hd]
        nst = n_s[bi, hd:hd + 1, :]
        s = lax.dot_general(qb, k.astype(BF16), (((1,), (1,)), ((), ())), preferred_element_type=F32)
        s = s * jnp.exp(dmat - m_t)
        num = a * jnp.dot(qb, cst.astype(BF16), preferred_element_type=F32) \
            + jnp.dot(s.astype(BF16), vb, preferred_element_type=F32)
        den = a * jnp.sum(q * nst, axis=1, keepdims=True) + jnp.sum(s, axis=1, keepdims=True)
        h = num / jnp.maximum(jnp.abs(den), jnp.exp(-m_t))
        hn = h * lax.rsqrt(jnp.mean(h * h, -1, keepdims=True) + EPS) * nw_ref[hd:hd + 1, :]
        gate = _sigmoid(o_ref[rows, hd * dv:(hd + 1) * dv])
        hm_ref[rows, hd * dv:(hd + 1) * dv] = (hn * gate).astype(hm_ref.dtype)

        g_r = b_l - b_r + li_r
        g_c = b_l - b_c + li_c
        m_new = jnp.maximum(b_l + m_prev, jnp.max(g_r, axis=1, keepdims=True))
        decay = jnp.exp(b_l + m_prev - m_new)
        kw = k * jnp.exp(g_c - m_new)
        c_new = decay * cst + lax.dot_general(kw.astype(BF16), vb, (((0,), (0,)), ((), ())),
                                              preferred_element_type=F32)
        n_new = decay * nst + jnp.sum(kw, axis=0, keepdims=True)
        c_s[bi, hd] = c_new
        n_s[bi, hd:hd + 1, :] = n_new
        m_s[bi, :, hd:hd + 1] = m_new

    for bi in range(nb):
        for hd in range(n_heads):
            chain(bi, hd)

    @pl.when(c == last)
    def _():
        c_out[...] = c_s[...]
        n_out[...] = n_s[...]
        m_out[...] = m_s[...]


def _mlstm(proj, g, c0, n0, m0, norm_w, bsz, s, slab0):
    _, n_heads, dqk, dv = c0.shape
    ln = s if s <= MLSTM_CHUNK else MLSTM_CHUNK
    assert s % ln == 0 and ln % SUBLANES == 0
    nc = s // ln
    wqk, wv = n_heads * dqk, n_heads * dv
    slab = proj.shape[2]
    assert 2 * wqk == slab and wv == slab
    gt3 = g[:, :2 * n_heads].reshape(bsz * nc, ln, 2 * n_heads).transpose(0, 2, 1)
    nb = _pick(bsz, (MLSTM_SEQS, 2, 1)) if nc == 1 else 1
    kern = functools.partial(_mlstm_kernel, n_heads=n_heads, scale=dqk ** -0.5, nb=nb)
    rows = lambda col: (lambda b, c: (b * nc + c, col))
    slab_rows = lambda k: pl.BlockSpec((None, nb * ln, slab), lambda b, c: (slab0 + k, b * nc + c, 0))
    per_seq = lambda shape: pl.BlockSpec((nb,) + shape, lambda b, c: (b,) + (0,) * len(shape))
    out = pl.pallas_call(
        kern,
        grid=(bsz // nb, nc),
        in_specs=[slab_rows(0), slab_rows(1), slab_rows(2),
                  pl.BlockSpec((nb * ln, LANES), rows(0)),
                  pl.BlockSpec((nb, 2 * n_heads, ln), lambda b, c: (b * nc + c, 0, 0)),
                  per_seq((n_heads, dqk, dv)), per_seq((n_heads, dqk)), per_seq((1, n_heads)),
                  pl.BlockSpec((n_heads, dv), lambda b, c: (0, 0))],
        out_specs=[pl.BlockSpec((nb * ln, wv), rows(0)),
                   per_seq((n_heads, dqk, dv)), per_seq((n_heads, dqk)), per_seq((1, n_heads))],
        out_shape=[jax.ShapeDtypeStruct((bsz * s, wv), BF16),
                   jax.ShapeDtypeStruct((bsz, n_heads, dqk, dv), F32),
                   jax.ShapeDtypeStruct((bsz, n_heads, dqk), F32),
                   jax.ShapeDtypeStruct((bsz, 1, n_heads), F32)],
        scratch_shapes=[pltpu.VMEM((nb, n_heads, dqk, dv), F32), pltpu.VMEM((nb, n_heads, dqk), F32),
                        pltpu.VMEM((nb, 1, n_heads), F32)],
        compiler_params=_params(("arbitrary", "arbitrary")),
        name="mlstm",
    )(proj, proj, proj, g, gt3, c0, n0, m0.reshape(bsz, 1, n_heads), norm_w)
    hm, c_new, n_new, m_new = out
    return hm, c_new, n_new, m_new.reshape(bsz, n_heads)


def _outproj_kernel(co_ref, hm_ref, w_ref, o_ref):
    kc = co_ref.shape[1]
    o_ref[...] = jnp.dot(co_ref[...], w_ref[0:kc, :], preferred_element_type=F32) \
        + jnp.dot(hm_ref[...], w_ref[kc:2 * kc, :], preferred_element_type=F32)


def _outproj(co, hm, w_bf):
    m, kc = co.shape
    d = w_bf.shape[-1]
    assert hm.shape[1] == kc and w_bf.shape[0] == 2 * kc
    tn = _pick(d, (2048, 1024, 512, 256, 128))
    tm = _pick(m, (512, 256, 128, 64, 32, 16, 8))
    return pl.pallas_call(
        _outproj_kernel,
        grid=(d // tn, m // tm),
        in_specs=[pl.BlockSpec((tm, kc), lambda j, i: (i, 0)),
                  pl.BlockSpec((tm, kc), lambda j, i: (i, 0)),
                  pl.BlockSpec((2 * kc, tn), lambda j, i: (0, j))],
        out_specs=pl.BlockSpec((tm, tn), lambda j, i: (i, j)),
        out_shape=jax.ShapeDtypeStruct((m, d), F32),
        compiler_params=_params(("parallel", "parallel")),
        name="outproj",
    )(co, hm, w_bf)


def _router_kernel(mix_ref, x_ref, g1_ref, sc2_ref, sh2_ref, n1_ref, n2_ref, wr_ref, br_ref,
                   x1_ref, h2_ref, ti_ref, tg_ref, *, st):
    mix = mix_ref[...]
    mn = mix * lax.rsqrt(jnp.mean(mix * mix, -1, keepdims=True) + EPS) * n1_ref[...]
    x1 = x_ref[...] + _mod_rows(g1_ref, st) * mn
    x1_ref[...] = x1
    y2 = x1 * lax.rsqrt(jnp.mean(x1 * x1, -1, keepdims=True) + EPS) * n2_ref[...]
    h2 = y2 * (1.0 + _mod_rows(sc2_ref, st)) + _mod_rows(sh2_ref, st)
    half = h2.shape[1] // 2
    h2_ref[...] = _pack_pair(h2[:, :half], h2[:, half:])
    logits = jnp.dot(h2.astype(BF16), wr_ref[...], preferred_element_type=F32) + br_ref[...]
    n_exp = logits.shape[1]
    lane = lax.broadcasted_iota(I32, logits.shape, 1)
    lane_o = lax.broadcasted_iota(I32, ti_ref.shape, 1)
    idx_out = jnp.zeros(ti_ref.shape, I32)
    val_out = jnp.zeros(tg_ref.shape, F32)
    top = None
    den = jnp.zeros((logits.shape[0], 1), F32)
    for r in range(TOP_K):
        mx = jnp.max(logits, axis=1, keepdims=True)
        ix = jnp.min(jnp.where(logits == mx, lane, n_exp), axis=1, keepdims=True)
        if top is None:
            top = mx
        e = jnp.exp(mx - top)
        den = den + e
        idx_out = jnp.where(lane_o == r, ix, idx_out)
        val_out = jnp.where(lane_o == r, e, val_out)
        logits = jnp.where(lane == ix, NEG_BIG, logits)
    ti_ref[...] = idx_out
    tg_ref[...] = val_out / den


def _router(mix, x2, mod3, grp, n1, n2, wr, br):
    rows, d = x2.shape
    n_exp = wr.shape[1]
    kern = functools.partial(_router_kernel, st=grp.st)
    return pl.pallas_call(
        kern,
        grid=grp.grid,
        in_specs=[grp.rows(d), grp.rows(d), grp.mod(d, 2), grp.mod(d, 4), grp.mod(d, 3),
                  grp.const((1, d)), grp.const((1, d)), grp.const((d, n_exp)), grp.const((1, n_exp))],
        out_specs=[grp.rows(d), grp.rows(d // 2), grp.rows(LANES), grp.rows(LANES)],
        out_shape=[jax.ShapeDtypeStruct((rows, d), F32), jax.ShapeDtypeStruct((rows, d // 2), U32),
                   jax.ShapeDtypeStruct((rows, LANES), I32), jax.ShapeDtypeStruct((rows, LANES), F32)],
        compiler_params=_params(("parallel", "parallel")),
        name="router",
    )(mix, x2, mod3, mod3, mod3, n1, n2, wr, br)


def _rank_kernel(ti_ref, ps_ref, o_ref, carry_ref):
    @pl.when(pl.program_id(0) == 0)
    def _():
        carry_ref[...] = jnp.zeros(carry_ref.shape, F32)

    ti = ti_ref[...].astype(F32)
    tt = ti.shape[0]
    lane = lax.broadcasted_iota(I32, ti.shape, 1)
    lane_f = lane.astype(F32)
    cols = []
    member = jnp.zeros(ti.shape, F32)
    for kk in range(TOP_K):
        ek = jnp.sum(jnp.where(lane == kk, ti, 0.0), axis=1, keepdims=True)
        cols.append(ek)
        member = member + jnp.where(lane_f == ek, 1.0, 0.0)
    r = lax.broadcasted_iota(I32, (tt, tt), 0)
    c = lax.broadcasted_iota(I32, (tt, tt), 1)
    before = jnp.where(r > c, 1.0, 0.0).astype(BF16)
    base = jnp.dot(before, member.astype(BF16), preferred_element_type=F32) + carry_ref[...] + ps_ref[...]
    out = jnp.zeros(o_ref.shape, I32)
    for kk in range(TOP_K):
        dk = jnp.sum(jnp.where(lane_f == cols[kk], base, 0.0), axis=1, keepdims=True)
        out = jnp.where(lane == kk, dk.astype(I32), out)
    o_ref[...] = out
    carry_ref[...] += jnp.sum(member, axis=0, keepdims=True)


def _rank(top_i, pad_start):
    t = top_i.shape[0]
    tt = _pick(t, (512, 256, 128, 64, 32, 16, 8))
    return pl.pallas_call(
        _rank_kernel,
        grid=(t // tt,),
        in_specs=[pl.BlockSpec((tt, LANES), lambda i: (i, 0)), pl.BlockSpec((1, LANES), lambda i: (0, 0))],
        out_specs=pl.BlockSpec((tt, LANES), lambda i: (i, 0)),
        out_shape=jax.ShapeDtypeStruct((t, LANES), I32),
        scratch_shapes=[pltpu.VMEM((1, LANES), F32)],
        compiler_params=_params(("arbitrary",)),
        name="moe_rank",
    )(top_i, pad_start)


def _route(top_i, n_experts, tm):
    t = top_i.shape[0]
    a = t * TOP_K
    flat_e = top_i[:, :TOP_K].reshape(a)
    counts = jnp.sum((flat_e[:, None] == jnp.arange(n_experts, dtype=I32)[None, :]).astype(I32), axis=0)
    padded = (counts + tm - 1) // tm * tm
    pad_end = jnp.cumsum(padded)
    pad_start = pad_end - padded
    nt = a // tm + n_experts
    tile_start = jnp.arange(nt, dtype=I32) * tm
    tile_u = (tile_start < pad_end[-1]).astype(I32)
    n_used = jnp.sum(tile_u)
    tile_e = jnp.sum((pad_end[None, :] <= tile_start[:, None]).astype(I32), axis=1)
    tile_e = jnp.minimum(tile_e, n_experts - 1)
    tile_e = jnp.where(tile_u == 1, tile_e, tile_e[jnp.maximum(n_used - 1, 0)])
    idx = jnp.arange(nt, dtype=I32)
    starts = jnp.logical_and(tile_u == 1, jnp.logical_or(idx == 0, tile_e != jnp.roll(tile_e, 1)))
    seg = (jnp.cumsum(starts.astype(I32)) - 1).astype(I32)
    later = lax.cummin(jnp.where(starts, idx, nt)[::-1])[::-1]
    nxt = jnp.concatenate([later[1:], jnp.full((1,), nt, I32)])
    next_e = tile_e[jnp.where(nxt >= nt, 0, nxt)]
    n_seg = jnp.sum(starts.astype(I32)).reshape(1)
    x_tile = jnp.minimum(idx, jnp.maximum(n_used - 1, 0))
    tiles = (tile_e, tile_u, seg, next_e, n_seg, x_tile)
    ps = jnp.zeros((1, LANES), F32).at[0, :n_experts].set(pad_start.astype(F32))
    dest = _rank(top_i, ps)
    return dest, tiles, nt


def _row_copy(src_ref, dst_ref, sem, src_row, dst_row):
    return pltpu.make_async_copy(src_ref.at[pl.ds(src_row, 1), :], dst_ref.at[pl.ds(dst_row, 1), :], sem)


def _dispatch_kernel(dest_ref, src_ref, xs_in, xs_out, sem):
    del xs_in
    tt = src_ref.shape[0]

    def start(r, carry):
        for kk in range(TOP_K):
            _row_copy(src_ref, xs_out, sem, r, dest_ref[0, r * TOP_K + kk]).start(priority=kk % 2)
        return carry

    lax.fori_loop(0, tt, start, 0, unroll=ISSUE_UNROLL)
    for _ in range(TOP_K):
        pltpu.make_async_copy(src_ref, src_ref, sem).wait()


def _dispatch(h2p, dest, xs):
    t, w = h2p.shape
    tt = _pick(t, (256, 128, 64, 32, 16, 8))
    nt = t // tt
    return pl.pallas_call(
        _dispatch_kernel,
        grid=(nt,),
        in_specs=[pl.BlockSpec((None, 1, tt * TOP_K), lambda i: (i, 0, 0), memory_space=pltpu.SMEM),
                  pl.BlockSpec((tt, w), lambda i: (i, 0)),
                  pl.BlockSpec(memory_space=pl.ANY)],
        out_specs=pl.BlockSpec(memory_space=pl.ANY),
        out_shape=jax.ShapeDtypeStruct(xs.shape, xs.dtype),
        scratch_shapes=[pltpu.SemaphoreType.DMA(())],
        input_output_aliases={2: 0},
        compiler_params=_params(("arbitrary",)),
        name="moe_dispatch",
    )(dest[:, :TOP_K].reshape(nt, 1, tt * TOP_K), h2p, xs)


N_TILE_TABLES = 6


def _segment_weights(tables, w_hbm, wst_ref, sem, *, l, tn, col_offs):
    te_ref, tu_ref, sg_ref, ne_ref, ns_ref = tables[:5]
    j = pl.program_id(0)
    i = pl.program_id(1)
    n_seg = ns_ref[0]
    first = jnp.logical_and(tu_ref[i] == 1,
                            jnp.logical_or(i == 0, te_ref[i] != te_ref[jnp.maximum(i - 1, 0)]))
    g = j * n_seg + sg_ref[i]

    def copies(e, jj):
        return [pltpu.make_async_copy(
            w_hbm.at[l, e, :, pl.ds(pl.multiple_of((off + jj) * tn, tn), tn)], wst_ref.at[m], sem)
            for m, off in enumerate(col_offs)]

    @pl.when(jnp.logical_and(j == 0, i == 0))
    def _():
        for cp in copies(te_ref[0], 0):
            cp.start(priority=WEIGHT_DMA_PRIORITY)

    def wait_weights():
        for cp in copies(te_ref[i], j):
            cp.wait()

    def start_next():
        @pl.when(g + 1 < pl.num_programs(0) * n_seg)
        def _():
            jn = jnp.where(sg_ref[i] + 1 == n_seg, j + 1, j)
            for cp in copies(ne_ref[i], jn):
                cp.start(priority=WEIGHT_DMA_PRIORITY)

    return first, wait_weights, start_next


def _gmm1_kernel(*refs, l, tn, nj):
    tables = refs[:N_TILE_TABLES]
    x_ref, w_hbm, bg_ref, bu_ref, o_ref, wst_ref, wgb_ref, wub_ref, sem = refs[N_TILE_TABLES:]
    used = tables[1][pl.program_id(1)] == 1
    first, wait_weights, start_next = _segment_weights(tables, w_hbm, wst_ref, sem, l=l, tn=tn,
                                                       col_offs=(0, nj))
    half = x_ref.shape[1]

    def tile(wg_lo, wg_hi, wu_lo, wu_hi):
        xlo, xhi = _unpack_pair(x_ref[...])
        gt = jnp.dot(xlo, wg_lo, preferred_element_type=F32) \
            + jnp.dot(xhi, wg_hi, preferred_element_type=F32) + bg_ref[...]
        up = jnp.dot(xlo, wu_lo, preferred_element_type=F32) \
            + jnp.dot(xhi, wu_hi, preferred_element_type=F32) + bu_ref[...]
        gt = jnp.minimum(gt, SWIGLU_LIMIT)
        up = jnp.clip(up, -SWIGLU_LIMIT, SWIGLU_LIMIT)
        act = (up + 1.0) * gt * _sigmoid(SWIGLU_ALPHA * gt)
        o_ref[...] = act.astype(o_ref.dtype)

    @pl.when(first)
    def _():
        wait_weights()
        ws = []
        for m, dst in enumerate((wgb_ref, wub_ref)):
            for k0 in (0, half):
                wk = wst_ref[m, k0:k0 + half, :].astype(BF16)
                dst[k0:k0 + half, :] = wk
                ws.append(wk)
        tile(*ws)
        start_next()

    @pl.when(jnp.logical_and(used, jnp.logical_not(first)))
    def _():
        tile(wgb_ref[0:half, :], wgb_ref[half:2 * half, :], wub_ref[0:half, :], wub_ref[half:2 * half, :])

    @pl.when(jnp.logical_not(used))
    def _():
        o_ref[...] = jnp.zeros(o_ref.shape, o_ref.dtype)


def _gmm1(xs, w1, b1, tiles, l, tm):
    rows, half = xs.shape
    d = 2 * half
    dff = w1.shape[-1] // 2
    nt = rows // tm
    tn = _pick(dff, (1024, 512, 256, 128))
    nj = dff // tn
    b1r = b1.reshape(b1.shape[0], b1.shape[1], 1, 2 * dff)
    kern = functools.partial(_gmm1_kernel, l=l, tn=tn, nj=nj)
    grid_spec = pltpu.PrefetchScalarGridSpec(
        num_scalar_prefetch=N_TILE_TABLES,
        grid=(nj, nt),
        in_specs=[pl.BlockSpec((tm, half), lambda j, i, *t: (t[5][i], 0)),
                  pl.BlockSpec(memory_space=pl.ANY),
                  pl.BlockSpec((None, None, 1, tn), lambda j, i, te, *t: (l, te[i], 0, j)),
                  pl.BlockSpec((None, None, 1, tn), lambda j, i, te, *t: (l, te[i], 0, nj + j))],
        out_specs=pl.BlockSpec((tm, tn), lambda j, i, *t: (i, j)),
        scratch_shapes=[pltpu.VMEM((2, d, tn), F32), pltpu.VMEM((d, tn), BF16), pltpu.VMEM((d, tn), BF16),
                        pltpu.SemaphoreType.DMA(())])
    return pl.pallas_call(
        kern,
        grid_spec=grid_spec,
        out_shape=jax.ShapeDtypeStruct((rows, dff), BF16),
        compiler_params=_params(("arbitrary", "arbitrary"), GMM_VMEM_LIMIT),
        name="moe_gmm1",
    )(*tiles, xs, w1, b1r, b1r)


def _gmm2_kernel(*refs, l, tn):
    tables = refs[:N_TILE_TABLES]
    a_ref, w_hbm, b_ref, o_ref, wst_ref, wb_ref, sem = refs[N_TILE_TABLES:]
    used = tables[1][pl.program_id(1)] == 1
    first, wait_weights, start_next = _segment_weights(tables, w_hbm, wst_ref, sem, l=l, tn=tn, col_offs=(0,))

    def tile(w):
        y = jnp.dot(a_ref[...], w, preferred_element_type=F32) + b_ref[...]
        o_ref[...] = _pack_pair(y[:, :tn // 2], y[:, tn // 2:])

    @pl.when(first)
    def _():
        wait_weights()
        w = wst_ref[0].astype(BF16)
        wb_ref[...] = w
        tile(w)
        start_next()

    @pl.when(jnp.logical_and(used, jnp.logical_not(first)))
    def _():
        tile(wb_ref[...])

    @pl.when(jnp.logical_not(used))
    def _():
        o_ref[...] = jnp.zeros(o_ref.shape, o_ref.dtype)


def _gmm2(act, w2, b2, tiles, l, tm):
    rows, dff = act.shape
    d = w2.shape[-1]
    nt = rows // tm
    tn = _pick(d, (4096, 2048, 1024, 512, 256, 128))
    b2r = b2.reshape(b2.shape[0], b2.shape[1], 1, d)
    kern = functools.partial(_gmm2_kernel, l=l, tn=tn)
    grid_spec = pltpu.PrefetchScalarGridSpec(
        num_scalar_prefetch=N_TILE_TABLES,
        grid=(d // tn, nt),
        in_specs=[pl.BlockSpec((tm, dff), lambda j, i, *t: (i, 0)),
                  pl.BlockSpec(memory_space=pl.ANY),
                  pl.BlockSpec((None, None, 1, tn), lambda j, i, te, *t: (l, te[i], 0, j))],
        out_specs=pl.BlockSpec((tm, tn // 2), lambda j, i, *t: (i, j)),
        scratch_shapes=[pltpu.VMEM((1, dff, tn), F32), pltpu.VMEM((dff, tn), BF16),
                        pltpu.SemaphoreType.DMA(())])
    y = pl.pallas_call(
        kern,
        grid_spec=grid_spec,
        out_shape=jax.ShapeDtypeStruct((rows, d // 2), U32),
        compiler_params=_params(("arbitrary", "arbitrary"), GMM_VMEM_LIMIT),
        name="moe_gmm2",
    )(*tiles, act, w2, b2r)
    return y, tn // 2


def _combine_kernel(pos_ref, posn_ref, y_hbm, tg_ref, x1_ref, g2_ref, nw_ref, o_ref, buf_ref, sem, *, st, pw):
    ni = pl.num_programs(1)
    n = pl.program_id(0) * ni + pl.program_id(1)
    total = pl.num_programs(0) * ni
    tt, d = o_ref.shape
    slot = lax.rem(n, 2)

    def start_rows(idx_ref, sl):
        def body(r, carry):
            for kk in range(TOP_K):
                cp = _row_copy(y_hbm, buf_ref.at[sl, kk], sem.at[sl], idx_ref[0, r * TOP_K + kk], r)
                cp.start(priority=kk % 2)
            return carry
        lax.fori_loop(0, tt, body, 0, unroll=ISSUE_UNROLL)

    @pl.when(n == 0)
    def _():
        start_rows(pos_ref, 0)

    @pl.when(n + 1 < total)
    def _():
        start_rows(posn_ref, 1 - slot)

    pltpu.make_async_copy(buf_ref.at[slot], buf_ref.at[slot], sem.at[slot]).wait()

    gates = tg_ref[...]
    lane = lax.broadcasted_iota(I32, gates.shape, 1)
    gk = [jnp.sum(jnp.where(lane == kk, gates, 0.0), axis=1, keepdims=True) for kk in range(TOP_K)]
    ssq = jnp.zeros((tt, 1), F32)
    for blk in range(d // (2 * pw)):
        lo = jnp.zeros((tt, pw), F32)
        hi = jnp.zeros((tt, pw), F32)
        for kk in range(TOP_K):
            p = buf_ref[slot, kk, :, blk * pw:(blk + 1) * pw]
            lo = lo + gk[kk] * lax.bitcast_convert_type(p << 16, F32)
            hi = hi + gk[kk] * lax.bitcast_convert_type(p & jnp.uint32(HI_HALF), F32)
        ssq = ssq + jnp.sum(lo * lo, -1, keepdims=True) + jnp.sum(hi * hi, -1, keepdims=True)
        o_ref[:, 2 * pw * blk:2 * pw * blk + pw] = lo
        o_ref[:, 2 * pw * blk + pw:2 * pw * (blk + 1)] = hi
    fn = o_ref[...] * lax.rsqrt(ssq / d + EPS) * nw_ref[...]
    o_ref[...] = x1_ref[...] + _mod_rows(g2_ref, st) * fn


def _combine(y, pw, dest, tg, x1, mod3, grp, nw):
    rows, d = x1.shape
    tt = grp.ts
    nt = rows // tt
    ni = grp.n_inner
    kern = functools.partial(_combine_kernel, st=grp.st, pw=pw)
    dest3 = dest[:, :TOP_K].reshape(nt, 1, tt * TOP_K)
    idx_spec = lambda ahead: pl.BlockSpec(
        (None, 1, tt * TOP_K), lambda o, i: (jnp.minimum(o * ni + i + ahead, nt - 1), 0, 0),
        memory_space=pltpu.SMEM)
    return pl.pallas_call(
        kern,
        grid=grp.grid,
        in_specs=[idx_spec(0), idx_spec(1),
                  pl.BlockSpec(memory_space=pl.ANY),
                  grp.rows(LANES), grp.rows(d), grp.mod(d, 5), grp.const((1, d))],
        out_specs=grp.rows(d),
        out_shape=jax.ShapeDtypeStruct((rows, d), F32),
        scratch_shapes=[pltpu.VMEM((2, TOP_K, tt, d // 2), U32), pltpu.SemaphoreType.DMA((2,))],
        compiler_params=_params(("arbitrary", "arbitrary")),
        name="moe_combine",
    )(dest3, dest3, y, tg, x1, mod3, nw)


def _mixer(x2, mod3, grp, conv_buf, c0, n0, m0, p, l):
    n_heads = c0.shape[1]
    ch = p['conv_w'].shape[-1]
    h, g = _prenorm(x2, mod3, grp, p['norm1_pre'], p['w_gate'], p['b_gate'], n_heads)
    proj = _inproj(h, p['w_in'], ch)
    conv = _conv_seq if grp.nbq == 1 else _conv_step
    co, new_buf = conv(proj, conv_buf, p['conv_w'], p['conv_b'], p['conv_ln_w'], p['conv_ln_b'], grp.bsz, grp.s)
    hm, c_new, n_new, m_new = _mlstm(proj, g, c0, n0, m0, p['mlstm_norm_w'], grp.bsz, grp.s, 2)
    mix = _outproj(co, hm, p['w_out'])
    x1, h2p, ti, tg = _router(mix, x2, mod3, grp, p['norm1_post'], p['norm2_pre'], p['w_router'], p['b_router'])
    return dict(x1=x1, h2p=h2p, ti=ti, tg=tg, state=(new_buf, c_new, n_new, m_new))


def kernel(x_prompt, x_sample, c_prompt, c_sample, state_conv, state_mlstm_C, state_mlstm_n, state_mlstm_m,
           w_ada, b_ada, norm1_pre, w_in, b_gates, conv_w, conv_b, conv_ln_w, conv_ln_b, mlstm_norm_w,
           w_out, norm1_post, norm2_pre, w_router, b_router, w1, b1, w2, b2, norm2_post):
    depth = w_ada.shape[0]
    bp, sp, d = x_prompt.shape
    bs, ss, _ = x_sample.shape
    n_heads, dqk, dv = state_mlstm_C.shape[2:]
    n_experts = w_router.shape[-1]
    nst, ch = state_conv.shape[2:]
    n_gate = 2 * n_heads
    assert n_gate <= LANES and n_experts <= LANES
    tp, tsmp = bp * sp, bs * ss

    mp = -(-(bp + bs) // SUBLANES) * SUBLANES
    c_all = jnp.zeros((mp, d), F32).at[:bs].set(c_sample).at[bs:bs + bp].set(c_prompt)
    grp_p = _Group(bp, sp, bs, ROW_TILE)
    grp_s = _Group(bs, ss, 0, ROW_TILE)

    xp, xs = x_prompt.reshape(tp, d), x_sample.reshape(tsmp, d)
    outs = [[] for _ in range(8)]
    for l in range(depth):
        row = lambda v: v[l].reshape(1, -1)
        p = dict(
            w_in=w_in[l, :, :w_in.shape[-1] - n_gate].astype(BF16), w_out=w_out[l].astype(BF16),
            w_gate=jnp.pad(w_in[l, :, w_in.shape[-1] - n_gate:], ((0, 0), (0, LANES - n_gate))),
            b_gate=jnp.pad(b_gates[l], (0, LANES - n_gate)).reshape(1, LANES),
            norm1_pre=row(norm1_pre), conv_w=conv_w[l], conv_b=row(conv_b), conv_ln_w=row(conv_ln_w),
            conv_ln_b=row(conv_ln_b), mlstm_norm_w=mlstm_norm_w[l], norm1_post=row(norm1_post),
            norm2_pre=row(norm2_pre), w_router=w_router[l].astype(BF16), b_router=row(b_router))
        mod3 = _ada(c_all, w_ada, b_ada, l).reshape(mp, 1, 6 * d)
        zero = lambda shape: jnp.zeros(shape, F32)
        gp = _mixer(xp, mod3, grp_p, zero((bp, nst, ch)), zero((bp, n_heads, dqk, dv)),
                    zero((bp, n_heads, dqk)), zero((bp, n_heads)), p, l)
        gs = _mixer(xs, mod3, grp_s, state_conv[l], state_mlstm_C[l], state_mlstm_n[l],
                    state_mlstm_m[l], p, l)

        top_i = jnp.concatenate([gp['ti'], gs['ti']], axis=0)
        dest, tiles, nt = _route(top_i, n_experts, MOE_ROWS)
        xsorted = jnp.zeros((nt * MOE_ROWS, d // 2), U32)
        xsorted = _dispatch(gp['h2p'], dest[:tp], xsorted)
        xsorted = _dispatch(gs['h2p'], dest[tp:], xsorted)
        act = _gmm1(xsorted, w1, b1, tiles, l, MOE_ROWS)
        y, pw = _gmm2(act, w2, b2, tiles, l, MOE_ROWS)
        nw2 = row(norm2_post)
        xp = _combine(y, pw, dest[:tp], gp['tg'], gp['x1'], mod3, grp_p, nw2)
        xs = _combine(y, pw, dest[tp:], gs['tg'], gs['x1'], mod3, grp_s, nw2)
        for o, v in zip(outs, gp['state'] + gs['state']):
            o.append(v)
    stack = (lambda o: o[0][None]) if depth == 1 else jnp.stack
    return (xp.reshape(bp, sp, d), xs.reshape(bs, ss, d)) + tuple(stack(o) for o in outs)
```

```python
import functools

import jax
import jax.numpy as jnp
from jax import lax
from jax.experimental import pallas as pl
from jax.experimental.pallas import tpu as pltpu

F32 = jnp.float32
BF16 = jnp.bfloat16
I32 = jnp.int32
U32 = jnp.uint32

EPS = 1e-6
GATE_CAP = 15.0
TOP_K = 4
SWIGLU_LIMIT = 7.0
SWIGLU_ALPHA = 1.702
NEG_BIG = -1e30

LANES = 128
SUBLANES = 8
VMEM_LIMIT = 56 * 1024 * 1024
GMM_VMEM_LIMIT = 60 * 1024 * 1024
MLSTM_CHUNK = 256
MLSTM_SEQS = 4
MOE_ROWS = 256
ROW_TILE = 256
HI_HALF = 0xFFFF0000
ISSUE_UNROLL = 4
WEIGHT_DMA_PRIORITY = 1


def _params(sem, vmem=VMEM_LIMIT):
    return pltpu.CompilerParams(dimension_semantics=sem, vmem_limit_bytes=vmem)


def _sigmoid(x):
    return 1.0 / (1.0 + jnp.exp(-x))


def _pick(n, prefs):
    for p in prefs:
        if n % p == 0:
            return p
    return n


def _pack_pair(lo, hi):
    lo_b = lax.bitcast_convert_type(lo.astype(BF16).astype(F32), U32) >> 16
    hi_b = lax.bitcast_convert_type(hi.astype(BF16).astype(F32), U32) & jnp.uint32(HI_HALF)
    return hi_b | lo_b


def _unpack_pair(p):
    lo = lax.bitcast_convert_type(p << 16, F32).astype(BF16)
    hi = lax.bitcast_convert_type(p & jnp.uint32(HI_HALF), F32).astype(BF16)
    return lo, hi


class _Group:
    def __init__(self, bsz, s, mod_row0, tile_rows):
        self.bsz, self.s = bsz, s
        if s >= tile_rows:
            self.nbq, self.st = 1, _pick(s, (tile_rows, 128, 64, 32, 16, 8))
            self.n_outer, self.n_inner = bsz, s // self.st
        else:
            self.nbq, self.st = _pick(bsz, (tile_rows // s, 8, 4, 2, 1)), s
            self.n_outer, self.n_inner = bsz // self.nbq, 1
        assert mod_row0 % self.nbq == 0
        self.mod_blk0 = mod_row0 // self.nbq
        self.ts = self.nbq * self.st
        self.grid = (self.n_outer, self.n_inner)

    def rows(self, width, col=0):
        ni = self.n_inner
        return pl.BlockSpec((self.ts, width), lambda o, i, *_: (o * ni + i, col))

    def mod(self, d, col):
        b0 = self.mod_blk0
        return pl.BlockSpec((self.nbq, 1, d), lambda o, i, *_: (b0 + o, 0, col))

    def const(self, shape):
        nd = len(shape)
        return pl.BlockSpec(shape, lambda o, i, *_: (0,) * nd)


def _mod_rows(m_ref, st):
    m = m_ref[...]
    nbq, _, d = m.shape
    if nbq == 1:
        return m[0]
    return jnp.broadcast_to(m, (nbq, st, d)).reshape(nbq * st, d)


def _ada_kernel(c_ref, w_ref, b_ref, o_ref):
    c = c_ref[...]
    s = (c * _sigmoid(c)).astype(BF16)
    o_ref[...] = jnp.dot(s, w_ref[...].astype(BF16), preferred_element_type=F32) + b_ref[...]


def _ada(c_all, w_ada, b_ada, l):
    mp, d = c_all.shape
    n = w_ada.shape[-1]
    tn = _pick(n, (512, 256, 128))
    return pl.pallas_call(
        _ada_kernel,
        grid=(n // tn,),
        in_specs=[pl.BlockSpec((mp, d), lambda j: (0, 0)),
                  pl.BlockSpec((None, d, tn), lambda j: (l, 0, j)),
                  pl.BlockSpec((None, 1, tn), lambda j: (l, 0, j))],
        out_specs=pl.BlockSpec((mp, tn), lambda j: (0, j)),
        out_shape=jax.ShapeDtypeStruct((mp, n), F32),
        compiler_params=_params(("parallel",)),
        name="ada",
    )(c_all, w_ada, b_ada.reshape(b_ada.shape[0], 1, n))


def _prenorm_kernel(x_ref, sc_ref, sh_ref, nw_ref, wg_ref, bg_ref, h_ref, g_ref, *, n_heads, st):
    x = x_ref[...]
    y = x * lax.rsqrt(jnp.mean(x * x, -1, keepdims=True) + EPS) * nw_ref[...]
    h = y * (1.0 + _mod_rows(sc_ref, st)) + _mod_rows(sh_ref, st)
    hb = h.astype(BF16)
    h_ref[...] = hb
    z = jnp.dot(hb, wg_ref[...].astype(BF16), preferred_element_type=F32) + bg_ref[...]
    cap = GATE_CAP * jnp.tanh(z / GATE_CAP)
    logsig = jnp.minimum(cap, 0.0) - jnp.log(1.0 + jnp.exp(-jnp.abs(cap)))
    lane = lax.broadcasted_iota(I32, z.shape, 1)
    g_ref[...] = jnp.where(lane < n_heads, cap, logsig)


def _prenorm(x2, mod3, grp, nw, wg, bg, n_heads):
    rows, d = x2.shape
    kern = functools.partial(_prenorm_kernel, n_heads=n_heads, st=grp.st)
    return pl.pallas_call(
        kern,
        grid=grp.grid,
        in_specs=[grp.rows(d), grp.mod(d, 1), grp.mod(d, 0), grp.const((1, d)),
                  grp.const((d, LANES)), grp.const((1, LANES))],
        out_specs=[grp.rows(d), grp.rows(LANES)],
        out_shape=[jax.ShapeDtypeStruct((rows, d), BF16), jax.ShapeDtypeStruct((rows, LANES), F32)],
        compiler_params=_params(("parallel", "parallel")),
        name="prenorm",
    )(x2, mod3, mod3, nw, wg, bg)


def _inproj_kernel(a_ref, w_ref, o_ref):
    o_ref[...] = jnp.dot(a_ref[...], w_ref[...], preferred_element_type=F32)


def _inproj(h, w_bf, slab):
    m, d = h.shape
    n = w_bf.shape[1]
    assert n % slab == 0
    tm = _pick(m, (512, 256, 128, 64, 32, 16, 8))
    return pl.pallas_call(
        _inproj_kernel,
        grid=(n // slab, m // tm),
        in_specs=[pl.BlockSpec((tm, d), lambda j, i: (i, 0)),
                  pl.BlockSpec((d, slab), lambda j, i: (0, j))],
        out_specs=pl.BlockSpec((None, tm, slab), lambda j, i: (j, i, 0)),
        out_shape=jax.ShapeDtypeStruct((n // slab, m, slab), F32),
        compiler_params=_params(("parallel", "parallel")),
        name="inproj",
    )(h, w_bf)


CONV_HALO = 32


def _conv_taps(f_ref, w_ref, base, rows, c0, cc, width):
    acc = jnp.zeros((rows, cc), F32)
    for ph in range(SUBLANES):
        if ph >= width:
            break
        n_al = (width - 1 - ph) // SUBLANES + 1
        gb = f_ref[base + ph: base + ph + rows + SUBLANES * (n_al - 1), c0:c0 + cc]
        for a in range(n_al):
            j = SUBLANES * a + ph
            acc = acc + w_ref[j:j + 1, c0:c0 + cc] * gb[SUBLANES * a:SUBLANES * a + rows]
    return acc


def _ln_swish(y, lw, lb):
    mu = jnp.mean(y, -1, keepdims=True)
    yc = y - mu
    yn = yc * lax.rsqrt(jnp.mean(yc * yc, -1, keepdims=True) + EPS) * lw + lb
    return yn * _sigmoid(yn)


def _conv_taps_strided(f_ref, w_ref, cb_ref, y_ref, off, ts, c, width):
    nseg = ts // SUBLANES
    lanes = slice(c * LANES, (c + 1) * LANES)
    wv = [jnp.broadcast_to(w_ref[j:j + 1, lanes], (SUBLANES, LANES)) for j in range(width)]
    acc = [None] * nseg
    for v in range(nseg + width - 1):
        yv = f_ref[c, pl.ds(v + off, SUBLANES, stride=nseg), :]
        for u in range(max(0, v - (width - 1)), min(nseg - 1, v) + 1):
            term = wv[v - u] * yv
            acc[u] = term if acc[u] is None else acc[u] + term
    cb = cb_ref[:, lanes]
    for u in range(nseg):
        y_ref[c, pl.ds(u, SUBLANES, stride=nseg), :] = acc[u] + cb


def _conv_seq_kernel(av_ref, ag_ref, st_ref, w_ref, cb_ref, lw_ref, lb_ref, o_ref, ns_ref, f_ref, y_ref,
                     *, ts, width):
    i = pl.program_id(1)
    off = CONV_HALO - (width - 1)
    nchunk = f_ref.shape[0]
    chunks = [slice(c * LANES, (c + 1) * LANES) for c in range(nchunk)]

    @pl.when(i == 0)
    def _():
        for c, lanes in enumerate(chunks):
            f_ref[c, 0:off, :] = jnp.zeros((off, LANES), F32)
            f_ref[c, off:CONV_HALO, :] = st_ref[:, lanes]

    u = av_ref[...] * _sigmoid(ag_ref[...])
    for c, lanes in enumerate(chunks):
        f_ref[c, CONV_HALO:CONV_HALO + ts, :] = u[:, lanes]
    for c in range(nchunk):
        _conv_taps_strided(f_ref, w_ref, cb_ref, y_ref, off, ts, c, width)
    y = jnp.concatenate([y_ref[c] for c in range(nchunk)], axis=1)
    o_ref[...] = _ln_swish(y, lw_ref[...], lb_ref[...]).astype(o_ref.dtype)

    @pl.when(i == pl.num_programs(1) - 1)
    def _():
        for c, lanes in enumerate(chunks):
            ns_ref[:, lanes] = f_ref[c, ts + off:ts + CONV_HALO, :]

    for c in range(nchunk):
        f_ref[c, 0:CONV_HALO, :] = f_ref[c, ts:ts + CONV_HALO, :]


def _conv_seq(proj, state, w, cb, lw, lb, bsz, s):
    width, ch = w.shape
    assert proj.shape[2] == ch
    ts = _pick(s, (128, 64, 32))
    ns = s // ts
    assert ch % LANES == 0 and ts >= CONV_HALO >= width - 1
    kern = functools.partial(_conv_seq_kernel, ts=ts, width=width)
    vec = pl.BlockSpec((1, ch), lambda b, i: (0, 0))
    return pl.pallas_call(
        kern,
        grid=(bsz, ns),
        in_specs=[pl.BlockSpec((None, ts, ch), lambda b, i: (0, b * ns + i, 0)),
                  pl.BlockSpec((None, ts, ch), lambda b, i: (1, b * ns + i, 0)),
                  pl.BlockSpec((None, width - 1, ch), lambda b, i: (b, 0, 0)),
                  pl.BlockSpec((width, ch), lambda b, i: (0, 0)),
                  vec, vec, vec],
        out_specs=[pl.BlockSpec((ts, ch), lambda b, i: (b * ns + i, 0)),
                   pl.BlockSpec((None, width - 1, ch), lambda b, i: (b, 0, 0))],
        out_shape=[jax.ShapeDtypeStruct((bsz * s, ch), BF16),
                   jax.ShapeDtypeStruct((bsz, width - 1, ch), F32)],
        scratch_shapes=[pltpu.VMEM((ch // LANES, CONV_HALO + ts, LANES), F32),
                        pltpu.VMEM((ch // LANES, ts, LANES), F32)],
        compiler_params=_params(("arbitrary", "arbitrary")),
        name="conv_seq",
    )(proj, proj, state, w, cb, lw, lb)


def _conv_step_kernel(av_ref, ag_ref, st_ref, w_ref, cb_ref, lw_ref, lb_ref, o_ref, ns_ref, f_ref, y_ref,
                      *, nb, s, width, cc):
    nst = width - 1
    ch = f_ref.shape[1]

    def body(q, carry):
        r = pl.multiple_of(q * s, s)
        f_ref[0:nst, :] = st_ref[q]
        f_ref[nst:nst + s, :] = av_ref[pl.ds(r, s), :] * _sigmoid(ag_ref[pl.ds(r, s), :])
        for c0 in range(0, ch, cc):
            acc = _conv_taps(f_ref, w_ref, 0, s, c0, cc, width)
            y_ref[pl.ds(r, s), c0:c0 + cc] = acc + cb_ref[:, c0:c0 + cc]
        ns_ref[q] = f_ref[s:s + nst, :]
        return carry

    lax.fori_loop(0, nb, body, 0)
    o_ref[...] = _ln_swish(y_ref[...], lw_ref[...], lb_ref[...]).astype(o_ref.dtype)


def _conv_step(proj, state, w, cb, lw, lb, bsz, s):
    width, ch = w.shape
    assert proj.shape[2] == ch
    assert s % SUBLANES == 0
    nb = _pick(bsz, (16, 8, 4, 2, 1))
    cc = _pick(ch, (512, 256, 128))
    kern = functools.partial(_conv_step_kernel, nb=nb, s=s, width=width, cc=cc)
    vec = pl.BlockSpec((1, ch), lambda b: (0, 0))
    frows = -(-(width - 1 + s) // SUBLANES) * SUBLANES
    return pl.pallas_call(
        kern,
        grid=(bsz // nb,),
        in_specs=[pl.BlockSpec((None, nb * s, ch), lambda b: (0, b, 0)),
                  pl.BlockSpec((None, nb * s, ch), lambda b: (1, b, 0)),
                  pl.BlockSpec((nb, width - 1, ch), lambda b: (b, 0, 0)),
                  pl.BlockSpec((width, ch), lambda b: (0, 0)),
                  vec, vec, vec],
        out_specs=[pl.BlockSpec((nb * s, ch), lambda b: (b, 0)),
                   pl.BlockSpec((nb, width - 1, ch), lambda b: (b, 0, 0))],
        out_shape=[jax.ShapeDtypeStruct((bsz * s, ch), BF16),
                   jax.ShapeDtypeStruct((bsz, width - 1, ch), F32)],
        scratch_shapes=[pltpu.VMEM((frows, ch), F32), pltpu.VMEM((nb * s, ch), F32)],
        compiler_params=_params(("parallel",)),
        name="conv_step",
    )(proj, proj, state, w, cb, lw, lb)


def _mlstm_kernel(qk_ref, v_ref, o_ref, g_ref, gt_ref, c0_ref, n0_ref, m0_ref, nw_ref,
                  hm_ref, c_out, n_out, m_out, c_s, n_s, m_s, *, n_heads, scale, nb):
    c = pl.program_id(1)
    last = pl.num_programs(1) - 1
    ln = qk_ref.shape[0] // nb
    wqk = qk_ref.shape[1] // 2
    dqk = wqk // n_heads
    dv = v_ref.shape[1] // n_heads

    @pl.when(c == 0)
    def _():
        c_s[...] = c0_ref[...]
        n_s[...] = n0_ref[...]
        m_s[...] = m0_ref[...]

    row = lax.broadcasted_iota(I32, (ln, ln), 0)
    col = lax.broadcasted_iota(I32, (ln, ln), 1)
    tri = row >= col
    tri_t = row <= col

    def chain(bi, hd):
        rows = slice(bi * ln, (bi + 1) * ln)
        g = g_ref[rows, :]
        gt = gt_ref[bi]
        li_c = g[:, hd:hd + 1]
        lf_c = g[:, n_heads + hd:n_heads + hd + 1]
        li_r = gt[hd:hd + 1, :]
        lf_r = gt[n_heads + hd:n_heads + hd + 1, :]
        b_c = jnp.sum(jnp.where(tri, lf_r, 0.0), axis=1, keepdims=True)
        b_r = jnp.sum(jnp.where(tri_t, lf_c, 0.0), axis=0, keepdims=True)
        b_l = jnp.sum(lf_r, axis=1, keepdims=True)
        m_prev = m_s[bi, :, hd:hd + 1]

        dmat = jnp.where(tri, b_c - b_r + li_r, NEG_BIG)
        inter = b_c + m_prev
        m_t = jnp.maximum(inter, jnp.max(dmat, axis=1, keepdims=True))
        a = jnp.exp(inter - m_t)

        q = qk_ref[rows, hd * dqk:(hd + 1) * dqk]
        k = qk_ref[rows, wqk + hd * dqk:wqk + (hd + 1) * dqk] * scale
        qb = q.astype(BF16)
        vb = v_ref[rows, hd * dv:(hd + 1) * dv].astype(BF16)
        cst = c_s[bi, hd]
        nst = n_s[bi, hd:hd + 1, :]
        s = lax.dot_general(qb, k.astype(BF16), (((1,), (1,)), ((), ())), preferred_element_type=F32)
        s = s * jnp.exp(dmat - m_t)
        num = a * jnp.dot(qb, cst.astype(BF16), preferred_element_type=F32) \
            + jnp.dot(s.astype(BF16), vb, preferred_element_type=F32)
        den = a * jnp.sum(q * nst, axis=1, keepdims=True) + jnp.sum(s, axis=1, keepdims=True)
        h = num / jnp.maximum(jnp.abs(den), jnp.exp(-m_t))
        hn = h * lax.rsqrt(jnp.mean(h * h, -1, keepdims=True) + EPS) * nw_ref[hd:hd + 1, :]
        gate = _sigmoid(o_ref[rows, hd * dv:(hd + 1) * dv])
        hm_ref[rows, hd * dv:(hd + 1) * dv] = (hn * gate).astype(hm_ref.dtype)

        g_r = b_l - b_r + li_r
        g_c = b_l - b_c + li_c
        m_new = jnp.maximum(b_l + m_prev, jnp.max(g_r, axis=1, keepdims=True))
        decay = jnp.exp(b_l + m_prev - m_new)
        kw = k * jnp.exp(g_c - m_new)
        c_new = decay * cst + lax.dot_general(kw.astype(BF16), vb, (((0,), (0,)), ((), ())),
                                              preferred_element_type=F32)
        n_new = decay * nst + jnp.sum(kw, axis=0, keepdims=True)
        c_s[bi, hd] = c_new
        n_s[bi, hd:hd + 1, :] = n_new
        m_s[bi, :, hd:hd + 1] = m_new

    for bi in range(nb):
        for hd in range(n_heads):
            chain(bi, hd)

    @pl.when(c == last)
    def _():
        c_out[...] = c_s[...]
        n_out[...] = n_s[...]
        m_out[...] = m_s[...]


def _mlstm(proj, g, c0, n0, m0, norm_w, bsz, s, slab0):
    _, n_heads, dqk, dv = c0.shape
    ln = s if s <= MLSTM_CHUNK else MLSTM_CHUNK
    assert s % ln == 0 and ln % SUBLANES == 0
    nc = s // ln
    wqk, wv = n_heads * dqk, n_heads * dv
    slab = proj.shape[2]
    assert 2 * wqk == slab and wv == slab
    gt3 = g[:, :2 * n_heads].reshape(bsz * nc, ln, 2 * n_heads).transpose(0, 2, 1)
    nb = _pick(bsz, (MLSTM_SEQS, 2, 1)) if nc == 1 else 1
    kern = functools.partial(_mlstm_kernel, n_heads=n_heads, scale=dqk ** -0.5, nb=nb)
    rows = lambda col: (lambda b, c: (b * nc + c, col))
    slab_rows = lambda k: pl.BlockSpec((None, nb * ln, slab), lambda b, c: (slab0 + k, b * nc + c, 0))
    per_seq = lambda shape: pl.BlockSpec((nb,) + shape, lambda b, c: (b,) + (0,) * len(shape))
    out = pl.pallas_call(
        kern,
        grid=(bsz // nb, nc),
        in_specs=[slab_rows(0), slab_rows(1), slab_rows(2),
                  pl.BlockSpec((nb * ln, LANES), rows(0)),
                  pl.BlockSpec((nb, 2 * n_heads, ln), lambda b, c: (b * nc + c, 0, 0)),
                  per_seq((n_heads, dqk, dv)), per_seq((n_heads, dqk)), per_seq((1, n_heads)),
                  pl.BlockSpec((n_heads, dv), lambda b, c: (0, 0))],
        out_specs=[pl.BlockSpec((nb * ln, wv), rows(0)),
                   per_seq((n_heads, dqk, dv)), per_seq((n_heads, dqk)), per_seq((1, n_heads))],
        out_shape=[jax.ShapeDtypeStruct((bsz * s, wv), BF16),
                   jax.ShapeDtypeStruct((bsz, n_heads, dqk, dv), F32),
                   jax.ShapeDtypeStruct((bsz, n_heads, dqk), F32),
                   jax.ShapeDtypeStruct((bsz, 1, n_heads), F32)],
        scratch_shapes=[pltpu.VMEM((nb, n_heads, dqk, dv), F32), pltpu.VMEM((nb, n_heads, dqk), F32),
                        pltpu.VMEM((nb, 1, n_heads), F32)],
        compiler_params=_params(("arbitrary", "arbitrary")),
        name="mlstm",
    )(proj, proj, proj, g, gt3, c0, n0, m0.reshape(bsz, 1, n_heads), norm_w)
    hm, c_new, n_new, m_new = out
    return hm, c_new, n_new, m_new.reshape(bsz, n_heads)


def _outproj_kernel(co_ref, hm_ref, w_ref, o_ref):
    kc = co_ref.shape[1]
    o_ref[...] = jnp.dot(co_ref[...], w_ref[0:kc, :], preferred_element_type=F32) \
        + jnp.dot(hm_ref[...], w_ref[kc:2 * kc, :], preferred_element_type=F32)


def _outproj(co, hm, w_bf):
    m, kc = co.shape
    d = w_bf.shape[-1]
    assert hm.shape[1] == kc and w_bf.shape[0] == 2 * kc
    tn = _pick(d, (2048, 1024, 512, 256, 128))
    tm = _pick(m, (512, 256, 128, 64, 32, 16, 8))
    return pl.pallas_call(
        _outproj_kernel,
        grid=(d // tn, m // tm),
        in_specs=[pl.BlockSpec((tm, kc), lambda j, i: (i, 0)),
                  pl.BlockSpec((tm, kc), lambda j, i: (i, 0)),
                  pl.BlockSpec((2 * kc, tn), lambda j, i: (0, j))],
        out_specs=pl.BlockSpec((tm, tn), lambda j, i: (i, j)),
        out_shape=jax.ShapeDtypeStruct((m, d), F32),
        compiler_params=_params(("parallel", "parallel")),
        name="outproj",
    )(co, hm, w_bf)


def _router_kernel(mix_ref, x_ref, g1_ref, sc2_ref, sh2_ref, n1_ref, n2_ref, wr_ref, br_ref,
                   x1_ref, h2_ref, ti_ref, tg_ref, *, st):
    mix = mix_ref[...]
    mn = mix * lax.rsqrt(jnp.mean(mix * mix, -1, keepdims=True) + EPS) * n1_ref[...]
    x1 = x_ref[...] + _mod_rows(g1_ref, st) * mn
    x1_ref[...] = x1
    y2 = x1 * lax.rsqrt(jnp.mean(x1 * x1, -1, keepdims=True) + EPS) * n2_ref[...]
    h2 = y2 * (1.0 + _mod_rows(sc2_ref, st)) + _mod_rows(sh2_ref, st)
    half = h2.shape[1] // 2
    h2_ref[...] = _pack_pair(h2[:, :half], h2[:, half:])
    logits = jnp.dot(h2.astype(BF16), wr_ref[...], preferred_element_type=F32) + br_ref[...]
    n_exp = logits.shape[1]
    lane = lax.broadcasted_iota(I32, logits.shape, 1)
    lane_o = lax.broadcasted_iota(I32, ti_ref.shape, 1)
    idx_out = jnp.zeros(ti_ref.shape, I32)
    val_out = jnp.zeros(tg_ref.shape, F32)
    top = None
    den = jnp.zeros((logits.shape[0], 1), F32)
    for r in range(TOP_K):
        mx = jnp.max(logits, axis=1, keepdims=True)
        ix = jnp.min(jnp.where(logits == mx, lane, n_exp), axis=1, keepdims=True)
        if top is None:
            top = mx
        e = jnp.exp(mx - top)
        den = den + e
        idx_out = jnp.where(lane_o == r, ix, idx_out)
        val_out = jnp.where(lane_o == r, e, val_out)
        logits = jnp.where(lane == ix, NEG_BIG, logits)
    ti_ref[...] = idx_out
    tg_ref[...] = val_out / den


def _router(mix, x2, mod3, grp, n1, n2, wr, br):
    rows, d = x2.shape
    n_exp = wr.shape[1]
    kern = functools.partial(_router_kernel, st=grp.st)
    return pl.pallas_call(
        kern,
        grid=grp.grid,
        in_specs=[grp.rows(d), grp.rows(d), grp.mod(d, 2), grp.mod(d, 4), grp.mod(d, 3),
                  grp.const((1, d)), grp.const((1, d)), grp.const((d, n_exp)), grp.const((1, n_exp))],
        out_specs=[grp.rows(d), grp.rows(d // 2), grp.rows(LANES), grp.rows(LANES)],
        out_shape=[jax.ShapeDtypeStruct((rows, d), F32), jax.ShapeDtypeStruct((rows, d // 2), U32),
                   jax.ShapeDtypeStruct((rows, LANES), I32), jax.ShapeDtypeStruct((rows, LANES), F32)],
        compiler_params=_params(("parallel", "parallel")),
        name="router",
    )(mix, x2, mod3, mod3, mod3, n1, n2, wr, br)


def _rank_kernel(ti_ref, ps_ref, o_ref, carry_ref):
    @pl.when(pl.program_id(0) == 0)
    def _():
        carry_ref[...] = jnp.zeros(carry_ref.shape, F32)

    ti = ti_ref[...].astype(F32)
    tt = ti.shape[0]
    lane = lax.broadcasted_iota(I32, ti.shape, 1)
    lane_f = lane.astype(F32)
    cols = []
    member = jnp.zeros(ti.shape, F32)
    for kk in range(TOP_K):
        ek = jnp.sum(jnp.where(lane == kk, ti, 0.0), axis=1, keepdims=True)
        cols.append(ek)
        member = member + jnp.where(lane_f == ek, 1.0, 0.0)
    r = lax.broadcasted_iota(I32, (tt, tt), 0)
    c = lax.broadcasted_iota(I32, (tt, tt), 1)
    before = jnp.where(r > c, 1.0, 0.0).astype(BF16)
    base = jnp.dot(before, member.astype(BF16), preferred_element_type=F32) + carry_ref[...] + ps_ref[...]
    out = jnp.zeros(o_ref.shape, I32)
    for kk in range(TOP_K):
        dk = jnp.sum(jnp.where(lane_f == cols[kk], base, 0.0), axis=1, keepdims=True)
        out = jnp.where(lane == kk, dk.astype(I32), out)
    o_ref[...] = out
    carry_ref[...] += jnp.sum(member, axis=0, keepdims=True)


def _rank(top_i, pad_start):
    t = top_i.shape[0]
    tt = _pick(t, (512, 256, 128, 64, 32, 16, 8))
    return pl.pallas_call(
        _rank_kernel,
        grid=(t // tt,),
        in_specs=[pl.BlockSpec((tt, LANES), lambda i: (i, 0)), pl.BlockSpec((1, LANES), lambda i: (0, 0))],
        out_specs=pl.BlockSpec((tt, LANES), lambda i: (i, 0)),
        out_shape=jax.ShapeDtypeStruct((t, LANES), I32),
        scratch_shapes=[pltpu.VMEM((1, LANES), F32)],
        compiler_params=_params(("arbitrary",)),
        name="moe_rank",
    )(top_i, pad_start)


def _route(top_i, n_experts, tm):
    t = top_i.shape[0]
    a = t * TOP_K
    flat_e = top_i[:, :TOP_K].reshape(a)
    counts = jnp.sum((flat_e[:, None] == jnp.arange(n_experts, dtype=I32)[None, :]).astype(I32), axis=0)
    padded = (counts + tm - 1) // tm * tm
    pad_end = jnp.cumsum(padded)
    pad_start = pad_end - padded
    nt = a // tm + n_experts
    tile_start = jnp.arange(nt, dtype=I32) * tm
    tile_u = (tile_start < pad_end[-1]).astype(I32)
    n_used = jnp.sum(tile_u)
    tile_e = jnp.sum((pad_end[None, :] <= tile_start[:, None]).astype(I32), axis=1)
    tile_e = jnp.minimum(tile_e, n_experts - 1)
    tile_e = jnp.where(tile_u == 1, tile_e, tile_e[jnp.maximum(n_used - 1, 0)])
    idx = jnp.arange(nt, dtype=I32)
    starts = jnp.logical_and(tile_u == 1, jnp.logical_or(idx == 0, tile_e != jnp.roll(tile_e, 1)))
    seg = (jnp.cumsum(starts.astype(I32)) - 1).astype(I32)
    later = lax.cummin(jnp.where(starts, idx, nt)[::-1])[::-1]
    nxt = jnp.concatenate([later[1:], jnp.full((1,), nt, I32)])
    next_e = tile_e[jnp.where(nxt >= nt, 0, nxt)]
    n_seg = jnp.sum(starts.astype(I32)).reshape(1)
    x_tile = jnp.minimum(idx, jnp.maximum(n_used - 1, 0))
    tiles = (tile_e, tile_u, seg, next_e, n_seg, x_tile)
    fill = (jnp.concatenate([pad_start + counts, pad_end[-1:]]).astype(I32),
            jnp.concatenate([padded - counts, nt * tm - pad_end[-1:]]).astype(I32))
    ps = jnp.zeros((1, LANES), F32).at[0, :n_experts].set(pad_start.astype(F32))
    dest = _rank(top_i, ps)
    return dest, tiles, fill, nt


def _row_copy(src_ref, dst_ref, sem, src_row, dst_row):
    return pltpu.make_async_copy(src_ref.at[pl.ds(src_row, 1), :], dst_ref.at[pl.ds(dst_row, 1), :], sem)


def _dispatch_kernel(fs_ref, fn_ref, dest_ref, srcp_ref, srcs_ref, xs_out, zrow_ref, sem, *, ntp, nts):
    i = pl.program_id(0)

    def send(src_ref):
        def start(r, carry):
            for kk in range(TOP_K):
                _row_copy(src_ref, xs_out, sem, r, dest_ref[0, r * TOP_K + kk]).start(priority=kk % 2)
            return carry

        lax.fori_loop(0, src_ref.shape[0], start, 0, unroll=ISSUE_UNROLL)
        for _ in range(TOP_K):
            pltpu.make_async_copy(src_ref, src_ref, sem).wait()

    @pl.when(i < ntp)
    def _():
        send(srcp_ref)

    @pl.when(jnp.logical_and(i >= ntp, i < ntp + nts))
    def _():
        send(srcs_ref)

    @pl.when(i >= ntp + nts)
    def _():
        f = i - (ntp + nts)
        row0, n = fs_ref[f], fn_ref[f]
        zrow_ref[...] = jnp.zeros(zrow_ref.shape, zrow_ref.dtype)

        def start(r, carry):
            _row_copy(zrow_ref, xs_out, sem, 0, row0 + r).start()
            return carry

        def wait(r, carry):
            _row_copy(zrow_ref, xs_out, sem, 0, row0).wait()
            return carry

        lax.fori_loop(0, n, start, 0)
        lax.fori_loop(0, n, wait, 0)


def _dispatch(h2p_p, h2p_s, dest, fill, rows):
    fill_start, fill_n = fill
    tp, w = h2p_p.shape
    tsm = h2p_s.shape[0]
    tt = _pick(tsm, (256, 128, 64, 32, 16, 8))
    assert tp % tt == 0
    ntp, nts = tp // tt, tsm // tt
    kern = functools.partial(_dispatch_kernel, ntp=ntp, nts=nts)
    grid_spec = pltpu.PrefetchScalarGridSpec(
        num_scalar_prefetch=2,
        grid=(ntp + nts + fill_n.shape[0],),
        in_specs=[pl.BlockSpec((None, 1, tt * TOP_K), lambda i, *_: (jnp.minimum(i, ntp + nts - 1), 0, 0),
                               memory_space=pltpu.SMEM),
                  pl.BlockSpec((tt, w), lambda i, *_: (jnp.minimum(i, ntp - 1), 0)),
                  pl.BlockSpec((tt, w), lambda i, *_: (jnp.clip(i - ntp, 0, nts - 1), 0))],
        out_specs=pl.BlockSpec(memory_space=pl.ANY),
        scratch_shapes=[pltpu.VMEM((SUBLANES, w), U32), pltpu.SemaphoreType.DMA(())])
    return pl.pallas_call(
        kern,
        grid_spec=grid_spec,
        out_shape=jax.ShapeDtypeStruct((rows, w), U32),
        compiler_params=_params(("arbitrary",)),
        name="moe_dispatch",
    )(fill_start, fill_n, dest[:, :TOP_K].reshape(ntp + nts, 1, tt * TOP_K), h2p_p, h2p_s)


N_TILE_TABLES = 6


def _segment_weights(tables, w_hbm, wst_ref, sem, *, l, tn, col_offs):
    te_ref, tu_ref, sg_ref, ne_ref, ns_ref = tables[:5]
    j = pl.program_id(0)
    i = pl.program_id(1)
    n_seg = ns_ref[0]
    first = jnp.logical_and(tu_ref[i] == 1,
                            jnp.logical_or(i == 0, te_ref[i] != te_ref[jnp.maximum(i - 1, 0)]))
    g = j * n_seg + sg_ref[i]

    def copies(e, jj):
        return [pltpu.make_async_copy(
            w_hbm.at[l, e, :, pl.ds(pl.multiple_of((off + jj) * tn, tn), tn)], wst_ref.at[m], sem)
            for m, off in enumerate(col_offs)]

    @pl.when(jnp.logical_and(j == 0, i == 0))
    def _():
        for cp in copies(te_ref[0], 0):
            cp.start(priority=WEIGHT_DMA_PRIORITY)

    def wait_weights():
        for cp in copies(te_ref[i], j):
            cp.wait()

    def start_next():
        @pl.when(g + 1 < pl.num_programs(0) * n_seg)
        def _():
            jn = jnp.where(sg_ref[i] + 1 == n_seg, j + 1, j)
            for cp in copies(ne_ref[i], jn):
                cp.start(priority=WEIGHT_DMA_PRIORITY)

    return first, wait_weights, start_next


def _gmm1_kernel(*refs, l, tn, nj):
    tables = refs[:N_TILE_TABLES]
    x_ref, w_hbm, bg_ref, bu_ref, o_ref, wst_ref, wgb_ref, wub_ref, sem = refs[N_TILE_TABLES:]
    used = tables[1][pl.program_id(1)] == 1
    first, wait_weights, start_next = _segment_weights(tables, w_hbm, wst_ref, sem, l=l, tn=tn,
                                                       col_offs=(0, nj))
    half = x_ref.shape[1]

    def tile(wg_lo, wg_hi, wu_lo, wu_hi):
        xlo, xhi = _unpack_pair(x_ref[...])
        gt = jnp.dot(xlo, wg_lo, preferred_element_type=F32) \
            + jnp.dot(xhi, wg_hi, preferred_element_type=F32) + bg_ref[...]
        up = jnp.dot(xlo, wu_lo, preferred_element_type=F32) \
            + jnp.dot(xhi, wu_hi, preferred_element_type=F32) + bu_ref[...]
        gt = jnp.minimum(gt, SWIGLU_LIMIT)
        up = jnp.clip(up, -SWIGLU_LIMIT, SWIGLU_LIMIT)
        act = (up + 1.0) * gt * _sigmoid(SWIGLU_ALPHA * gt)
        o_ref[...] = act.astype(o_ref.dtype)

    @pl.when(first)
    def _():
        wait_weights()
        ws = []
        for m, dst in enumerate((wgb_ref, wub_ref)):
            for k0 in (0, half):
                wk = wst_ref[m, k0:k0 + half, :].astype(BF16)
                dst[k0:k0 + half, :] = wk
                ws.append(wk)
        tile(*ws)
        start_next()

    @pl.when(jnp.logical_and(used, jnp.logical_not(first)))
    def _():
        tile(wgb_ref[0:half, :], wgb_ref[half:2 * half, :], wub_ref[0:half, :], wub_ref[half:2 * half, :])

    @pl.when(jnp.logical_not(used))
    def _():
        o_ref[...] = jnp.zeros(o_ref.shape, o_ref.dtype)


def _gmm1(xs, w1, b1, tiles, l, tm):
    rows, half = xs.shape
    d = 2 * half
    dff = w1.shape[-1] // 2
    nt = rows // tm
    tn = _pick(dff, (1024, 512, 256, 128))
    nj = dff // tn
    b1r = b1.reshape(b1.shape[0], b1.shape[1], 1, 2 * dff)
    kern = functools.partial(_gmm1_kernel, l=l, tn=tn, nj=nj)
    grid_spec = pltpu.PrefetchScalarGridSpec(
        num_scalar_prefetch=N_TILE_TABLES,
        grid=(nj, nt),
        in_specs=[pl.BlockSpec((tm, half), lambda j, i, *t: (t[5][i], 0)),
                  pl.BlockSpec(memory_space=pl.ANY),
                  pl.BlockSpec((None, None, 1, tn), lambda j, i, te, *t: (l, te[i], 0, j)),
                  pl.BlockSpec((None, None, 1, tn), lambda j, i, te, *t: (l, te[i], 0, nj + j))],
        out_specs=pl.BlockSpec((tm, tn), lambda j, i, *t: (i, j)),
        scratch_shapes=[pltpu.VMEM((2, d, tn), F32), pltpu.VMEM((d, tn), BF16), pltpu.VMEM((d, tn), BF16),
                        pltpu.SemaphoreType.DMA(())])
    return pl.pallas_call(
        kern,
        grid_spec=grid_spec,
        out_shape=jax.ShapeDtypeStruct((rows, dff), BF16),
        compiler_params=_params(("arbitrary", "arbitrary"), GMM_VMEM_LIMIT),
        name="moe_gmm1",
    )(*tiles, xs, w1, b1r, b1r)


def _gmm2_kernel(*refs, l, tn):
    tables = refs[:N_TILE_TABLES]
    a_ref, w_hbm, b_ref, o_ref, wst_ref, wb_ref, sem = refs[N_TILE_TABLES:]
    used = tables[1][pl.program_id(1)] == 1
    first, wait_weights, start_next = _segment_weights(tables, w_hbm, wst_ref, sem, l=l, tn=tn, col_offs=(0,))

    def tile(w):
        y = jnp.dot(a_ref[...], w, preferred_element_type=F32) + b_ref[...]
        o_ref[...] = _pack_pair(y[:, :tn // 2], y[:, tn // 2:])

    @pl.when(first)
    def _():
        wait_weights()
        w = wst_ref[0].astype(BF16)
        wb_ref[...] = w
        tile(w)
        start_next()

    @pl.when(jnp.logical_and(used, jnp.logical_not(first)))
    def _():
        tile(wb_ref[...])

    @pl.when(jnp.logical_not(used))
    def _():
        o_ref[...] = jnp.zeros(o_ref.shape, o_ref.dtype)


def _gmm2(act, w2, b2, tiles, l, tm):
    rows, dff = act.shape
    d = w2.shape[-1]
    nt = rows // tm
    tn = _pick(d, (4096, 2048, 1024, 512, 256, 128))
    b2r = b2.reshape(b2.shape[0], b2.shape[1], 1, d)
    kern = functools.partial(_gmm2_kernel, l=l, tn=tn)
    grid_spec = pltpu.PrefetchScalarGridSpec(
        num_scalar_prefetch=N_TILE_TABLES,
        grid=(d // tn, nt),
        in_specs=[pl.BlockSpec((tm, dff), lambda j, i, *t: (i, 0)),
                  pl.BlockSpec(memory_space=pl.ANY),
                  pl.BlockSpec((None, None, 1, tn), lambda j, i, te, *t: (l, te[i], 0, j))],
        out_specs=pl.BlockSpec((tm, tn // 2), lambda j, i, *t: (i, j)),
        scratch_shapes=[pltpu.VMEM((1, dff, tn), F32), pltpu.VMEM((dff, tn), BF16),
                        pltpu.SemaphoreType.DMA(())])
    y = pl.pallas_call(
        kern,
        grid_spec=grid_spec,
        out_shape=jax.ShapeDtypeStruct((rows, d // 2), U32),
        compiler_params=_params(("arbitrary", "arbitrary"), GMM_VMEM_LIMIT),
        name="moe_gmm2",
    )(*tiles, act, w2, b2r)
    return y, tn // 2


def _combine_kernel(pos_ref, posn_ref, y_hbm, tg_ref, x1_ref, g2_ref, nw_ref, o_ref, buf_ref, sem, *, st, pw):
    ni = pl.num_programs(1)
    n = pl.program_id(0) * ni + pl.program_id(1)
    total = pl.num_programs(0) * ni
    tt, d = o_ref.shape
    slot = lax.rem(n, 2)

    def start_rows(idx_ref, sl):
        def body(r, carry):
            for kk in range(TOP_K):
                cp = _row_copy(y_hbm, buf_ref.at[sl, kk], sem.at[sl], idx_ref[0, r * TOP_K + kk], r)
                cp.start(priority=kk % 2)
            return carry
        lax.fori_loop(0, tt, body, 0, unroll=ISSUE_UNROLL)

    @pl.when(n == 0)
    def _():
        start_rows(pos_ref, 0)

    @pl.when(n + 1 < total)
    def _():
        start_rows(posn_ref, 1 - slot)

    pltpu.make_async_copy(buf_ref.at[slot], buf_ref.at[slot], sem.at[slot]).wait()

    gates = tg_ref[...]
    lane = lax.broadcasted_iota(I32, gates.shape, 1)
    gk = [jnp.sum(jnp.where(lane == kk, gates, 0.0), axis=1, keepdims=True) for kk in range(TOP_K)]
    ssq = jnp.zeros((tt, 1), F32)
    for blk in range(d // (2 * pw)):
        lo = jnp.zeros((tt, pw), F32)
        hi = jnp.zeros((tt, pw), F32)
        for kk in range(TOP_K):
            p = buf_ref[slot, kk, :, blk * pw:(blk + 1) * pw]
            lo = lo + gk[kk] * lax.bitcast_convert_type(p << 16, F32)
            hi = hi + gk[kk] * lax.bitcast_convert_type(p & jnp.uint32(HI_HALF), F32)
        ssq = ssq + jnp.sum(lo * lo, -1, keepdims=True) + jnp.sum(hi * hi, -1, keepdims=True)
        o_ref[:, 2 * pw * blk:2 * pw * blk + pw] = lo
        o_ref[:, 2 * pw * blk + pw:2 * pw * (blk + 1)] = hi
    fn = o_ref[...] * lax.rsqrt(ssq / d + EPS) * nw_ref[...]
    o_ref[...] = x1_ref[...] + _mod_rows(g2_ref, st) * fn


def _combine(y, pw, dest, tg, x1, mod3, grp, nw):
    rows, d = x1.shape
    tt = grp.ts
    nt = rows // tt
    ni = grp.n_inner
    kern = functools.partial(_combine_kernel, st=grp.st, pw=pw)
    dest3 = dest[:, :TOP_K].reshape(nt, 1, tt * TOP_K)
    idx_spec = lambda ahead: pl.BlockSpec(
        (None, 1, tt * TOP_K), lambda o, i: (jnp.minimum(o * ni + i + ahead, nt - 1), 0, 0),
        memory_space=pltpu.SMEM)
    return pl.pallas_call(
        kern,
        grid=grp.grid,
        in_specs=[idx_spec(0), idx_spec(1),
                  pl.BlockSpec(memory_space=pl.ANY),
                  grp.rows(LANES), grp.rows(d), grp.mod(d, 5), grp.const((1, d))],
        out_specs=grp.rows(d),
        out_shape=jax.ShapeDtypeStruct((rows, d), F32),
        scratch_shapes=[pltpu.VMEM((2, TOP_K, tt, d // 2), U32), pltpu.SemaphoreType.DMA((2,))],
        compiler_params=_params(("arbitrary", "arbitrary")),
        name="moe_combine",
    )(dest3, dest3, y, tg, x1, mod3, nw)


def _mixer(x2, mod3, grp, conv_buf, c0, n0, m0, p, l):
    n_heads = c0.shape[1]
    ch = p['conv_w'].shape[-1]
    h, g = _prenorm(x2, mod3, grp, p['norm1_pre'], p['w_gate'], p['b_gate'], n_heads)
    proj = _inproj(h, p['w_in'], ch)
    conv = _conv_seq if grp.nbq == 1 else _conv_step
    co, new_buf = conv(proj, conv_buf, p['conv_w'], p['conv_b'], p['conv_ln_w'], p['conv_ln_b'], grp.bsz, grp.s)
    hm, c_new, n_new, m_new = _mlstm(proj, g, c0, n0, m0, p['mlstm_norm_w'], grp.bsz, grp.s, 2)
    mix = _outproj(co, hm, p['w_out'])
    x1, h2p, ti, tg = _router(mix, x2, mod3, grp, p['norm1_post'], p['norm2_pre'], p['w_router'], p['b_router'])
    return dict(x1=x1, h2p=h2p, ti=ti, tg=tg, state=(new_buf, c_new, n_new, m_new))


def kernel(x_prompt, x_sample, c_prompt, c_sample, state_conv, state_mlstm_C, state_mlstm_n, state_mlstm_m,
           w_ada, b_ada, norm1_pre, w_in, b_gates, conv_w, conv_b, conv_ln_w, conv_ln_b, mlstm_norm_w,
           w_out, norm1_post, norm2_pre, w_router, b_router, w1, b1, w2, b2, norm2_post):
    depth = w_ada.shape[0]
    bp, sp, d = x_prompt.shape
    bs, ss, _ = x_sample.shape
    n_heads, dqk, dv = state_mlstm_C.shape[2:]
    n_experts = w_router.shape[-1]
    nst, ch = state_conv.shape[2:]
    n_gate = 2 * n_heads
    assert n_gate <= LANES and n_experts <= LANES
    tp, tsmp = bp * sp, bs * ss

    mp = -(-(bp + bs) // SUBLANES) * SUBLANES
    c_all = jnp.zeros((mp, d), F32).at[:bs].set(c_sample).at[bs:bs + bp].set(c_prompt)
    grp_p = _Group(bp, sp, bs, ROW_TILE)
    grp_s = _Group(bs, ss, 0, ROW_TILE)

    xp, xs = x_prompt.reshape(tp, d), x_sample.reshape(tsmp, d)
    outs = [[] for _ in range(8)]
    for l in range(depth):
        row = lambda v: v[l].reshape(1, -1)
        p = dict(
            w_in=w_in[l, :, :w_in.shape[-1] - n_gate].astype(BF16), w_out=w_out[l].astype(BF16),
            w_gate=jnp.pad(w_in[l, :, w_in.shape[-1] - n_gate:], ((0, 0), (0, LANES - n_gate))),
            b_gate=jnp.pad(b_gates[l], (0, LANES - n_gate)).reshape(1, LANES),
            norm1_pre=row(norm1_pre), conv_w=conv_w[l], conv_b=row(conv_b), conv_ln_w=row(conv_ln_w),
            conv_ln_b=row(conv_ln_b), mlstm_norm_w=mlstm_norm_w[l], norm1_post=row(norm1_post),
            norm2_pre=row(norm2_pre), w_router=w_router[l].astype(BF16), b_router=row(b_router))
        mod3 = _ada(c_all, w_ada, b_ada, l).reshape(mp, 1, 6 * d)
        zero = lambda shape: jnp.zeros(shape, F32)
        gp = _mixer(xp, mod3, grp_p, zero((bp, nst, ch)), zero((bp, n_heads, dqk, dv)),
                    zero((bp, n_heads, dqk)), zero((bp, n_heads)), p, l)
        gs = _mixer(xs, mod3, grp_s, state_conv[l], state_mlstm_C[l], state_mlstm_n[l],
                    state_mlstm_m[l], p, l)

        top_i = jnp.concatenate([gp['ti'], gs['ti']], axis=0)
        dest, tiles, fill, nt = _route(top_i, n_experts, MOE_ROWS)
        xsorted = _dispatch(gp['h2p'], gs['h2p'], dest, fill, nt * MOE_ROWS)
        act = _gmm1(xsorted, w1, b1, tiles, l, MOE_ROWS)
        y, pw = _gmm2(act, w2, b2, tiles, l, MOE_ROWS)
        nw2 = row(norm2_post)
        xp = _combine(y, pw, dest[:tp], gp['tg'], gp['x1'], mod3, grp_p, nw2)
        xs = _combine(y, pw, dest[tp:], gs['tg'], gs['x1'], mod3, grp_s, nw2)
        for o, v in zip(outs, gp['state'] + gs['state']):
            o.append(v)
    stack = (lambda o: o[0][None]) if depth == 1 else jnp.stack
    return (xp.reshape(bp, sp, d), xs.reshape(bs, ss, d)) + tuple(stack(o) for o in outs)
```

```python
import functools

import jax
import jax.numpy as jnp
from jax import lax
from jax.experimental import pallas as pl
from jax.experimental.pallas import tpu as pltpu

F32 = jnp.float32
BF16 = jnp.bfloat16
I32 = jnp.int32
U32 = jnp.uint32

EPS = 1e-6
GATE_CAP = 15.0
TOP_K = 4
SWIGLU_LIMIT = 7.0
SWIGLU_ALPHA = 1.702
NEG_BIG = -1e30

LANES = 128
SUBLANES = 8
VMEM_LIMIT = 56 * 1024 * 1024
GMM_VMEM_LIMIT = 60 * 1024 * 1024
MLSTM_CHUNK = 256
MLSTM_SEQS = 4
MOE_ROWS = 256
ROW_TILE = 256
HI_HALF = 0xFFFF0000
ISSUE_UNROLL = 4
WEIGHT_DMA_PRIORITY = 1


def _params(sem, vmem=VMEM_LIMIT):
    return pltpu.CompilerParams(dimension_semantics=sem, vmem_limit_bytes=vmem)


def _sigmoid(x):
    return 1.0 / (1.0 + jnp.exp(-x))


def _pick(n, prefs):
    for p in prefs:
        if n % p == 0:
            return p
    return n


def _pack_pair(lo, hi):
    lo_b = lax.bitcast_convert_type(lo.astype(BF16).astype(F32), U32) >> 16
    hi_b = lax.bitcast_convert_type(hi.astype(BF16).astype(F32), U32) & jnp.uint32(HI_HALF)
    return hi_b | lo_b


def _unpack_pair(p):
    lo = lax.bitcast_convert_type(p << 16, F32).astype(BF16)
    hi = lax.bitcast_convert_type(p & jnp.uint32(HI_HALF), F32).astype(BF16)
    return lo, hi


class _Group:
    def __init__(self, bsz, s, mod_row0, tile_rows):
        self.bsz, self.s = bsz, s
        if s >= tile_rows:
            self.nbq, self.st = 1, _pick(s, (tile_rows, 128, 64, 32, 16, 8))
            self.n_outer, self.n_inner = bsz, s // self.st
        else:
            self.nbq, self.st = _pick(bsz, (tile_rows // s, 8, 4, 2, 1)), s
            self.n_outer, self.n_inner = bsz // self.nbq, 1
        assert mod_row0 % self.nbq == 0
        self.mod_blk0 = mod_row0 // self.nbq
        self.ts = self.nbq * self.st
        self.grid = (self.n_outer, self.n_inner)

    def rows(self, width, col=0):
        ni = self.n_inner
        return pl.BlockSpec((self.ts, width), lambda o, i, *_: (o * ni + i, col))

    def mod(self, d, col):
        b0 = self.mod_blk0
        return pl.BlockSpec((self.nbq, 1, d), lambda o, i, *_: (b0 + o, 0, col))

    def const(self, shape):
        nd = len(shape)
        return pl.BlockSpec(shape, lambda o, i, *_: (0,) * nd)


def _mod_rows(m_ref, st):
    m = m_ref[...]
    nbq, _, d = m.shape
    if nbq == 1:
        return m[0]
    return jnp.broadcast_to(m, (nbq, st, d)).reshape(nbq * st, d)


def _ada_kernel(c_ref, w_ref, b_ref, o_ref):
    c = c_ref[...]
    s = (c * _sigmoid(c)).astype(BF16)
    o_ref[...] = jnp.dot(s, w_ref[...].astype(BF16), preferred_element_type=F32) + b_ref[...]


def _ada(c_all, w_ada, b_ada, l):
    mp, d = c_all.shape
    n = w_ada.shape[-1]
    tn = _pick(n, (512, 256, 128))
    return pl.pallas_call(
        _ada_kernel,
        grid=(n // tn,),
        in_specs=[pl.BlockSpec((mp, d), lambda j: (0, 0)),
                  pl.BlockSpec((None, d, tn), lambda j: (l, 0, j)),
                  pl.BlockSpec((None, 1, tn), lambda j: (l, 0, j))],
        out_specs=pl.BlockSpec((mp, tn), lambda j: (0, j)),
        out_shape=jax.ShapeDtypeStruct((mp, n), F32),
        compiler_params=_params(("parallel",)),
        name="ada",
    )(c_all, w_ada, b_ada.reshape(b_ada.shape[0], 1, n))


def _prenorm_kernel(x_ref, sc_ref, sh_ref, nw_ref, wg_ref, bg_ref, h_ref, g_ref, *, n_heads, st):
    x = x_ref[...]
    y = x * lax.rsqrt(jnp.mean(x * x, -1, keepdims=True) + EPS) * nw_ref[...]
    h = y * (1.0 + _mod_rows(sc_ref, st)) + _mod_rows(sh_ref, st)
    hb = h.astype(BF16)
    h_ref[...] = hb
    z = jnp.dot(hb, wg_ref[...].astype(BF16), preferred_element_type=F32) + bg_ref[...]
    cap = GATE_CAP * jnp.tanh(z / GATE_CAP)
    logsig = jnp.minimum(cap, 0.0) - jnp.log(1.0 + jnp.exp(-jnp.abs(cap)))
    lane = lax.broadcasted_iota(I32, z.shape, 1)
    g_ref[...] = jnp.where(lane < n_heads, cap, logsig)


def _prenorm(x2, mod3, grp, nw, wg, bg, n_heads):
    rows, d = x2.shape
    kern = functools.partial(_prenorm_kernel, n_heads=n_heads, st=grp.st)
    return pl.pallas_call(
        kern,
        grid=grp.grid,
        in_specs=[grp.rows(d), grp.mod(d, 1), grp.mod(d, 0), grp.const((1, d)),
                  grp.const((d, LANES)), grp.const((1, LANES))],
        out_specs=[grp.rows(d), grp.rows(LANES)],
        out_shape=[jax.ShapeDtypeStruct((rows, d), BF16), jax.ShapeDtypeStruct((rows, LANES), F32)],
        compiler_params=_params(("parallel", "parallel")),
        name="prenorm",
    )(x2, mod3, mod3, nw, wg, bg)


def _inproj_kernel(a_ref, w_ref, o_ref):
    o_ref[...] = jnp.dot(a_ref[...], w_ref[...], preferred_element_type=F32)


def _inproj(h, w_bf, slab):
    m, d = h.shape
    n = w_bf.shape[1]
    assert n % slab == 0
    tm = _pick(m, (512, 256, 128, 64, 32, 16, 8))
    return pl.pallas_call(
        _inproj_kernel,
        grid=(n // slab, m // tm),
        in_specs=[pl.BlockSpec((tm, d), lambda j, i: (i, 0)),
                  pl.BlockSpec((d, slab), lambda j, i: (0, j))],
        out_specs=pl.BlockSpec((None, tm, slab), lambda j, i: (j, i, 0)),
        out_shape=jax.ShapeDtypeStruct((n // slab, m, slab), F32),
        compiler_params=_params(("parallel", "parallel")),
        name="inproj",
    )(h, w_bf)


CONV_HALO = 32


def _conv_taps(f_ref, w_ref, base, rows, c0, cc, width):
    acc = jnp.zeros((rows, cc), F32)
    for ph in range(SUBLANES):
        if ph >= width:
            break
        n_al = (width - 1 - ph) // SUBLANES + 1
        gb = f_ref[base + ph: base + ph + rows + SUBLANES * (n_al - 1), c0:c0 + cc]
        for a in range(n_al):
            j = SUBLANES * a + ph
            acc = acc + w_ref[j:j + 1, c0:c0 + cc] * gb[SUBLANES * a:SUBLANES * a + rows]
    return acc


def _ln_swish(y, lw, lb):
    mu = jnp.mean(y, -1, keepdims=True)
    yc = y - mu
    yn = yc * lax.rsqrt(jnp.mean(yc * yc, -1, keepdims=True) + EPS) * lw + lb
    return yn * _sigmoid(yn)


def _conv_taps_strided(f_ref, w_ref, cb_ref, y_ref, off, ts, c, width):
    nseg = ts // SUBLANES
    lanes = slice(c * LANES, (c + 1) * LANES)
    wv = [jnp.broadcast_to(w_ref[j:j + 1, lanes], (SUBLANES, LANES)) for j in range(width)]
    acc = [None] * nseg
    for v in range(nseg + width - 1):
        yv = f_ref[c, pl.ds(v + off, SUBLANES, stride=nseg), :]
        for u in range(max(0, v - (width - 1)), min(nseg - 1, v) + 1):
            term = wv[v - u] * yv
            acc[u] = term if acc[u] is None else acc[u] + term
    cb = cb_ref[:, lanes]
    for u in range(nseg):
        y_ref[c, pl.ds(u, SUBLANES, stride=nseg), :] = acc[u] + cb


def _conv_seq_kernel(av_ref, ag_ref, st_ref, w_ref, cb_ref, lw_ref, lb_ref, o_ref, ns_ref, f_ref, y_ref,
                     *, ts, width):
    i = pl.program_id(1)
    off = CONV_HALO - (width - 1)
    nchunk = f_ref.shape[0]
    chunks = [slice(c * LANES, (c + 1) * LANES) for c in range(nchunk)]

    @pl.when(i == 0)
    def _():
        for c, lanes in enumerate(chunks):
            f_ref[c, 0:off, :] = jnp.zeros((off, LANES), F32)
            f_ref[c, off:CONV_HALO, :] = st_ref[:, lanes]

    u = av_ref[...] * _sigmoid(ag_ref[...])
    for c, lanes in enumerate(chunks):
        f_ref[c, CONV_HALO:CONV_HALO + ts, :] = u[:, lanes]
    for c in range(nchunk):
        _conv_taps_strided(f_ref, w_ref, cb_ref, y_ref, off, ts, c, width)
    y = jnp.concatenate([y_ref[c] for c in range(nchunk)], axis=1)
    o_ref[...] = _ln_swish(y, lw_ref[...], lb_ref[...]).astype(o_ref.dtype)

    @pl.when(i == pl.num_programs(1) - 1)
    def _():
        for c, lanes in enumerate(chunks):
            ns_ref[:, lanes] = f_ref[c, ts + off:ts + CONV_HALO, :]

    for c in range(nchunk):
        f_ref[c, 0:CONV_HALO, :] = f_ref[c, ts:ts + CONV_HALO, :]


def _conv_seq(proj, state, w, cb, lw, lb, bsz, s):
    width, ch = w.shape
    assert proj.shape[2] == ch
    ts = _pick(s, (128, 64, 32))
    ns = s // ts
    assert ch % LANES == 0 and ts >= CONV_HALO >= width - 1
    kern = functools.partial(_conv_seq_kernel, ts=ts, width=width)
    vec = pl.BlockSpec((1, ch), lambda b, i: (0, 0))
    return pl.pallas_call(
        kern,
        grid=(bsz, ns),
        in_specs=[pl.BlockSpec((None, ts, ch), lambda b, i: (0, b * ns + i, 0)),
                  pl.BlockSpec((None, ts, ch), lambda b, i: (1, b * ns + i, 0)),
                  pl.BlockSpec((None, width - 1, ch), lambda b, i: (b, 0, 0)),
                  pl.BlockSpec((width, ch), lambda b, i: (0, 0)),
                  vec, vec, vec],
        out_specs=[pl.BlockSpec((ts, ch), lambda b, i: (b * ns + i, 0)),
                   pl.BlockSpec((None, width - 1, ch), lambda b, i: (b, 0, 0))],
        out_shape=[jax.ShapeDtypeStruct((bsz * s, ch), BF16),
                   jax.ShapeDtypeStruct((bsz, width - 1, ch), F32)],
        scratch_shapes=[pltpu.VMEM((ch // LANES, CONV_HALO + ts, LANES), F32),
                        pltpu.VMEM((ch // LANES, ts, LANES), F32)],
        compiler_params=_params(("arbitrary", "arbitrary")),
        name="conv_seq",
    )(proj, proj, state, w, cb, lw, lb)


def _conv_step_kernel(av_ref, ag_ref, st_ref, w_ref, cb_ref, lw_ref, lb_ref, o_ref, ns_ref, f_ref, y_ref,
                      *, nb, s, width, cc):
    nst = width - 1
    ch = f_ref.shape[1]

    def body(q, carry):
        r = pl.multiple_of(q * s, s)
        f_ref[0:nst, :] = st_ref[q]
        f_ref[nst:nst + s, :] = av_ref[pl.ds(r, s), :] * _sigmoid(ag_ref[pl.ds(r, s), :])
        for c0 in range(0, ch, cc):
            acc = _conv_taps(f_ref, w_ref, 0, s, c0, cc, width)
            y_ref[pl.ds(r, s), c0:c0 + cc] = acc + cb_ref[:, c0:c0 + cc]
        ns_ref[q] = f_ref[s:s + nst, :]
        return carry

    lax.fori_loop(0, nb, body, 0)
    o_ref[...] = _ln_swish(y_ref[...], lw_ref[...], lb_ref[...]).astype(o_ref.dtype)


def _conv_step(proj, state, w, cb, lw, lb, bsz, s):
    width, ch = w.shape
    assert proj.shape[2] == ch
    assert s % SUBLANES == 0
    nb = _pick(bsz, (16, 8, 4, 2, 1))
    cc = _pick(ch, (512, 256, 128))
    kern = functools.partial(_conv_step_kernel, nb=nb, s=s, width=width, cc=cc)
    vec = pl.BlockSpec((1, ch), lambda b: (0, 0))
    frows = -(-(width - 1 + s) // SUBLANES) * SUBLANES
    return pl.pallas_call(
        kern,
        grid=(bsz // nb,),
        in_specs=[pl.BlockSpec((None, nb * s, ch), lambda b: (0, b, 0)),
                  pl.BlockSpec((None, nb * s, ch), lambda b: (1, b, 0)),
                  pl.BlockSpec((nb, width - 1, ch), lambda b: (b, 0, 0)),
                  pl.BlockSpec((width, ch), lambda b: (0, 0)),
                  vec, vec, vec],
        out_specs=[pl.BlockSpec((nb * s, ch), lambda b: (b, 0)),
                   pl.BlockSpec((nb, width - 1, ch), lambda b: (b, 0, 0))],
        out_shape=[jax.ShapeDtypeStruct((bsz * s, ch), BF16),
                   jax.ShapeDtypeStruct((bsz, width - 1, ch), F32)],
        scratch_shapes=[pltpu.VMEM((frows, ch), F32), pltpu.VMEM((nb * s, ch), F32)],
        compiler_params=_params(("parallel",)),
        name="conv_step",
    )(proj, proj, state, w, cb, lw, lb)


def _mlstm_kernel(qk_ref, v_ref, o_ref, g_ref, gt_ref, c0_ref, n0_ref, m0_ref, nw_ref,
                  hm_ref, c_out, n_out, m_out, c_s, n_s, m_s, *, n_heads, scale, nb):
    c = pl.program_id(1)
    last = pl.num_programs(1) - 1
    ln = qk_ref.shape[0] // nb
    wqk = qk_ref.shape[1] // 2
    dqk = wqk // n_heads
    dv = v_ref.shape[1] // n_heads

    @pl.when(c == 0)
    def _():
        c_s[...] = c0_ref[...]
        n_s[...] = n0_ref[...]
        m_s[...] = m0_ref[...]

    row = lax.broadcasted_iota(I32, (ln, ln), 0)
    col = lax.broadcasted_iota(I32, (ln, ln), 1)
    tri = row >= col
    tri_t = row <= col

    def chain(bi, hd):
        rows = slice(bi * ln, (bi + 1) * ln)
        g = g_ref[rows, :]
        gt = gt_ref[bi]
        li_c = g[:, hd:hd + 1]
        lf_c = g[:, n_heads + hd:n_heads + hd + 1]
        li_r = gt[hd:hd + 1, :]
        lf_r = gt[n_heads + hd:n_heads + hd + 1, :]
        b_c = jnp.sum(jnp.where(tri, lf_r, 0.0), axis=1, keepdims=True)
        b_r = jnp.sum(jnp.where(tri_t, lf_c, 0.0), axis=0, keepdims=True)
        b_l = jnp.sum(lf_r, axis=1, keepdims=True)
        m_prev = m_s[bi, :, hd:hd + 1]

        dmat = jnp.where(tri, b_c - b_r + li_r, NEG_BIG)
        inter = b_c + m_prev
        m_t = jnp.maximum(inter, jnp.max(dmat, axis=1, keepdims=True))
        a = jnp.exp(inter - m_t)

        q = qk_ref[rows, hd * dqk:(hd + 1) * dqk]
        k = qk_ref[rows, wqk + hd * dqk:wqk + (hd + 1) * dqk] * scale
        qb = q.astype(BF16)
        vb = v_ref[rows, hd * dv:(hd + 1) * dv].astype(BF16)
        cst = c_s[bi, hd]
        nst = n_s[bi, hd:hd + 1, :]
        s = lax.dot_general(qb, k.astype(BF16), (((1,), (1,)), ((), ())), preferred_element_type=F32)
        s = s * jnp.exp(dmat - m_t)
        num = a * jnp.dot(qb, cst.astype(BF16), preferred_element_type=F32) \
            + jnp.dot(s.astype(BF16), vb, preferred_element_type=F32)
        den = a * jnp.sum(q * nst, axis=1, keepdims=True) + jnp.sum(s, axis=1, keepdims=True)
        h = num / jnp.maximum(jnp.abs(den), jnp.exp(-m_t))
        hn = h * lax.rsqrt(jnp.mean(h * h, -1, keepdims=True) + EPS) * nw_ref[hd:hd + 1, :]
        gate = _sigmoid(o_ref[rows, hd * dv:(hd + 1) * dv])
        hm_ref[rows, hd * dv:(hd + 1) * dv] = (hn * gate).astype(hm_ref.dtype)

        g_r = b_l - b_r + li_r
        g_c = b_l - b_c + li_c
        m_new = jnp.maximum(b_l + m_prev, jnp.max(g_r, axis=1, keepdims=True))
        decay = jnp.exp(b_l + m_prev - m_new)
        kw = k * jnp.exp(g_c - m_new)
        c_new = decay * cst + lax.dot_general(kw.astype(BF16), vb, (((0,), (0,)), ((), ())),
                                              preferred_element_type=F32)
        n_new = decay * nst + jnp.sum(kw, axis=0, keepdims=True)
        c_s[bi, hd] = c_new
        n_s[bi, hd:hd + 1, :] = n_new
        m_s[bi, :, hd:hd + 1] = m_new

    for bi in range(nb):
        for hd in range(n_heads):
            chain(bi, hd)

    @pl.when(c == last)
    def _():
        c_out[...] = c_s[...]
        n_out[...] = n_s[...]
        m_out[...] = m_s[...]


def _mlstm(proj, g, c0, n0, m0, norm_w, bsz, s, slab0):
    _, n_heads, dqk, dv = c0.shape
    ln = s if s <= MLSTM_CHUNK else MLSTM_CHUNK
    assert s % ln == 0 and ln % SUBLANES == 0
    nc = s // ln
    wqk, wv = n_heads * dqk, n_heads * dv
    slab = proj.shape[2]
    assert 2 * wqk == slab and wv == slab
    gt3 = g[:, :2 * n_heads].reshape(bsz * nc, ln, 2 * n_heads).transpose(0, 2, 1)
    nb = _pick(bsz, (MLSTM_SEQS, 2, 1)) if nc == 1 else 1
    kern = functools.partial(_mlstm_kernel, n_heads=n_heads, scale=dqk ** -0.5, nb=nb)
    rows = lambda col: (lambda b, c: (b * nc + c, col))
    slab_rows = lambda k: pl.BlockSpec((None, nb * ln, slab), lambda b, c: (slab0 + k, b * nc + c, 0))
    per_seq = lambda shape: pl.BlockSpec((nb,) + shape, lambda b, c: (b,) + (0,) * len(shape))
    out = pl.pallas_call(
        kern,
        grid=(bsz // nb, nc),
        in_specs=[slab_rows(0), slab_rows(1), slab_rows(2),
                  pl.BlockSpec((nb * ln, LANES), rows(0)),
                  pl.BlockSpec((nb, 2 * n_heads, ln), lambda b, c: (b * nc + c, 0, 0)),
                  per_seq((n_heads, dqk, dv)), per_seq((n_heads, dqk)), per_seq((1, n_heads)),
                  pl.BlockSpec((n_heads, dv), lambda b, c: (0, 0))],
        out_specs=[pl.BlockSpec((nb * ln, wv), rows(0)),
                   per_seq((n_heads, dqk, dv)), per_seq((n_heads, dqk)), per_seq((1, n_heads))],
        out_shape=[jax.ShapeDtypeStruct((bsz * s, wv), BF16),
                   jax.ShapeDtypeStruct((bsz, n_heads, dqk, dv), F32),
                   jax.ShapeDtypeStruct((bsz, n_heads, dqk), F32),
                   jax.ShapeDtypeStruct((bsz, 1, n_heads), F32)],
        scratch_shapes=[pltpu.VMEM((nb, n_heads, dqk, dv), F32), pltpu.VMEM((nb, n_heads, dqk), F32),
                        pltpu.VMEM((nb, 1, n_heads), F32)],
        compiler_params=_params(("arbitrary", "arbitrary")),
        name="mlstm",
    )(proj, proj, proj, g, gt3, c0, n0, m0.reshape(bsz, 1, n_heads), norm_w)
    hm, c_new, n_new, m_new = out
    return hm, c_new, n_new, m_new.reshape(bsz, n_heads)


def _outproj_kernel(co_ref, hm_ref, w_ref, o_ref):
    kc = co_ref.shape[1]
    o_ref[...] = jnp.dot(co_ref[...], w_ref[0:kc, :], preferred_element_type=F32) \
        + jnp.dot(hm_ref[...], w_ref[kc:2 * kc, :], preferred_element_type=F32)


def _outproj(co, hm, w_bf):
    m, kc = co.shape
    d = w_bf.shape[-1]
    assert hm.shape[1] == kc and w_bf.shape[0] == 2 * kc
    tn = _pick(d, (2048, 1024, 512, 256, 128))
    tm = _pick(m, (512, 256, 128, 64, 32, 16, 8))
    return pl.pallas_call(
        _outproj_kernel,
        grid=(d // tn, m // tm),
        in_specs=[pl.BlockSpec((tm, kc), lambda j, i: (i, 0)),
                  pl.BlockSpec((tm, kc), lambda j, i: (i, 0)),
                  pl.BlockSpec((2 * kc, tn), lambda j, i: (0, j))],
        out_specs=pl.BlockSpec((tm, tn), lambda j, i: (i, j)),
        out_shape=jax.ShapeDtypeStruct((m, d), F32),
        compiler_params=_params(("parallel", "parallel")),
        name="outproj",
    )(co, hm, w_bf)


def _router_kernel(mix_ref, x_ref, g1_ref, sc2_ref, sh2_ref, n1_ref, n2_ref, wr_ref, br_ref,
                   x1_ref, h2_ref, ti_ref, tg_ref, *, st):
    mix = mix_ref[...]
    mn = mix * lax.rsqrt(jnp.mean(mix * mix, -1, keepdims=True) + EPS) * n1_ref[...]
    x1 = x_ref[...] + _mod_rows(g1_ref, st) * mn
    x1_ref[...] = x1
    y2 = x1 * lax.rsqrt(jnp.mean(x1 * x1, -1, keepdims=True) + EPS) * n2_ref[...]
    h2 = y2 * (1.0 + _mod_rows(sc2_ref, st)) + _mod_rows(sh2_ref, st)
    half = h2.shape[1] // 2
    h2_ref[...] = _pack_pair(h2[:, :half], h2[:, half:])
    logits = jnp.dot(h2.astype(BF16), wr_ref[...], preferred_element_type=F32) + br_ref[...]
    n_exp = logits.shape[1]
    lane = lax.broadcasted_iota(I32, logits.shape, 1)
    lane_o = lax.broadcasted_iota(I32, ti_ref.shape, 1)
    idx_out = jnp.zeros(ti_ref.shape, I32)
    val_out = jnp.zeros(tg_ref.shape, F32)
    top = None
    den = jnp.zeros((logits.shape[0], 1), F32)
    for r in range(TOP_K):
        mx = jnp.max(logits, axis=1, keepdims=True)
        ix = jnp.min(jnp.where(logits == mx, lane, n_exp), axis=1, keepdims=True)
        if top is None:
            top = mx
        e = jnp.exp(mx - top)
        den = den + e
        idx_out = jnp.where(lane_o == r, ix, idx_out)
        val_out = jnp.where(lane_o == r, e, val_out)
        logits = jnp.where(lane == ix, NEG_BIG, logits)
    ti_ref[...] = idx_out
    tg_ref[...] = val_out / den


def _router(mix, x2, mod3, grp, n1, n2, wr, br):
    rows, d = x2.shape
    n_exp = wr.shape[1]
    kern = functools.partial(_router_kernel, st=grp.st)
    return pl.pallas_call(
        kern,
        grid=grp.grid,
        in_specs=[grp.rows(d), grp.rows(d), grp.mod(d, 2), grp.mod(d, 4), grp.mod(d, 3),
                  grp.const((1, d)), grp.const((1, d)), grp.const((d, n_exp)), grp.const((1, n_exp))],
        out_specs=[grp.rows(d), grp.rows(d // 2), grp.rows(LANES), grp.rows(LANES)],
        out_shape=[jax.ShapeDtypeStruct((rows, d), F32), jax.ShapeDtypeStruct((rows, d // 2), U32),
                   jax.ShapeDtypeStruct((rows, LANES), I32), jax.ShapeDtypeStruct((rows, LANES), F32)],
        compiler_params=_params(("parallel", "parallel")),
        name="router",
    )(mix, x2, mod3, mod3, mod3, n1, n2, wr, br)


def _rank_kernel(ti_ref, ps_ref, o_ref, carry_ref):
    @pl.when(pl.program_id(0) == 0)
    def _():
        carry_ref[...] = jnp.zeros(carry_ref.shape, F32)

    ti = ti_ref[...].astype(F32)
    tt = ti.shape[0]
    lane = lax.broadcasted_iota(I32, ti.shape, 1)
    lane_f = lane.astype(F32)
    cols = []
    member = jnp.zeros(ti.shape, F32)
    for kk in range(TOP_K):
        ek = jnp.sum(jnp.where(lane == kk, ti, 0.0), axis=1, keepdims=True)
        cols.append(ek)
        member = member + jnp.where(lane_f == ek, 1.0, 0.0)
    r = lax.broadcasted_iota(I32, (tt, tt), 0)
    c = lax.broadcasted_iota(I32, (tt, tt), 1)
    before = jnp.where(r > c, 1.0, 0.0).astype(BF16)
    base = jnp.dot(before, member.astype(BF16), preferred_element_type=F32) + carry_ref[...] + ps_ref[...]
    out = jnp.zeros(o_ref.shape, I32)
    for kk in range(TOP_K):
        dk = jnp.sum(jnp.where(lane_f == cols[kk], base, 0.0), axis=1, keepdims=True)
        out = jnp.where(lane == kk, dk.astype(I32), out)
    o_ref[...] = out
    carry_ref[...] += jnp.sum(member, axis=0, keepdims=True)


def _rank(top_i, pad_start):
    t = top_i.shape[0]
    tt = _pick(t, (512, 256, 128, 64, 32, 16, 8))
    return pl.pallas_call(
        _rank_kernel,
        grid=(t // tt,),
        in_specs=[pl.BlockSpec((tt, LANES), lambda i: (i, 0)), pl.BlockSpec((1, LANES), lambda i: (0, 0))],
        out_specs=pl.BlockSpec((tt, LANES), lambda i: (i, 0)),
        out_shape=jax.ShapeDtypeStruct((t, LANES), I32),
        scratch_shapes=[pltpu.VMEM((1, LANES), F32)],
        compiler_params=_params(("arbitrary",)),
        name="moe_rank",
    )(top_i, pad_start)


def _route(top_i, n_experts, tm):
    t = top_i.shape[0]
    a = t * TOP_K
    flat_e = top_i[:, :TOP_K].reshape(a)
    counts = jnp.sum((flat_e[:, None] == jnp.arange(n_experts, dtype=I32)[None, :]).astype(I32), axis=0)
    padded = (counts + tm - 1) // tm * tm
    pad_end = jnp.cumsum(padded)
    pad_start = pad_end - padded
    nt = a // tm + n_experts
    tile_start = jnp.arange(nt, dtype=I32) * tm
    tile_u = (tile_start < pad_end[-1]).astype(I32)
    n_used = jnp.sum(tile_u)
    tile_e = jnp.sum((pad_end[None, :] <= tile_start[:, None]).astype(I32), axis=1)
    tile_e = jnp.minimum(tile_e, n_experts - 1)
    tile_e = jnp.where(tile_u == 1, tile_e, tile_e[jnp.maximum(n_used - 1, 0)])
    idx = jnp.arange(nt, dtype=I32)
    starts = jnp.logical_and(tile_u == 1, jnp.logical_or(idx == 0, tile_e != jnp.roll(tile_e, 1)))
    seg = (jnp.cumsum(starts.astype(I32)) - 1).astype(I32)
    later = lax.cummin(jnp.where(starts, idx, nt)[::-1])[::-1]
    nxt = jnp.concatenate([later[1:], jnp.full((1,), nt, I32)])
    next_e = tile_e[jnp.where(nxt >= nt, 0, nxt)]
    n_seg = jnp.sum(starts.astype(I32)).reshape(1)
    x_tile = jnp.minimum(idx, jnp.maximum(n_used - 1, 0))
    tiles = (tile_e, tile_u, seg, next_e, n_seg, x_tile)
    fill = (jnp.concatenate([pad_start + counts, pad_end[-1:]]).astype(I32),
            jnp.concatenate([padded - counts, nt * tm - pad_end[-1:]]).astype(I32))
    ps = jnp.zeros((1, LANES), F32).at[0, :n_experts].set(pad_start.astype(F32))
    dest = _rank(top_i, ps)
    return dest, tiles, fill, nt


def _row_copy(src_ref, dst_ref, sem, src_row, dst_row):
    return pltpu.make_async_copy(src_ref.at[pl.ds(src_row, 1), :], dst_ref.at[pl.ds(dst_row, 1), :], sem)


def _dispatch_kernel(fs_ref, fn_ref, dest_ref, srcp_ref, srcs_ref, xs_out, zrow_ref, sem, *, ntp, nts, tm):
    i = pl.program_id(0)

    def send(src_ref):
        def start(r, carry):
            for kk in range(TOP_K):
                _row_copy(src_ref, xs_out, sem, r, dest_ref[0, r * TOP_K + kk]).start(priority=kk % 2)
            return carry

        lax.fori_loop(0, src_ref.shape[0], start, 0, unroll=ISSUE_UNROLL)
        for _ in range(TOP_K):
            pltpu.make_async_copy(src_ref, src_ref, sem).wait()

    @pl.when(i < ntp)
    def _():
        send(srcp_ref)

    @pl.when(jnp.logical_and(i >= ntp, i < ntp + nts))
    def _():
        send(srcs_ref)

    f = i - (ntp + nts)
    n_fill = pl.num_programs(0) - (ntp + nts)

    @pl.when(f == 0)
    def _():
        zrow_ref[...] = jnp.zeros(zrow_ref.shape, zrow_ref.dtype)

    def zero_rows(row, size):
        dst = row if size < SUBLANES else pl.multiple_of(row, SUBLANES)
        return pltpu.make_async_copy(zrow_ref.at[pl.ds(0, size), :], xs_out.at[pl.ds(dst, size), :], sem)

    def pad_run(row0, n, go):
        odd = n & (SUBLANES - 1)
        for r in range(SUBLANES - 1):
            @pl.when(odd > r)
            def _():
                go(zero_rows(row0 + r, 1))
        pos = row0 + odd
        size = SUBLANES
        while size < tm:
            bit = (n // size) & 1

            @pl.when(bit == 1)
            def _():
                go(zero_rows(pos, size))
            pos = pos + bit * size
            size *= 2

    def tail_run(row0, n, go):
        def body(t, carry):
            go(zero_rows(row0 + t * tm, tm))
            return carry
        lax.fori_loop(0, n // tm, body, 0)

    for go in (lambda cp: cp.start(), lambda cp: cp.wait()):
        @pl.when(jnp.logical_and(f >= 0, f < n_fill - 1))
        def _():
            pad_run(fs_ref[f], fn_ref[f], go)

        @pl.when(f == n_fill - 1)
        def _():
            tail_run(fs_ref[f], fn_ref[f], go)


def _dispatch(h2p_p, h2p_s, dest, fill, rows, tm):
    fill_start, fill_n = fill
    tp, w = h2p_p.shape
    tsm = h2p_s.shape[0]
    tt = _pick(tsm, (256, 128, 64, 32, 16, 8))
    assert tp % tt == 0
    ntp, nts = tp // tt, tsm // tt
    kern = functools.partial(_dispatch_kernel, ntp=ntp, nts=nts, tm=tm)
    grid_spec = pltpu.PrefetchScalarGridSpec(
        num_scalar_prefetch=2,
        grid=(ntp + nts + fill_n.shape[0],),
        in_specs=[pl.BlockSpec((None, 1, tt * TOP_K), lambda i, *_: (jnp.minimum(i, ntp + nts - 1), 0, 0),
                               memory_space=pltpu.SMEM),
                  pl.BlockSpec((tt, w), lambda i, *_: (jnp.minimum(i, ntp - 1), 0)),
                  pl.BlockSpec((tt, w), lambda i, *_: (jnp.clip(i - ntp, 0, nts - 1), 0))],
        out_specs=pl.BlockSpec(memory_space=pl.ANY),
        scratch_shapes=[pltpu.VMEM((tm, w), U32), pltpu.SemaphoreType.DMA(())])
    return pl.pallas_call(
        kern,
        grid_spec=grid_spec,
        out_shape=jax.ShapeDtypeStruct((rows, w), U32),
        compiler_params=_params(("arbitrary",)),
        name="moe_dispatch",
    )(fill_start, fill_n, dest[:, :TOP_K].reshape(ntp + nts, 1, tt * TOP_K), h2p_p, h2p_s)


N_TILE_TABLES = 6


def _segment_weights(tables, w_hbm, wst_ref, sem, *, l, tn, col_offs):
    te_ref, tu_ref, sg_ref, ne_ref, ns_ref = tables[:5]
    j = pl.program_id(0)
    i = pl.program_id(1)
    n_seg = ns_ref[0]
    first = jnp.logical_and(tu_ref[i] == 1,
                            jnp.logical_or(i == 0, te_ref[i] != te_ref[jnp.maximum(i - 1, 0)]))
    g = j * n_seg + sg_ref[i]

    def copies(e, jj):
        return [pltpu.make_async_copy(
            w_hbm.at[l, e, :, pl.ds(pl.multiple_of((off + jj) * tn, tn), tn)], wst_ref.at[m], sem)
            for m, off in enumerate(col_offs)]

    @pl.when(jnp.logical_and(j == 0, i == 0))
    def _():
        for cp in copies(te_ref[0], 0):
            cp.start(priority=WEIGHT_DMA_PRIORITY)

    def wait_weights():
        for cp in copies(te_ref[i], j):
            cp.wait()

    def start_next():
        @pl.when(g + 1 < pl.num_programs(0) * n_seg)
        def _():
            jn = jnp.where(sg_ref[i] + 1 == n_seg, j + 1, j)
            for cp in copies(ne_ref[i], jn):
                cp.start(priority=WEIGHT_DMA_PRIORITY)

    return first, wait_weights, start_next


def _gmm1_kernel(*refs, l, tn, nj):
    tables = refs[:N_TILE_TABLES]
    x_ref, w_hbm, bg_ref, bu_ref, o_ref, wst_ref, wgb_ref, wub_ref, sem = refs[N_TILE_TABLES:]
    used = tables[1][pl.program_id(1)] == 1
    first, wait_weights, start_next = _segment_weights(tables, w_hbm, wst_ref, sem, l=l, tn=tn,
                                                       col_offs=(0, nj))
    half = x_ref.shape[1]

    def tile(wg_lo, wg_hi, wu_lo, wu_hi):
        xlo, xhi = _unpack_pair(x_ref[...])
        gt = jnp.dot(xlo, wg_lo, preferred_element_type=F32) \
            + jnp.dot(xhi, wg_hi, preferred_element_type=F32) + bg_ref[...]
        up = jnp.dot(xlo, wu_lo, preferred_element_type=F32) \
            + jnp.dot(xhi, wu_hi, preferred_element_type=F32) + bu_ref[...]
        gt = jnp.minimum(gt, SWIGLU_LIMIT)
        up = jnp.clip(up, -SWIGLU_LIMIT, SWIGLU_LIMIT)
        act = (up + 1.0) * gt * _sigmoid(SWIGLU_ALPHA * gt)
        o_ref[...] = act.astype(o_ref.dtype)

    @pl.when(first)
    def _():
        wait_weights()
        ws = []
        for m, dst in enumerate((wgb_ref, wub_ref)):
            for k0 in (0, half):
                wk = wst_ref[m, k0:k0 + half, :].astype(BF16)
                dst[k0:k0 + half, :] = wk
                ws.append(wk)
        tile(*ws)
        start_next()

    @pl.when(jnp.logical_and(used, jnp.logical_not(first)))
    def _():
        tile(wgb_ref[0:half, :], wgb_ref[half:2 * half, :], wub_ref[0:half, :], wub_ref[half:2 * half, :])

    @pl.when(jnp.logical_not(used))
    def _():
        o_ref[...] = jnp.zeros(o_ref.shape, o_ref.dtype)


def _gmm1(xs, w1, b1, tiles, l, tm):
    rows, half = xs.shape
    d = 2 * half
    dff = w1.shape[-1] // 2
    nt = rows // tm
    tn = _pick(dff, (1024, 512, 256, 128))
    nj = dff // tn
    b1r = b1.reshape(b1.shape[0], b1.shape[1], 1, 2 * dff)
    kern = functools.partial(_gmm1_kernel, l=l, tn=tn, nj=nj)
    grid_spec = pltpu.PrefetchScalarGridSpec(
        num_scalar_prefetch=N_TILE_TABLES,
        grid=(nj, nt),
        in_specs=[pl.BlockSpec((tm, half), lambda j, i, *t: (t[5][i], 0)),
                  pl.BlockSpec(memory_space=pl.ANY),
                  pl.BlockSpec((None, None, 1, tn), lambda j, i, te, *t: (l, te[i], 0, j)),
                  pl.BlockSpec((None, None, 1, tn), lambda j, i, te, *t: (l, te[i], 0, nj + j))],
        out_specs=pl.BlockSpec((tm, tn), lambda j, i, *t: (i, j)),
        scratch_shapes=[pltpu.VMEM((2, d, tn), F32), pltpu.VMEM((d, tn), BF16), pltpu.VMEM((d, tn), BF16),
                        pltpu.SemaphoreType.DMA(())])
    return pl.pallas_call(
        kern,
        grid_spec=grid_spec,
        out_shape=jax.ShapeDtypeStruct((rows, dff), BF16),
        compiler_params=_params(("arbitrary", "arbitrary"), GMM_VMEM_LIMIT),
        name="moe_gmm1",
    )(*tiles, xs, w1, b1r, b1r)


def _gmm2_kernel(*refs, l, tn):
    tables = refs[:N_TILE_TABLES]
    a_ref, w_hbm, b_ref, o_ref, wst_ref, wb_ref, sem = refs[N_TILE_TABLES:]
    used = tables[1][pl.program_id(1)] == 1
    first, wait_weights, start_next = _segment_weights(tables, w_hbm, wst_ref, sem, l=l, tn=tn, col_offs=(0,))

    def tile(w):
        y = jnp.dot(a_ref[...], w, preferred_element_type=F32) + b_ref[...]
        o_ref[...] = _pack_pair(y[:, :tn // 2], y[:, tn // 2:])

    @pl.when(first)
    def _():
        wait_weights()
        w = wst_ref[0].astype(BF16)
        wb_ref[...] = w
        tile(w)
        start_next()

    @pl.when(jnp.logical_and(used, jnp.logical_not(first)))
    def _():
        tile(wb_ref[...])

    @pl.when(jnp.logical_not(used))
    def _():
        o_ref[...] = jnp.zeros(o_ref.shape, o_ref.dtype)


def _gmm2(act, w2, b2, tiles, l, tm):
    rows, dff = act.shape
    d = w2.shape[-1]
    nt = rows // tm
    tn = _pick(d, (4096, 2048, 1024, 512, 256, 128))
    b2r = b2.reshape(b2.shape[0], b2.shape[1], 1, d)
    kern = functools.partial(_gmm2_kernel, l=l, tn=tn)
    grid_spec = pltpu.PrefetchScalarGridSpec(
        num_scalar_prefetch=N_TILE_TABLES,
        grid=(d // tn, nt),
        in_specs=[pl.BlockSpec((tm, dff), lambda j, i, *t: (i, 0)),
                  pl.BlockSpec(memory_space=pl.ANY),
                  pl.BlockSpec((None, None, 1, tn), lambda j, i, te, *t: (l, te[i], 0, j))],
        out_specs=pl.BlockSpec((tm, tn // 2), lambda j, i, *t: (i, j)),
        scratch_shapes=[pltpu.VMEM((1, dff, tn), F32), pltpu.VMEM((dff, tn), BF16),
                        pltpu.SemaphoreType.DMA(())])
    y = pl.pallas_call(
        kern,
        grid_spec=grid_spec,
        out_shape=jax.ShapeDtypeStruct((rows, d // 2), U32),
        compiler_params=_params(("arbitrary", "arbitrary"), GMM_VMEM_LIMIT),
        name="moe_gmm2",
    )(*tiles, act, w2, b2r)
    return y, tn // 2


def _combine_kernel(pos_ref, posn_ref, y_hbm, tg_ref, x1_ref, g2_ref, nw_ref, o_ref, buf_ref, sem, *, st, pw):
    ni = pl.num_programs(1)
    n = pl.program_id(0) * ni + pl.program_id(1)
    total = pl.num_programs(0) * ni
    tt, d = o_ref.shape
    slot = lax.rem(n, 2)

    def start_rows(idx_ref, sl):
        def body(r, carry):
            for kk in range(TOP_K):
                cp = _row_copy(y_hbm, buf_ref.at[sl, kk], sem.at[sl], idx_ref[0, r * TOP_K + kk], r)
                cp.start(priority=kk % 2)
            return carry
        lax.fori_loop(0, tt, body, 0, unroll=ISSUE_UNROLL)

    @pl.when(n == 0)
    def _():
        start_rows(pos_ref, 0)

    @pl.when(n + 1 < total)
    def _():
        start_rows(posn_ref, 1 - slot)

    pltpu.make_async_copy(buf_ref.at[slot], buf_ref.at[slot], sem.at[slot]).wait()

    gates = tg_ref[...]
    lane = lax.broadcasted_iota(I32, gates.shape, 1)
    gk = [jnp.sum(jnp.where(lane == kk, gates, 0.0), axis=1, keepdims=True) for kk in range(TOP_K)]
    ssq = jnp.zeros((tt, 1), F32)
    for blk in range(d // (2 * pw)):
        lo = jnp.zeros((tt, pw), F32)
        hi = jnp.zeros((tt, pw), F32)
        for kk in range(TOP_K):
            p = buf_ref[slot, kk, :, blk * pw:(blk + 1) * pw]
            lo = lo + gk[kk] * lax.bitcast_convert_type(p << 16, F32)
            hi = hi + gk[kk] * lax.bitcast_convert_type(p & jnp.uint32(HI_HALF), F32)
        ssq = ssq + jnp.sum(lo * lo, -1, keepdims=True) + jnp.sum(hi * hi, -1, keepdims=True)
        o_ref[:, 2 * pw * blk:2 * pw * blk + pw] = lo
        o_ref[:, 2 * pw * blk + pw:2 * pw * (blk + 1)] = hi
    fn = o_ref[...] * lax.rsqrt(ssq / d + EPS) * nw_ref[...]
    o_ref[...] = x1_ref[...] + _mod_rows(g2_ref, st) * fn


def _combine(y, pw, dest, tg, x1, mod3, grp, nw):
    rows, d = x1.shape
    tt = grp.ts
    nt = rows // tt
    ni = grp.n_inner
    kern = functools.partial(_combine_kernel, st=grp.st, pw=pw)
    dest3 = dest[:, :TOP_K].reshape(nt, 1, tt * TOP_K)
    idx_spec = lambda ahead: pl.BlockSpec(
        (None, 1, tt * TOP_K), lambda o, i: (jnp.minimum(o * ni + i + ahead, nt - 1), 0, 0),
        memory_space=pltpu.SMEM)
    return pl.pallas_call(
        kern,
        grid=grp.grid,
        in_specs=[idx_spec(0), idx_spec(1),
                  pl.BlockSpec(memory_space=pl.ANY),
                  grp.rows(LANES), grp.rows(d), grp.mod(d, 5), grp.const((1, d))],
        out_specs=grp.rows(d),
        out_shape=jax.ShapeDtypeStruct((rows, d), F32),
        scratch_shapes=[pltpu.VMEM((2, TOP_K, tt, d // 2), U32), pltpu.SemaphoreType.DMA((2,))],
        compiler_params=_params(("arbitrary", "arbitrary")),
        name="moe_combine",
    )(dest3, dest3, y, tg, x1, mod3, nw)


def _mixer(x2, mod3, grp, conv_buf, c0, n0, m0, p, l):
    n_heads = c0.shape[1]
    ch = p['conv_w'].shape[-1]
    h, g = _prenorm(x2, mod3, grp, p['norm1_pre'], p['w_gate'], p['b_gate'], n_heads)
    proj = _inproj(h, p['w_in'], ch)
    conv = _conv_seq if grp.nbq == 1 else _conv_step
    co, new_buf = conv(proj, conv_buf, p['conv_w'], p['conv_b'], p['conv_ln_w'], p['conv_ln_b'], grp.bsz, grp.s)
    hm, c_new, n_new, m_new = _mlstm(proj, g, c0, n0, m0, p['mlstm_norm_w'], grp.bsz, grp.s, 2)
    mix = _outproj(co, hm, p['w_out'])
    x1, h2p, ti, tg = _router(mix, x2, mod3, grp, p['norm1_post'], p['norm2_pre'], p['w_router'], p['b_router'])
    return dict(x1=x1, h2p=h2p, ti=ti, tg=tg, state=(new_buf, c_new, n_new, m_new))


def kernel(x_prompt, x_sample, c_prompt, c_sample, state_conv, state_mlstm_C, state_mlstm_n, state_mlstm_m,
           w_ada, b_ada, norm1_pre, w_in, b_gates, conv_w, conv_b, conv_ln_w, conv_ln_b, mlstm_norm_w,
           w_out, norm1_post, norm2_pre, w_router, b_router, w1, b1, w2, b2, norm2_post):
    depth = w_ada.shape[0]
    bp, sp, d = x_prompt.shape
    bs, ss, _ = x_sample.shape
    n_heads, dqk, dv = state_mlstm_C.shape[2:]
    n_experts = w_router.shape[-1]
    nst, ch = state_conv.shape[2:]
    n_gate = 2 * n_heads
    assert n_gate <= LANES and n_experts <= LANES
    tp, tsmp = bp * sp, bs * ss

    mp = -(-(bp + bs) // SUBLANES) * SUBLANES
    c_all = jnp.zeros((mp, d), F32).at[:bs].set(c_sample).at[bs:bs + bp].set(c_prompt)
    grp_p = _Group(bp, sp, bs, ROW_TILE)
    grp_s = _Group(bs, ss, 0, ROW_TILE)

    xp, xs = x_prompt.reshape(tp, d), x_sample.reshape(tsmp, d)
    outs = [[] for _ in range(8)]
    for l in range(depth):
        row = lambda v: v[l].reshape(1, -1)
        p = dict(
            w_in=w_in[l, :, :w_in.shape[-1] - n_gate].astype(BF16), w_out=w_out[l].astype(BF16),
            w_gate=jnp.pad(w_in[l, :, w_in.shape[-1] - n_gate:], ((0, 0), (0, LANES - n_gate))),
            b_gate=jnp.pad(b_gates[l], (0, LANES - n_gate)).reshape(1, LANES),
            norm1_pre=row(norm1_pre), conv_w=conv_w[l], conv_b=row(conv_b), conv_ln_w=row(conv_ln_w),
            conv_ln_b=row(conv_ln_b), mlstm_norm_w=mlstm_norm_w[l], norm1_post=row(norm1_post),
            norm2_pre=row(norm2_pre), w_router=w_router[l].astype(BF16), b_router=row(b_router))
        mod3 = _ada(c_all, w_ada, b_ada, l).reshape(mp, 1, 6 * d)
        zero = lambda shape: jnp.zeros(shape, F32)
        gp = _mixer(xp, mod3, grp_p, zero((bp, nst, ch)), zero((bp, n_heads, dqk, dv)),
                    zero((bp, n_heads, dqk)), zero((bp, n_heads)), p, l)
        gs = _mixer(xs, mod3, grp_s, state_conv[l], state_mlstm_C[l], state_mlstm_n[l],
                    state_mlstm_m[l], p, l)

        top_i = jnp.concatenate([gp['ti'], gs['ti']], axis=0)
        dest, tiles, fill, nt = _route(top_i, n_experts, MOE_ROWS)
        xsorted = _dispatch(gp['h2p'], gs['h2p'], dest, fill, nt * MOE_ROWS, MOE_ROWS)
        act = _gmm1(xsorted, w1, b1, tiles, l, MOE_ROWS)
        y, pw = _gmm2(act, w2, b2, tiles, l, MOE_ROWS)
        nw2 = row(norm2_post)
        xp = _combine(y, pw, dest[:tp], gp['tg'], gp['x1'], mod3, grp_p, nw2)
        xs = _combine(y, pw, dest[tp:], gs['tg'], gs['x1'], mod3, grp_s, nw2)
        for o, v in zip(outs, gp['state'] + gs['state']):
            o.append(v)
    stack = (lambda o: o[0][None]) if depth == 1 else jnp.stack
    return (xp.reshape(bp, sp, d), xs.reshape(bs, ss, d)) + tuple(stack(o) for o in outs)
```

```python
import functools

import jax
import jax.numpy as jnp
from jax import lax
from jax.experimental import pallas as pl
from jax.experimental.pallas import tpu as pltpu

F32 = jnp.float32
BF16 = jnp.bfloat16
I32 = jnp.int32
U32 = jnp.uint32

EPS = 1e-6
GATE_CAP = 15.0
TOP_K = 4
SWIGLU_LIMIT = 7.0
SWIGLU_ALPHA = 1.702
NEG_BIG = -1e30

LANES = 128
SUBLANES = 8
VMEM_LIMIT = 56 * 1024 * 1024
GMM_VMEM_LIMIT = 60 * 1024 * 1024
MLSTM_CHUNK = 256
MLSTM_SEQS = 4
MOE_ROWS = 256
ROW_TILE = 256
HI_HALF = 0xFFFF0000
ISSUE_UNROLL = 4
COMBINE_ROWS = 16
WEIGHT_DMA_PRIORITY = 1


def _params(sem, vmem=VMEM_LIMIT):
    return pltpu.CompilerParams(dimension_semantics=sem, vmem_limit_bytes=vmem)


def _sigmoid(x):
    return 1.0 / (1.0 + jnp.exp(-x))


def _pick(n, prefs):
    for p in prefs:
        if n % p == 0:
            return p
    return n


def _pack_pair(lo, hi):
    lo_b = lax.bitcast_convert_type(lo.astype(BF16).astype(F32), U32) >> 16
    hi_b = lax.bitcast_convert_type(hi.astype(BF16).astype(F32), U32) & jnp.uint32(HI_HALF)
    return hi_b | lo_b


def _unpack_pair(p):
    lo = lax.bitcast_convert_type(p << 16, F32).astype(BF16)
    hi = lax.bitcast_convert_type(p & jnp.uint32(HI_HALF), F32).astype(BF16)
    return lo, hi


class _Group:
    def __init__(self, bsz, s, mod_row0, tile_rows):
        self.bsz, self.s = bsz, s
        if s >= tile_rows:
            self.nbq, self.st = 1, _pick(s, (tile_rows, 128, 64, 32, 16, 8))
            self.n_outer, self.n_inner = bsz, s // self.st
        else:
            self.nbq, self.st = _pick(bsz, (tile_rows // s, 8, 4, 2, 1)), s
            self.n_outer, self.n_inner = bsz // self.nbq, 1
        assert mod_row0 % self.nbq == 0
        self.mod_blk0 = mod_row0 // self.nbq
        self.ts = self.nbq * self.st
        self.grid = (self.n_outer, self.n_inner)

    def rows(self, width, col=0):
        ni = self.n_inner
        return pl.BlockSpec((self.ts, width), lambda o, i, *_: (o * ni + i, col))

    def mod(self, d, col):
        b0 = self.mod_blk0
        return pl.BlockSpec((self.nbq, 1, d), lambda o, i, *_: (b0 + o, 0, col))

    def const(self, shape):
        nd = len(shape)
        return pl.BlockSpec(shape, lambda o, i, *_: (0,) * nd)


def _mod_rows(m_ref, st):
    m = m_ref[...]
    nbq, _, d = m.shape
    if nbq == 1:
        return m[0]
    return jnp.broadcast_to(m, (nbq, st, d)).reshape(nbq * st, d)


def _ada_kernel(c_ref, w_ref, b_ref, o_ref):
    c = c_ref[...]
    s = (c * _sigmoid(c)).astype(BF16)
    o_ref[...] = jnp.dot(s, w_ref[...].astype(BF16), preferred_element_type=F32) + b_ref[...]


def _ada(c_all, w_ada, b_ada, l):
    mp, d = c_all.shape
    n = w_ada.shape[-1]
    tn = _pick(n, (512, 256, 128))
    return pl.pallas_call(
        _ada_kernel,
        grid=(n // tn,),
        in_specs=[pl.BlockSpec((mp, d), lambda j: (0, 0)),
                  pl.BlockSpec((None, d, tn), lambda j: (l, 0, j)),
                  pl.BlockSpec((None, 1, tn), lambda j: (l, 0, j))],
        out_specs=pl.BlockSpec((mp, tn), lambda j: (0, j)),
        out_shape=jax.ShapeDtypeStruct((mp, n), F32),
        compiler_params=_params(("parallel",)),
        name="ada",
    )(c_all, w_ada, b_ada.reshape(b_ada.shape[0], 1, n))


def _prenorm_kernel(x_ref, sc_ref, sh_ref, nw_ref, wg_ref, bg_ref, h_ref, g_ref, *, n_heads, st):
    x = x_ref[...]
    y = x * lax.rsqrt(jnp.mean(x * x, -1, keepdims=True) + EPS) * nw_ref[...]
    h = y * (1.0 + _mod_rows(sc_ref, st)) + _mod_rows(sh_ref, st)
    hb = h.astype(BF16)
    h_ref[...] = hb
    z = jnp.dot(hb, wg_ref[...].astype(BF16), preferred_element_type=F32) + bg_ref[...]
    cap = GATE_CAP * jnp.tanh(z / GATE_CAP)
    logsig = jnp.minimum(cap, 0.0) - jnp.log(1.0 + jnp.exp(-jnp.abs(cap)))
    lane = lax.broadcasted_iota(I32, z.shape, 1)
    g_ref[...] = jnp.where(lane < n_heads, cap, logsig)


def _prenorm(x2, mod3, grp, nw, wg, bg, n_heads):
    rows, d = x2.shape
    kern = functools.partial(_prenorm_kernel, n_heads=n_heads, st=grp.st)
    return pl.pallas_call(
        kern,
        grid=grp.grid,
        in_specs=[grp.rows(d), grp.mod(d, 1), grp.mod(d, 0), grp.const((1, d)),
                  grp.const((d, LANES)), grp.const((1, LANES))],
        out_specs=[grp.rows(d), grp.rows(LANES)],
        out_shape=[jax.ShapeDtypeStruct((rows, d), BF16), jax.ShapeDtypeStruct((rows, LANES), F32)],
        compiler_params=_params(("parallel", "parallel")),
        name="prenorm",
    )(x2, mod3, mod3, nw, wg, bg)


def _inproj_kernel(a_ref, w_ref, o_ref):
    o_ref[...] = jnp.dot(a_ref[...], w_ref[...], preferred_element_type=F32)


def _inproj(h, w_bf, slab):
    m, d = h.shape
    n = w_bf.shape[1]
    assert n % slab == 0
    tm = _pick(m, (512, 256, 128, 64, 32, 16, 8))
    return pl.pallas_call(
        _inproj_kernel,
        grid=(n // slab, m // tm),
        in_specs=[pl.BlockSpec((tm, d), lambda j, i: (i, 0)),
                  pl.BlockSpec((d, slab), lambda j, i: (0, j))],
        out_specs=pl.BlockSpec((None, tm, slab), lambda j, i: (j, i, 0)),
        out_shape=jax.ShapeDtypeStruct((n // slab, m, slab), F32),
        compiler_params=_params(("parallel", "parallel")),
        name="inproj",
    )(h, w_bf)


CONV_HALO = 32


def _conv_taps(f_ref, w_ref, base, rows, c0, cc, width):
    acc = jnp.zeros((rows, cc), F32)
    for ph in range(SUBLANES):
        if ph >= width:
            break
        n_al = (width - 1 - ph) // SUBLANES + 1
        gb = f_ref[base + ph: base + ph + rows + SUBLANES * (n_al - 1), c0:c0 + cc]
        for a in range(n_al):
            j = SUBLANES * a + ph
            acc = acc + w_ref[j:j + 1, c0:c0 + cc] * gb[SUBLANES * a:SUBLANES * a + rows]
    return acc


def _ln_swish(y, lw, lb):
    mu = jnp.mean(y, -1, keepdims=True)
    yc = y - mu
    yn = yc * lax.rsqrt(jnp.mean(yc * yc, -1, keepdims=True) + EPS) * lw + lb
    return yn * _sigmoid(yn)


def _conv_taps_strided(f_ref, w_ref, cb_ref, y_ref, off, ts, c, width):
    nseg = ts // SUBLANES
    lanes = slice(c * LANES, (c + 1) * LANES)
    wv = [jnp.broadcast_to(w_ref[j:j + 1, lanes], (SUBLANES, LANES)) for j in range(width)]
    acc = [None] * nseg
    for v in range(nseg + width - 1):
        yv = f_ref[c, pl.ds(v + off, SUBLANES, stride=nseg), :]
        for u in range(max(0, v - (width - 1)), min(nseg - 1, v) + 1):
            term = wv[v - u] * yv
            acc[u] = term if acc[u] is None else acc[u] + term
    cb = cb_ref[:, lanes]
    for u in range(nseg):
        y_ref[c, pl.ds(u, SUBLANES, stride=nseg), :] = acc[u] + cb


def _conv_seq_kernel(av_ref, ag_ref, st_ref, w_ref, cb_ref, lw_ref, lb_ref, o_ref, ns_ref, f_ref, y_ref,
                     *, ts, width):
    i = pl.program_id(1)
    off = CONV_HALO - (width - 1)
    nchunk = f_ref.shape[0]
    chunks = [slice(c * LANES, (c + 1) * LANES) for c in range(nchunk)]

    @pl.when(i == 0)
    def _():
        for c, lanes in enumerate(chunks):
            f_ref[c, 0:off, :] = jnp.zeros((off, LANES), F32)
            f_ref[c, off:CONV_HALO, :] = st_ref[:, lanes]

    u = av_ref[...] * _sigmoid(ag_ref[...])
    for c, lanes in enumerate(chunks):
        f_ref[c, CONV_HALO:CONV_HALO + ts, :] = u[:, lanes]
    for c in range(nchunk):
        _conv_taps_strided(f_ref, w_ref, cb_ref, y_ref, off, ts, c, width)
    y = jnp.concatenate([y_ref[c] for c in range(nchunk)], axis=1)
    o_ref[...] = _ln_swish(y, lw_ref[...], lb_ref[...]).astype(o_ref.dtype)

    @pl.when(i == pl.num_programs(1) - 1)
    def _():
        for c, lanes in enumerate(chunks):
            ns_ref[:, lanes] = f_ref[c, ts + off:ts + CONV_HALO, :]

    for c in range(nchunk):
        f_ref[c, 0:CONV_HALO, :] = f_ref[c, ts:ts + CONV_HALO, :]


def _conv_seq(proj, state, w, cb, lw, lb, bsz, s):
    width, ch = w.shape
    assert proj.shape[2] == ch
    ts = _pick(s, (128, 64, 32))
    ns = s // ts
    assert ch % LANES == 0 and ts >= CONV_HALO >= width - 1
    kern = functools.partial(_conv_seq_kernel, ts=ts, width=width)
    vec = pl.BlockSpec((1, ch), lambda b, i: (0, 0))
    return pl.pallas_call(
        kern,
        grid=(bsz, ns),
        in_specs=[pl.BlockSpec((None, ts, ch), lambda b, i: (0, b * ns + i, 0)),
                  pl.BlockSpec((None, ts, ch), lambda b, i: (1, b * ns + i, 0)),
                  pl.BlockSpec((None, width - 1, ch), lambda b, i: (b, 0, 0)),
                  pl.BlockSpec((width, ch), lambda b, i: (0, 0)),
                  vec, vec, vec],
        out_specs=[pl.BlockSpec((ts, ch), lambda b, i: (b * ns + i, 0)),
                   pl.BlockSpec((None, width - 1, ch), lambda b, i: (b, 0, 0))],
        out_shape=[jax.ShapeDtypeStruct((bsz * s, ch), BF16),
                   jax.ShapeDtypeStruct((bsz, width - 1, ch), F32)],
        scratch_shapes=[pltpu.VMEM((ch // LANES, CONV_HALO + ts, LANES), F32),
                        pltpu.VMEM((ch // LANES, ts, LANES), F32)],
        compiler_params=_params(("arbitrary", "arbitrary")),
        name="conv_seq",
    )(proj, proj, state, w, cb, lw, lb)


def _conv_step_kernel(av_ref, ag_ref, st_ref, w_ref, cb_ref, lw_ref, lb_ref, o_ref, ns_ref, f_ref, y_ref,
                      *, nb, s, width, cc):
    nst = width - 1
    ch = f_ref.shape[1]

    def body(q, carry):
        r = pl.multiple_of(q * s, s)
        f_ref[0:nst, :] = st_ref[q]
        f_ref[nst:nst + s, :] = av_ref[pl.ds(r, s), :] * _sigmoid(ag_ref[pl.ds(r, s), :])
        for c0 in range(0, ch, cc):
            acc = _conv_taps(f_ref, w_ref, 0, s, c0, cc, width)
            y_ref[pl.ds(r, s), c0:c0 + cc] = acc + cb_ref[:, c0:c0 + cc]
        ns_ref[q] = f_ref[s:s + nst, :]
        return carry

    lax.fori_loop(0, nb, body, 0)
    o_ref[...] = _ln_swish(y_ref[...], lw_ref[...], lb_ref[...]).astype(o_ref.dtype)


def _conv_step(proj, state, w, cb, lw, lb, bsz, s):
    width, ch = w.shape
    assert proj.shape[2] == ch
    assert s % SUBLANES == 0
    nb = _pick(bsz, (16, 8, 4, 2, 1))
    cc = _pick(ch, (512, 256, 128))
    kern = functools.partial(_conv_step_kernel, nb=nb, s=s, width=width, cc=cc)
    vec = pl.BlockSpec((1, ch), lambda b: (0, 0))
    frows = -(-(width - 1 + s) // SUBLANES) * SUBLANES
    return pl.pallas_call(
        kern,
        grid=(bsz // nb,),
        in_specs=[pl.BlockSpec((None, nb * s, ch), lambda b: (0, b, 0)),
                  pl.BlockSpec((None, nb * s, ch), lambda b: (1, b, 0)),
                  pl.BlockSpec((nb, width - 1, ch), lambda b: (b, 0, 0)),
                  pl.BlockSpec((width, ch), lambda b: (0, 0)),
                  vec, vec, vec],
        out_specs=[pl.BlockSpec((nb * s, ch), lambda b: (b, 0)),
                   pl.BlockSpec((nb, width - 1, ch), lambda b: (b, 0, 0))],
        out_shape=[jax.ShapeDtypeStruct((bsz * s, ch), BF16),
                   jax.ShapeDtypeStruct((bsz, width - 1, ch), F32)],
        scratch_shapes=[pltpu.VMEM((frows, ch), F32), pltpu.VMEM((nb * s, ch), F32)],
        compiler_params=_params(("parallel",)),
        name="conv_step",
    )(proj, proj, state, w, cb, lw, lb)


def _mlstm_kernel(qk_ref, v_ref, o_ref, g_ref, gt_ref, c0_ref, n0_ref, m0_ref, nw_ref,
                  hm_ref, c_out, n_out, m_out, c_s, n_s, m_s, *, n_heads, scale, nb):
    c = pl.program_id(1)
    last = pl.num_programs(1) - 1
    ln = qk_ref.shape[0] // nb
    wqk = qk_ref.shape[1] // 2
    dqk = wqk // n_heads
    dv = v_ref.shape[1] // n_heads

    @pl.when(c == 0)
    def _():
        c_s[...] = c0_ref[...]
        n_s[...] = n0_ref[...]
        m_s[...] = m0_ref[...]

    row = lax.broadcasted_iota(I32, (ln, ln), 0)
    col = lax.broadcasted_iota(I32, (ln, ln), 1)
    tri = row >= col
    tri_t = row <= col

    def chain(bi, hd):
        rows = slice(bi * ln, (bi + 1) * ln)
        g = g_ref[rows, :]
        gt = gt_ref[bi]
        li_c = g[:, hd:hd + 1]
        lf_c = g[:, n_heads + hd:n_heads + hd + 1]
        li_r = gt[hd:hd + 1, :]
        lf_r = gt[n_heads + hd:n_heads + hd + 1, :]
        b_c = jnp.sum(jnp.where(tri, lf_r, 0.0), axis=1, keepdims=True)
        b_r = jnp.sum(jnp.where(tri_t, lf_c, 0.0), axis=0, keepdims=True)
        b_l = jnp.sum(lf_r, axis=1, keepdims=True)
        m_prev = m_s[bi, :, hd:hd + 1]

        dmat = jnp.where(tri, b_c - b_r + li_r, NEG_BIG)
        inter = b_c + m_prev
        m_t = jnp.maximum(inter, jnp.max(dmat, axis=1, keepdims=True))
        a = jnp.exp(inter - m_t)

        q = qk_ref[rows, hd * dqk:(hd + 1) * dqk]
        k = qk_ref[rows, wqk + hd * dqk:wqk + (hd + 1) * dqk] * scale
        qb = q.astype(BF16)
        vb = v_ref[rows, hd * dv:(hd + 1) * dv].astype(BF16)
        cst = c_s[bi, hd]
        nst = n_s[bi, hd:hd + 1, :]
        s = lax.dot_general(qb, k.astype(BF16), (((1,), (1,)), ((), ())), preferred_element_type=F32)
        s = s * jnp.exp(dmat - m_t)
        num = a * jnp.dot(qb, cst.astype(BF16), preferred_element_type=F32) \
            + jnp.dot(s.astype(BF16), vb, preferred_element_type=F32)
        den = a * jnp.sum(q * nst, axis=1, keepdims=True) + jnp.sum(s, axis=1, keepdims=True)
        h = num / jnp.maximum(jnp.abs(den), jnp.exp(-m_t))
        hn = h * lax.rsqrt(jnp.mean(h * h, -1, keepdims=True) + EPS) * nw_ref[hd:hd + 1, :]
        gate = _sigmoid(o_ref[rows, hd * dv:(hd + 1) * dv])
        hm_ref[rows, hd * dv:(hd + 1) * dv] = (hn * gate).astype(hm_ref.dtype)

        g_r = b_l - b_r + li_r
        g_c = b_l - b_c + li_c
        m_new = jnp.maximum(b_l + m_prev, jnp.max(g_r, axis=1, keepdims=True))
        decay = jnp.exp(b_l + m_prev - m_new)
        kw = k * jnp.exp(g_c - m_new)
        c_new = decay * cst + lax.dot_general(kw.astype(BF16), vb, (((0,), (0,)), ((), ())),
                                              preferred_element_type=F32)
        n_new = decay * nst + jnp.sum(kw, axis=0, keepdims=True)
        c_s[bi, hd] = c_new
        n_s[bi, hd:hd + 1, :] = n_new
        m_s[bi, :, hd:hd + 1] = m_new

    for bi in range(nb):
        for hd in range(n_heads):
            chain(bi, hd)

    @pl.when(c == last)
    def _():
        c_out[...] = c_s[...]
        n_out[...] = n_s[...]
        m_out[...] = m_s[...]


def _mlstm(proj, g, c0, n0, m0, norm_w, bsz, s, slab0):
    _, n_heads, dqk, dv = c0.shape
    ln = s if s <= MLSTM_CHUNK else MLSTM_CHUNK
    assert s % ln == 0 and ln % SUBLANES == 0
    nc = s // ln
    wqk, wv = n_heads * dqk, n_heads * dv
    slab = proj.shape[2]
    assert 2 * wqk == slab and wv == slab
    gt3 = g[:, :2 * n_heads].reshape(bsz * nc, ln, 2 * n_heads).transpose(0, 2, 1)
    nb = _pick(bsz, (MLSTM_SEQS, 2, 1)) if nc == 1 else 1
    kern = functools.partial(_mlstm_kernel, n_heads=n_heads, scale=dqk ** -0.5, nb=nb)
    rows = lambda col: (lambda b, c: (b * nc + c, col))
    slab_rows = lambda k: pl.BlockSpec((None, nb * ln, slab), lambda b, c: (slab0 + k, b * nc + c, 0))
    per_seq = lambda shape: pl.BlockSpec((nb,) + shape, lambda b, c: (b,) + (0,) * len(shape))
    out = pl.pallas_call(
        kern,
        grid=(bsz // nb, nc),
        in_specs=[slab_rows(0), slab_rows(1), slab_rows(2),
                  pl.BlockSpec((nb * ln, LANES), rows(0)),
                  pl.BlockSpec((nb, 2 * n_heads, ln), lambda b, c: (b * nc + c, 0, 0)),
                  per_seq((n_heads, dqk, dv)), per_seq((n_heads, dqk)), per_seq((1, n_heads)),
                  pl.BlockSpec((n_heads, dv), lambda b, c: (0, 0))],
        out_specs=[pl.BlockSpec((nb * ln, wv), rows(0)),
                   per_seq((n_heads, dqk, dv)), per_seq((n_heads, dqk)), per_seq((1, n_heads))],
        out_shape=[jax.ShapeDtypeStruct((bsz * s, wv), BF16),
                   jax.ShapeDtypeStruct((bsz, n_heads, dqk, dv), F32),
                   jax.ShapeDtypeStruct((bsz, n_heads, dqk), F32),
                   jax.ShapeDtypeStruct((bsz, 1, n_heads), F32)],
        scratch_shapes=[pltpu.VMEM((nb, n_heads, dqk, dv), F32), pltpu.VMEM((nb, n_heads, dqk), F32),
                        pltpu.VMEM((nb, 1, n_heads), F32)],
        compiler_params=_params(("arbitrary", "arbitrary")),
        name="mlstm",
    )(proj, proj, proj, g, gt3, c0, n0, m0.reshape(bsz, 1, n_heads), norm_w)
    hm, c_new, n_new, m_new = out
    return hm, c_new, n_new, m_new.reshape(bsz, n_heads)


def _outproj_kernel(co_ref, hm_ref, w_ref, o_ref):
    kc = co_ref.shape[1]
    o_ref[...] = jnp.dot(co_ref[...], w_ref[0:kc, :], preferred_element_type=F32) \
        + jnp.dot(hm_ref[...], w_ref[kc:2 * kc, :], preferred_element_type=F32)


def _outproj(co, hm, w_bf):
    m, kc = co.shape
    d = w_bf.shape[-1]
    assert hm.shape[1] == kc and w_bf.shape[0] == 2 * kc
    tn = _pick(d, (2048, 1024, 512, 256, 128))
    tm = _pick(m, (512, 256, 128, 64, 32, 16, 8))
    return pl.pallas_call(
        _outproj_kernel,
        grid=(d // tn, m // tm),
        in_specs=[pl.BlockSpec((tm, kc), lambda j, i: (i, 0)),
                  pl.BlockSpec((tm, kc), lambda j, i: (i, 0)),
                  pl.BlockSpec((2 * kc, tn), lambda j, i: (0, j))],
        out_specs=pl.BlockSpec((tm, tn), lambda j, i: (i, j)),
        out_shape=jax.ShapeDtypeStruct((m, d), F32),
        compiler_params=_params(("parallel", "parallel")),
        name="outproj",
    )(co, hm, w_bf)


def _router_kernel(mix_ref, x_ref, g1_ref, sc2_ref, sh2_ref, n1_ref, n2_ref, wr_ref, br_ref,
                   x1_ref, h2_ref, ti_ref, tg_ref, *, st):
    mix = mix_ref[...]
    mn = mix * lax.rsqrt(jnp.mean(mix * mix, -1, keepdims=True) + EPS) * n1_ref[...]
    x1 = x_ref[...] + _mod_rows(g1_ref, st) * mn
    x1_ref[...] = x1
    y2 = x1 * lax.rsqrt(jnp.mean(x1 * x1, -1, keepdims=True) + EPS) * n2_ref[...]
    h2 = y2 * (1.0 + _mod_rows(sc2_ref, st)) + _mod_rows(sh2_ref, st)
    half = h2.shape[1] // 2
    h2_ref[...] = _pack_pair(h2[:, :half], h2[:, half:])
    logits = jnp.dot(h2.astype(BF16), wr_ref[...], preferred_element_type=F32) + br_ref[...]
    n_exp = logits.shape[1]
    lane = lax.broadcasted_iota(I32, logits.shape, 1)
    lane_o = lax.broadcasted_iota(I32, ti_ref.shape, 1)
    idx_out = jnp.zeros(ti_ref.shape, I32)
    val_out = jnp.zeros(tg_ref.shape, F32)
    top = None
    den = jnp.zeros((logits.shape[0], 1), F32)
    for r in range(TOP_K):
        mx = jnp.max(logits, axis=1, keepdims=True)
        ix = jnp.min(jnp.where(logits == mx, lane, n_exp), axis=1, keepdims=True)
        if top is None:
            top = mx
        e = jnp.exp(mx - top)
        den = den + e
        idx_out = jnp.where(lane_o == r, ix, idx_out)
        val_out = jnp.where(lane_o == r, e, val_out)
        logits = jnp.where(lane == ix, NEG_BIG, logits)
    ti_ref[...] = idx_out
    tg_ref[...] = val_out / den


def _router(mix, x2, mod3, grp, n1, n2, wr, br):
    rows, d = x2.shape
    n_exp = wr.shape[1]
    kern = functools.partial(_router_kernel, st=grp.st)
    return pl.pallas_call(
        kern,
        grid=grp.grid,
        in_specs=[grp.rows(d), grp.rows(d), grp.mod(d, 2), grp.mod(d, 4), grp.mod(d, 3),
                  grp.const((1, d)), grp.const((1, d)), grp.const((d, n_exp)), grp.const((1, n_exp))],
        out_specs=[grp.rows(d), grp.rows(d // 2), grp.rows(LANES), grp.rows(LANES)],
        out_shape=[jax.ShapeDtypeStruct((rows, d), F32), jax.ShapeDtypeStruct((rows, d // 2), U32),
                   jax.ShapeDtypeStruct((rows, LANES), I32), jax.ShapeDtypeStruct((rows, LANES), F32)],
        compiler_params=_params(("parallel", "parallel")),
        name="router",
    )(mix, x2, mod3, mod3, mod3, n1, n2, wr, br)


def _rank_kernel(ti_ref, ps_ref, o_ref, carry_ref):
    @pl.when(pl.program_id(0) == 0)
    def _():
        carry_ref[...] = jnp.zeros(carry_ref.shape, F32)

    ti = ti_ref[...].astype(F32)
    tt = ti.shape[0]
    lane = lax.broadcasted_iota(I32, ti.shape, 1)
    lane_f = lane.astype(F32)
    cols = []
    member = jnp.zeros(ti.shape, F32)
    for kk in range(TOP_K):
        ek = jnp.sum(jnp.where(lane == kk, ti, 0.0), axis=1, keepdims=True)
        cols.append(ek)
        member = member + jnp.where(lane_f == ek, 1.0, 0.0)
    r = lax.broadcasted_iota(I32, (tt, tt), 0)
    c = lax.broadcasted_iota(I32, (tt, tt), 1)
    before = jnp.where(r > c, 1.0, 0.0).astype(BF16)
    base = jnp.dot(before, member.astype(BF16), preferred_element_type=F32) + carry_ref[...] + ps_ref[...]
    out = jnp.zeros(o_ref.shape, I32)
    for kk in range(TOP_K):
        dk = jnp.sum(jnp.where(lane_f == cols[kk], base, 0.0), axis=1, keepdims=True)
        out = jnp.where(lane == kk, dk.astype(I32), out)
    o_ref[...] = out
    carry_ref[...] += jnp.sum(member, axis=0, keepdims=True)


def _rank(top_i, pad_start):
    t = top_i.shape[0]
    tt = _pick(t, (512, 256, 128, 64, 32, 16, 8))
    return pl.pallas_call(
        _rank_kernel,
        grid=(t // tt,),
        in_specs=[pl.BlockSpec((tt, LANES), lambda i: (i, 0)), pl.BlockSpec((1, LANES), lambda i: (0, 0))],
        out_specs=pl.BlockSpec((tt, LANES), lambda i: (i, 0)),
        out_shape=jax.ShapeDtypeStruct((t, LANES), I32),
        scratch_shapes=[pltpu.VMEM((1, LANES), F32)],
        compiler_params=_params(("arbitrary",)),
        name="moe_rank",
    )(top_i, pad_start)


def _route(top_i, n_experts, tm):
    t = top_i.shape[0]
    a = t * TOP_K
    flat_e = top_i[:, :TOP_K].reshape(a)
    counts = jnp.sum((flat_e[:, None] == jnp.arange(n_experts, dtype=I32)[None, :]).astype(I32), axis=0)
    padded = (counts + tm - 1) // tm * tm
    pad_end = jnp.cumsum(padded)
    pad_start = pad_end - padded
    nt = a // tm + n_experts
    tile_start = jnp.arange(nt, dtype=I32) * tm
    tile_u = (tile_start < pad_end[-1]).astype(I32)
    n_used = jnp.sum(tile_u)
    tile_e = jnp.sum((pad_end[None, :] <= tile_start[:, None]).astype(I32), axis=1)
    tile_e = jnp.minimum(tile_e, n_experts - 1)
    tile_e = jnp.where(tile_u == 1, tile_e, tile_e[jnp.maximum(n_used - 1, 0)])
    idx = jnp.arange(nt, dtype=I32)
    starts = jnp.logical_and(tile_u == 1, jnp.logical_or(idx == 0, tile_e != jnp.roll(tile_e, 1)))
    seg = (jnp.cumsum(starts.astype(I32)) - 1).astype(I32)
    later = lax.cummin(jnp.where(starts, idx, nt)[::-1])[::-1]
    nxt = jnp.concatenate([later[1:], jnp.full((1,), nt, I32)])
    next_e = tile_e[jnp.where(nxt >= nt, 0, nxt)]
    n_seg = jnp.sum(starts.astype(I32)).reshape(1)
    x_tile = jnp.minimum(idx, jnp.maximum(n_used - 1, 0))
    tiles = (tile_e, tile_u, seg, next_e, n_seg, x_tile)
    fill = (jnp.concatenate([pad_start + counts, pad_end[-1:]]).astype(I32),
            jnp.concatenate([padded - counts, nt * tm - pad_end[-1:]]).astype(I32))
    ps = jnp.zeros((1, LANES), F32).at[0, :n_experts].set(pad_start.astype(F32))
    dest = _rank(top_i, ps)
    return dest, tiles, fill, nt


def _row_copy(src_ref, dst_ref, sem, src_row, dst_row):
    return pltpu.make_async_copy(src_ref.at[pl.ds(src_row, 1), :], dst_ref.at[pl.ds(dst_row, 1), :], sem)


def _dispatch_kernel(fs_ref, fn_ref, dest_ref, srcp_ref, srcs_ref, xs_out, zrow_ref, sem, *, ntp, nts, tm):
    i = pl.program_id(0)

    def send(src_ref):
        def start(r, carry):
            for kk in range(TOP_K):
                _row_copy(src_ref, xs_out, sem, r, dest_ref[0, r * TOP_K + kk]).start(priority=kk % 2)
            return carry

        lax.fori_loop(0, src_ref.shape[0], start, 0, unroll=ISSUE_UNROLL)
        for _ in range(TOP_K):
            pltpu.make_async_copy(src_ref, src_ref, sem).wait()

    @pl.when(i < ntp)
    def _():
        send(srcp_ref)

    @pl.when(jnp.logical_and(i >= ntp, i < ntp + nts))
    def _():
        send(srcs_ref)

    f = i - (ntp + nts)
    n_fill = pl.num_programs(0) - (ntp + nts)

    @pl.when(f == 0)
    def _():
        zrow_ref[...] = jnp.zeros(zrow_ref.shape, zrow_ref.dtype)

    def zero_rows(row, size):
        dst = row if size < SUBLANES else pl.multiple_of(row, SUBLANES)
        return pltpu.make_async_copy(zrow_ref.at[pl.ds(0, size), :], xs_out.at[pl.ds(dst, size), :], sem)

    def pad_run(row0, n, go):
        odd = n & (SUBLANES - 1)
        for r in range(SUBLANES - 1):
            @pl.when(odd > r)
            def _():
                go(zero_rows(row0 + r, 1))
        pos = row0 + odd
        size = SUBLANES
        while size < tm:
            bit = (n // size) & 1

            @pl.when(bit == 1)
            def _():
                go(zero_rows(pos, size))
            pos = pos + bit * size
            size *= 2

    def tail_run(row0, n, go):
        def body(t, carry):
            go(zero_rows(row0 + t * tm, tm))
            return carry
        lax.fori_loop(0, n // tm, body, 0)

    for go in (lambda cp: cp.start(), lambda cp: cp.wait()):
        @pl.when(jnp.logical_and(f >= 0, f < n_fill - 1))
        def _():
            pad_run(fs_ref[f], fn_ref[f], go)

        @pl.when(f == n_fill - 1)
        def _():
            tail_run(fs_ref[f], fn_ref[f], go)


def _dispatch(h2p_p, h2p_s, dest, fill, rows, tm):
    fill_start, fill_n = fill
    tp, w = h2p_p.shape
    tsm = h2p_s.shape[0]
    tt = _pick(tsm, (256, 128, 64, 32, 16, 8))
    assert tp % tt == 0
    ntp, nts = tp // tt, tsm // tt
    kern = functools.partial(_dispatch_kernel, ntp=ntp, nts=nts, tm=tm)
    grid_spec = pltpu.PrefetchScalarGridSpec(
        num_scalar_prefetch=2,
        grid=(ntp + nts + fill_n.shape[0],),
        in_specs=[pl.BlockSpec((None, 1, tt * TOP_K), lambda i, *_: (jnp.minimum(i, ntp + nts - 1), 0, 0),
                               memory_space=pltpu.SMEM),
                  pl.BlockSpec((tt, w), lambda i, *_: (jnp.minimum(i, ntp - 1), 0)),
                  pl.BlockSpec((tt, w), lambda i, *_: (jnp.clip(i - ntp, 0, nts - 1), 0))],
        out_specs=pl.BlockSpec(memory_space=pl.ANY),
        scratch_shapes=[pltpu.VMEM((tm, w), U32), pltpu.SemaphoreType.DMA(())])
    return pl.pallas_call(
        kern,
        grid_spec=grid_spec,
        out_shape=jax.ShapeDtypeStruct((rows, w), U32),
        compiler_params=_params(("arbitrary",)),
        name="moe_dispatch",
    )(fill_start, fill_n, dest[:, :TOP_K].reshape(ntp + nts, 1, tt * TOP_K), h2p_p, h2p_s)


N_TILE_TABLES = 6


def _segment_weights(tables, w_hbm, wst_ref, sem, *, l, tn, col_offs):
    te_ref, tu_ref, sg_ref, ne_ref, ns_ref = tables[:5]
    j = pl.program_id(0)
    i = pl.program_id(1)
    n_seg = ns_ref[0]
    first = jnp.logical_and(tu_ref[i] == 1,
                            jnp.logical_or(i == 0, te_ref[i] != te_ref[jnp.maximum(i - 1, 0)]))
    g = j * n_seg + sg_ref[i]

    def copies(e, jj):
        return [pltpu.make_async_copy(
            w_hbm.at[l, e, :, pl.ds(pl.multiple_of((off + jj) * tn, tn), tn)], wst_ref.at[m], sem)
            for m, off in enumerate(col_offs)]

    @pl.when(jnp.logical_and(j == 0, i == 0))
    def _():
        for cp in copies(te_ref[0], 0):
            cp.start(priority=WEIGHT_DMA_PRIORITY)

    def wait_weights():
        for cp in copies(te_ref[i], j):
            cp.wait()

    def start_next():
        @pl.when(g + 1 < pl.num_programs(0) * n_seg)
        def _():
            jn = jnp.where(sg_ref[i] + 1 == n_seg, j + 1, j)
            for cp in copies(ne_ref[i], jn):
                cp.start(priority=WEIGHT_DMA_PRIORITY)

    return first, wait_weights, start_next


def _gmm1_kernel(*refs, l, tn, nj):
    tables = refs[:N_TILE_TABLES]
    x_ref, w_hbm, bg_ref, bu_ref, o_ref, wst_ref, wgb_ref, wub_ref, sem = refs[N_TILE_TABLES:]
    used = tables[1][pl.program_id(1)] == 1
    first, wait_weights, start_next = _segment_weights(tables, w_hbm, wst_ref, sem, l=l, tn=tn,
                                                       col_offs=(0, nj))
    half = x_ref.shape[1]

    def tile(wg_lo, wg_hi, wu_lo, wu_hi):
        xlo, xhi = _unpack_pair(x_ref[...])
        gt = jnp.dot(xlo, wg_lo, preferred_element_type=F32) \
            + jnp.dot(xhi, wg_hi, preferred_element_type=F32) + bg_ref[...]
        up = jnp.dot(xlo, wu_lo, preferred_element_type=F32) \
            + jnp.dot(xhi, wu_hi, preferred_element_type=F32) + bu_ref[...]
        gt = jnp.minimum(gt, SWIGLU_LIMIT)
        up = jnp.clip(up, -SWIGLU_LIMIT, SWIGLU_LIMIT)
        act = (up + 1.0) * gt * _sigmoid(SWIGLU_ALPHA * gt)
        o_ref[...] = act.astype(o_ref.dtype)

    @pl.when(first)
    def _():
        wait_weights()
        ws = []
        for m, dst in enumerate((wgb_ref, wub_ref)):
            for k0 in (0, half):
                wk = wst_ref[m, k0:k0 + half, :].astype(BF16)
                dst[k0:k0 + half, :] = wk
                ws.append(wk)
        tile(*ws)
        start_next()

    @pl.when(jnp.logical_and(used, jnp.logical_not(first)))
    def _():
        tile(wgb_ref[0:half, :], wgb_ref[half:2 * half, :], wub_ref[0:half, :], wub_ref[half:2 * half, :])

    @pl.when(jnp.logical_not(used))
    def _():
        o_ref[...] = jnp.zeros(o_ref.shape, o_ref.dtype)


def _gmm1(xs, w1, b1, tiles, l, tm):
    rows, half = xs.shape
    d = 2 * half
    dff = w1.shape[-1] // 2
    nt = rows // tm
    tn = _pick(dff, (1024, 512, 256, 128))
    nj = dff // tn
    b1r = b1.reshape(b1.shape[0], b1.shape[1], 1, 2 * dff)
    kern = functools.partial(_gmm1_kernel, l=l, tn=tn, nj=nj)
    grid_spec = pltpu.PrefetchScalarGridSpec(
        num_scalar_prefetch=N_TILE_TABLES,
        grid=(nj, nt),
        in_specs=[pl.BlockSpec((tm, half), lambda j, i, *t: (t[5][i], 0)),
                  pl.BlockSpec(memory_space=pl.ANY),
                  pl.BlockSpec((None, None, 1, tn), lambda j, i, te, *t: (l, te[i], 0, j)),
                  pl.BlockSpec((None, None, 1, tn), lambda j, i, te, *t: (l, te[i], 0, nj + j))],
        out_specs=pl.BlockSpec((tm, tn), lambda j, i, *t: (i, j)),
        scratch_shapes=[pltpu.VMEM((2, d, tn), F32), pltpu.VMEM((d, tn), BF16), pltpu.VMEM((d, tn), BF16),
                        pltpu.SemaphoreType.DMA(())])
    return pl.pallas_call(
        kern,
        grid_spec=grid_spec,
        out_shape=jax.ShapeDtypeStruct((rows, dff), BF16),
        compiler_params=_params(("arbitrary", "arbitrary"), GMM_VMEM_LIMIT),
        name="moe_gmm1",
    )(*tiles, xs, w1, b1r, b1r)


def _gmm2_kernel(*refs, l, tn):
    tables = refs[:N_TILE_TABLES]
    a_ref, w_hbm, b_ref, o_ref, wst_ref, wb_ref, sem = refs[N_TILE_TABLES:]
    used = tables[1][pl.program_id(1)] == 1
    first, wait_weights, start_next = _segment_weights(tables, w_hbm, wst_ref, sem, l=l, tn=tn, col_offs=(0,))

    def tile(w):
        y = jnp.dot(a_ref[...], w, preferred_element_type=F32) + b_ref[...]
        o_ref[...] = _pack_pair(y[:, :tn // 2], y[:, tn // 2:])

    @pl.when(first)
    def _():
        wait_weights()
        w = wst_ref[0].astype(BF16)
        wb_ref[...] = w
        tile(w)
        start_next()

    @pl.when(jnp.logical_and(used, jnp.logical_not(first)))
    def _():
        tile(wb_ref[...])

    @pl.when(jnp.logical_not(used))
    def _():
        o_ref[...] = jnp.zeros(o_ref.shape, o_ref.dtype)


def _gmm2(act, w2, b2, tiles, l, tm):
    rows, dff = act.shape
    d = w2.shape[-1]
    nt = rows // tm
    tn = _pick(d, (4096, 2048, 1024, 512, 256, 128))
    b2r = b2.reshape(b2.shape[0], b2.shape[1], 1, d)
    kern = functools.partial(_gmm2_kernel, l=l, tn=tn)
    grid_spec = pltpu.PrefetchScalarGridSpec(
        num_scalar_prefetch=N_TILE_TABLES,
        grid=(d // tn, nt),
        in_specs=[pl.BlockSpec((tm, dff), lambda j, i, *t: (i, 0)),
                  pl.BlockSpec(memory_space=pl.ANY),
                  pl.BlockSpec((None, None, 1, tn), lambda j, i, te, *t: (l, te[i], 0, j))],
        out_specs=pl.BlockSpec((tm, tn // 2), lambda j, i, *t: (i, j)),
        scratch_shapes=[pltpu.VMEM((1, dff, tn), F32), pltpu.VMEM((dff, tn), BF16),
                        pltpu.SemaphoreType.DMA(())])
    y = pl.pallas_call(
        kern,
        grid_spec=grid_spec,
        out_shape=jax.ShapeDtypeStruct((rows, d // 2), U32),
        compiler_params=_params(("arbitrary", "arbitrary"), GMM_VMEM_LIMIT),
        name="moe_gmm2",
    )(*tiles, act, w2, b2r)
    return y, tn // 2


def _combine_kernel(pos_ref, posn_ref, y_hbm, tg_ref, x1_ref, g2_ref, nw_ref, o_ref, buf_ref, sem, *, st, pw):
    ni = pl.num_programs(1)
    n = pl.program_id(0) * ni + pl.program_id(1)
    total = pl.num_programs(0) * ni
    tt, d = o_ref.shape
    slot = lax.rem(n, 2)

    def start_rows(idx_ref, sl):
        def body(r, carry):
            for kk in range(TOP_K):
                cp = _row_copy(y_hbm, buf_ref.at[sl, kk], sem.at[sl], idx_ref[0, r * TOP_K + kk], r)
                cp.start(priority=kk % 2)
            return carry
        lax.fori_loop(0, tt, body, 0, unroll=ISSUE_UNROLL)

    @pl.when(n == 0)
    def _():
        start_rows(pos_ref, 0)

    @pl.when(n + 1 < total)
    def _():
        start_rows(posn_ref, 1 - slot)

    pltpu.make_async_copy(buf_ref.at[slot], buf_ref.at[slot], sem.at[slot]).wait()

    rc = COMBINE_ROWS
    nbq = g2_ref.shape[0]
    nw = nw_ref[...]
    lane = lax.broadcasted_iota(I32, (rc, LANES), 1)
    for c in range(tt // rc):
        rows = slice(c * rc, (c + 1) * rc)
        gates = tg_ref[rows, :]
        gk = [jnp.sum(jnp.where(lane == kk, gates, 0.0), axis=1, keepdims=True) for kk in range(TOP_K)]
        ssq = jnp.zeros((rc, 1), F32)
        parts = []
        for blk in range(d // (2 * pw)):
            lo = jnp.zeros((rc, pw), F32)
            hi = jnp.zeros((rc, pw), F32)
            for kk in range(TOP_K):
                p = buf_ref[slot, kk, rows, blk * pw:(blk + 1) * pw]
                lo = lo + gk[kk] * lax.bitcast_convert_type(p << 16, F32)
                hi = hi + gk[kk] * lax.bitcast_convert_type(p & jnp.uint32(HI_HALF), F32)
            ssq = ssq + jnp.sum(lo * lo, -1, keepdims=True) + jnp.sum(hi * hi, -1, keepdims=True)
            parts += [lo, hi]
        f = jnp.concatenate(parts, axis=1)
        if nbq == 1:
            g2 = g2_ref[0]
        else:
            nq = rc // st
            g2 = jnp.broadcast_to(g2_ref[c * nq:(c + 1) * nq], (nq, st, d)).reshape(rc, d)
        o_ref[rows, :] = x1_ref[rows, :] + g2 * (f * lax.rsqrt(ssq / d + EPS) * nw)


def _combine(y, pw, dest, tg, x1, mod3, grp, nw):
    rows, d = x1.shape
    tt = grp.ts
    nt = rows // tt
    ni = grp.n_inner
    kern = functools.partial(_combine_kernel, st=grp.st, pw=pw)
    dest3 = dest[:, :TOP_K].reshape(nt, 1, tt * TOP_K)
    idx_spec = lambda ahead: pl.BlockSpec(
        (None, 1, tt * TOP_K), lambda o, i: (jnp.minimum(o * ni + i + ahead, nt - 1), 0, 0),
        memory_space=pltpu.SMEM)
    return pl.pallas_call(
        kern,
        grid=grp.grid,
        in_specs=[idx_spec(0), idx_spec(1),
                  pl.BlockSpec(memory_space=pl.ANY),
                  grp.rows(LANES), grp.rows(d), grp.mod(d, 5), grp.const((1, d))],
        out_specs=grp.rows(d),
        out_shape=jax.ShapeDtypeStruct((rows, d), F32),
        scratch_shapes=[pltpu.VMEM((2, TOP_K, tt, d // 2), U32), pltpu.SemaphoreType.DMA((2,))],
        compiler_params=_params(("arbitrary", "arbitrary")),
        name="moe_combine",
    )(dest3, dest3, y, tg, x1, mod3, nw)


def _mixer(x2, mod3, grp, conv_buf, c0, n0, m0, p, l):
    n_heads = c0.shape[1]
    ch = p['conv_w'].shape[-1]
    h, g = _prenorm(x2, mod3, grp, p['norm1_pre'], p['w_gate'], p['b_gate'], n_heads)
    proj = _inproj(h, p['w_in'], ch)
    conv = _conv_seq if grp.nbq == 1 else _conv_step
    co, new_buf = conv(proj, conv_buf, p['conv_w'], p['conv_b'], p['conv_ln_w'], p['conv_ln_b'], grp.bsz, grp.s)
    hm, c_new, n_new, m_new = _mlstm(proj, g, c0, n0, m0, p['mlstm_norm_w'], grp.bsz, grp.s, 2)
    mix = _outproj(co, hm, p['w_out'])
    x1, h2p, ti, tg = _router(mix, x2, mod3, grp, p['norm1_post'], p['norm2_pre'], p['w_router'], p['b_router'])
    return dict(x1=x1, h2p=h2p, ti=ti, tg=tg, state=(new_buf, c_new, n_new, m_new))


def kernel(x_prompt, x_sample, c_prompt, c_sample, state_conv, state_mlstm_C, state_mlstm_n, state_mlstm_m,
           w_ada, b_ada, norm1_pre, w_in, b_gates, conv_w, conv_b, conv_ln_w, conv_ln_b, mlstm_norm_w,
           w_out, norm1_post, norm2_pre, w_router, b_router, w1, b1, w2, b2, norm2_post):
    depth = w_ada.shape[0]
    bp, sp, d = x_prompt.shape
    bs, ss, _ = x_sample.shape
    n_heads, dqk, dv = state_mlstm_C.shape[2:]
    n_experts = w_router.shape[-1]
    nst, ch = state_conv.shape[2:]
    n_gate = 2 * n_heads
    assert n_gate <= LANES and n_experts <= LANES
    tp, tsmp = bp * sp, bs * ss

    mp = -(-(bp + bs) // SUBLANES) * SUBLANES
    c_all = jnp.zeros((mp, d), F32).at[:bs].set(c_sample).at[bs:bs + bp].set(c_prompt)
    grp_p = _Group(bp, sp, bs, ROW_TILE)
    grp_s = _Group(bs, ss, 0, ROW_TILE)

    xp, xs = x_prompt.reshape(tp, d), x_sample.reshape(tsmp, d)
    outs = [[] for _ in range(8)]
    for l in range(depth):
        row = lambda v: v[l].reshape(1, -1)
        p = dict(
            w_in=w_in[l, :, :w_in.shape[-1] - n_gate].astype(BF16), w_out=w_out[l].astype(BF16),
            w_gate=jnp.pad(w_in[l, :, w_in.shape[-1] - n_gate:], ((0, 0), (0, LANES - n_gate))),
            b_gate=jnp.pad(b_gates[l], (0, LANES - n_gate)).reshape(1, LANES),
            norm1_pre=row(norm1_pre), conv_w=conv_w[l], conv_b=row(conv_b), conv_ln_w=row(conv_ln_w),
            conv_ln_b=row(conv_ln_b), mlstm_norm_w=mlstm_norm_w[l], norm1_post=row(norm1_post),
            norm2_pre=row(norm2_pre), w_router=w_router[l].astype(BF16), b_router=row(b_router))
        mod3 = _ada(c_all, w_ada, b_ada, l).reshape(mp, 1, 6 * d)
        zero = lambda shape: jnp.zeros(shape, F32)
        gp = _mixer(xp, mod3, grp_p, zero((bp, nst, ch)), zero((bp, n_heads, dqk, dv)),
                    zero((bp, n_heads, dqk)), zero((bp, n_heads)), p, l)
        gs = _mixer(xs, mod3, grp_s, state_conv[l], state_mlstm_C[l], state_mlstm_n[l],
                    state_mlstm_m[l], p, l)

        top_i = jnp.concatenate([gp['ti'], gs['ti']], axis=0)
        dest, tiles, fill, nt = _route(top_i, n_experts, MOE_ROWS)
        xsorted = _dispatch(gp['h2p'], gs['h2p'], dest, fill, nt * MOE_ROWS, MOE_ROWS)
        act = _gmm1(xsorted, w1, b1, tiles, l, MOE_ROWS)
        y, pw = _gmm2(act, w2, b2, tiles, l, MOE_ROWS)
        nw2 = row(norm2_post)
        xp = _combine(y, pw, dest[:tp], gp['tg'], gp['x1'], mod3, grp_p, nw2)
        xs = _combine(y, pw, dest[tp:], gs['tg'], gs['x1'], mod3, grp_s, nw2)
        for o, v in zip(outs, gp['state'] + gs['state']):
            o.append(v)
    stack = (lambda o: o[0][None]) if depth == 1 else jnp.stack
    return (xp.reshape(bp, sp, d), xs.reshape(bs, ss, d)) + tuple(stack(o) for o in outs)
```
